```python
import jax, jax.numpy as jnp
from jax import lax
import numpy as np

D_MODEL = 1024
BATCH = 8
SEQ = 2048
DEPTH = 2
DEC_BATCH = 128
DEC_SEQ = 8
PAST_LEN = 16384
PAGE_SIZE = 128

POOL_WINDOWS = (2, 4, 8, 16)
N_POOL_GROUPS = len(POOL_WINDOWS)
D_POOL = D_MODEL // 2
POOL_GROUP = D_POOL // N_POOL_GROUPS
POOL_BUF = max(POOL_WINDOWS) - 1
D_RNN = D_MODEL
N_RNN_BLOCKS = 8
RNN_BLOCK = D_RNN // N_RNN_BLOCKS
RNN_CONV = 4
RG_C = 8.0
N_HEADS = 4
HEAD_DIM = 128
D_Q = N_HEADS * HEAD_DIM
N_MEM = 256
D_MIX = D_POOL + D_RNN + D_Q
N_BRANCH = 3
D_FF = 3 * D_MODEL
FFN_CONV = 3
EPS = 1e-6

kernel_name = "hybrid_pool_rglru_memxattn_decoder_step"


def rmsnorm(x, g):
    xf = x.astype(jnp.float32)
    y = xf * lax.rsqrt(jnp.mean(xf * xf, axis=-1, keepdims=True) + EPS)
    return (y * g.astype(jnp.float32)).astype(x.dtype)


def causal_dwconv(u, buf, w, b):
    k = w.shape[0]
    t = u.shape[1]
    ext = jnp.concatenate([buf.astype(u.dtype), u], axis=1)
    y = b
    for j in range(k):
        y = y + ext[:, j:j + t] * w[j]
    return y, ext[:, -(k - 1):]


def pool_mixer(u, buf, pos0, w_grp, scale):
    B, T, _ = u.shape
    ext = jnp.concatenate([buf.astype(u.dtype), u], axis=1)
    cs = jnp.cumsum(ext.astype(jnp.float32), axis=1)
    cs = jnp.pad(cs, ((0, 0), (1, 0), (0, 0)))
    pos = pos0 + jnp.arange(T, dtype=jnp.int32)
    outs = []
    for g, w in enumerate(POOL_WINDOWS):
        sl = slice(g * POOL_GROUP, (g + 1) * POOL_GROUP)
        win = cs[:, POOL_BUF + 1:POOL_BUF + 1 + T, sl] - cs[:, POOL_BUF + 1 - w:POOL_BUF + 1 - w + T, sl]
        cnt = jnp.minimum(pos + 1, w).astype(jnp.float32)[None, :, None]
        outs.append(win / cnt)
    d = (jnp.concatenate(outs, axis=-1) - u.astype(jnp.float32)).reshape(B, T, N_POOL_GROUPS, POOL_GROUP)
    y = jnp.einsum('btgc,gcd->btgd', d, w_grp.astype(jnp.float32)).reshape(B, T, D_POOL)
    y = y * scale.astype(jnp.float32)
    return y.astype(u.dtype), ext[:, -POOL_BUF:]


def _lin_combine(c1, c2):
    a1, b1 = c1
    a2, b2 = c2
    return a1 * a2, a2 * b1 + b2


def rglru(u, conv_buf, h0, conv_w, conv_b, w_a, b_a, w_x, b_x, lam):
    xc, new_conv = causal_dwconv(u, conv_buf, conv_w, conv_b)
    B, T, _ = xc.shape
    xf = xc.astype(jnp.float32)
    xb = xf.reshape(B, T, N_RNN_BLOCKS, RNN_BLOCK)
    r = jax.nn.sigmoid(jnp.einsum('btnc,ncd->btnd', xb, w_a.astype(jnp.float32)).reshape(B, T, D_RNN) + b_a.astype(jnp.float32))
    i = jax.nn.sigmoid(jnp.einsum('btnc,ncd->btnd', xb, w_x.astype(jnp.float32)).reshape(B, T, D_RNN) + b_x.astype(jnp.float32))
    log_a = -RG_C * r * jax.nn.softplus(-lam.astype(jnp.float32))
    a = jnp.exp(log_a)
    beta = jnp.sqrt(-jnp.expm1(2.0 * log_a))
    bterm = beta * (i * xf)
    bterm = bterm.at[:, 0].add(a[:, 0] * h0.astype(jnp.float32))
    _, h = lax.associative_scan(_lin_combine, (a, bterm), axis=1)
    return h.astype(u.dtype), new_conv, h[:, -1].astype(h0.dtype)


def mem_kv(mem, g_mem, w_k, w_v):
    B = mem.shape[0]
    mn = rmsnorm(mem, g_mem)
    k = (mn @ w_k).reshape(B, N_MEM, N_HEADS, HEAD_DIM)
    v = (mn @ w_v).reshape(B, N_MEM, N_HEADS, HEAD_DIM)
    return k, v


def cross_attn(q, k, v):
    s = jnp.einsum('bthd,bmhd->bhtm', q.astype(jnp.float32), k.astype(jnp.float32)) * (HEAD_DIM ** -0.5)
    p = jax.nn.softmax(s, axis=-1)
    o = jnp.einsum('bhtm,bmhd->bthd', p, v.astype(jnp.float32))
    return o.astype(q.dtype)


def layer(h, m_k, m_v, pool_buf, rnn_buf, rnn_h0, ffn_buf, pos0, lw):
    B, T, _ = h.shape
    xn = rmsnorm(h, lw['g_mix'])
    z = xn @ lw['w_in']
    u_pool = z[..., :D_POOL]
    u_rnn = z[..., D_POOL:D_POOL + D_RNN]
    q = z[..., D_POOL + D_RNN:].reshape(B, T, N_HEADS, HEAD_DIM)
    gates = jax.nn.sigmoid((xn @ lw['w_gate'] + lw['b_gate']).astype(jnp.float32)).reshape(B, T, N_BRANCH, D_MODEL)
    y_pool, new_pool = pool_mixer(u_pool, pool_buf, pos0, lw['pool_w'], lw['pool_scale'])
    y_rnn, new_rnn_buf, new_h = rglru(u_rnn, rnn_buf, rnn_h0, lw['rnn_conv_w'], lw['rnn_conv_b'],
                                      lw['rnn_wa'], lw['rnn_ba'], lw['rnn_wx'], lw['rnn_bx'], lw['rnn_lambda'])
    y_attn = cross_attn(q, m_k, m_v).reshape(B, T, D_Q)
    merged = (gates[:, :, 0] * (y_pool @ lw['w_br_pool']).astype(jnp.float32)
              + gates[:, :, 1] * (y_rnn @ lw['w_br_rnn']).astype(jnp.float32)
              + gates[:, :, 2] * (y_attn @ lw['w_br_attn']).astype(jnp.float32)).astype(h.dtype)
    h = h + merged @ lw['w_out']
    xn2 = rmsnorm(h, lw['g_ffn'])
    gv = xn2 @ lw['w_up']
    g_pre, val = gv[..., :D_FF], gv[..., D_FF:]
    g_conv, new_ffn = causal_dwconv(g_pre, ffn_buf, lw['ffn_conv_w'], lw['ffn_conv_b'])
    h = h + (jax.nn.gelu(g_conv) * val) @ lw['w_down']
    return h, new_pool, new_rnn_buf, new_h, new_ffn


def setup_inputs(seed: int = 0) -> dict:
    key = jax.random.key(seed)
    ks = iter(jax.random.split(key, 40))

    def nrm(shape, scale):
        return jax.random.normal(next(ks), shape, jnp.float32) * scale

    u = jax.random.uniform(next(ks), (DEPTH, D_RNN), jnp.float32, minval=0.9, maxval=0.999)
    a0 = u ** (1.0 / RG_C)
    rnn_lambda = jnp.log(a0) - jnp.log1p(-a0)
    return {
        'x_prompt': nrm((BATCH, SEQ, D_MODEL), 1.0),
        'x_sample': nrm((DEC_BATCH, DEC_SEQ, D_MODEL), 1.0),
        'mem_prompt': nrm((BATCH, N_MEM, D_MODEL), 1.0),
        'cache_mem_k': nrm((DEPTH, DEC_BATCH, N_MEM, N_HEADS, HEAD_DIM), 1.0),
        'cache_mem_v': nrm((DEPTH, DEC_BATCH, N_MEM, N_HEADS, HEAD_DIM), 1.0),
        'state_pool': nrm((DEPTH, DEC_BATCH, POOL_BUF, D_POOL), 1.0),
        'state_rnn_conv': nrm((DEPTH, DEC_BATCH, RNN_CONV - 1, D_RNN), 1.0),
        'state_rnn_h': nrm((DEPTH, DEC_BATCH, D_RNN), 0.5),
        'state_ffn_conv': nrm((DEPTH, DEC_BATCH, FFN_CONV - 1, D_FF), 1.0),
        'g_mix': 1.0 + nrm((DEPTH, D_MODEL), 0.05),
        'w_in': nrm((DEPTH, D_MODEL, D_MIX), D_MODEL ** -0.5),
        'w_gate': nrm((DEPTH, D_MODEL, N_BRANCH * D_MODEL), D_MODEL ** -0.5),
        'b_gate': nrm((DEPTH, N_BRANCH * D_MODEL), 0.01),
        'pool_w': nrm((DEPTH, N_POOL_GROUPS, POOL_GROUP, POOL_GROUP), POOL_GROUP ** -0.5),
        'pool_scale': 1.0 + nrm((DEPTH, D_POOL), 0.1),
        'rnn_conv_w': nrm((DEPTH, RNN_CONV, D_RNN), RNN_CONV ** -0.5),
        'rnn_conv_b': nrm((DEPTH, D_RNN), 0.01),
        'rnn_wa': nrm((DEPTH, N_RNN_BLOCKS, RNN_BLOCK, RNN_BLOCK), RNN_BLOCK ** -0.5),
        'rnn_ba': nrm((DEPTH, D_RNN), 0.01),
        'rnn_wx': nrm((DEPTH, N_RNN_BLOCKS, RNN_BLOCK, RNN_BLOCK), RNN_BLOCK ** -0.5),
        'rnn_bx': nrm((DEPTH, D_RNN), 0.01),
        'rnn_lambda': rnn_lambda,
        'g_mem': 1.0 + nrm((DEPTH, D_MODEL), 0.05),
        'w_k': nrm((DEPTH, D_MODEL, D_Q), D_MODEL ** -0.5),
        'w_v': nrm((DEPTH, D_MODEL, D_Q), D_MODEL ** -0.5),
        'w_br_pool': nrm((DEPTH, D_POOL, D_MODEL), D_POOL ** -0.5),
        'w_br_rnn': nrm((DEPTH, D_RNN, D_MODEL), D_RNN ** -0.5),
        'w_br_attn': nrm((DEPTH, D_Q, D_MODEL), D_Q ** -0.5),
        'w_out': nrm((DEPTH, D_MODEL, D_MODEL), D_MODEL ** -0.5),
        'g_ffn': 1.0 + nrm((DEPTH, D_MODEL), 0.05),
        'w_up': nrm((DEPTH, D_MODEL, 2 * D_FF), D_MODEL ** -0.5),
        'ffn_conv_w': nrm((DEPTH, FFN_CONV, D_FF), FFN_CONV ** -0.5),
        'ffn_conv_b': nrm((DEPTH, D_FF), 0.01),
        'w_down': nrm((DEPTH, D_FF, D_MODEL), D_FF ** -0.5),
        'g_final': 1.0 + nrm((D_MODEL,), 0.05),
    }


def reference(x_prompt, x_sample, mem_prompt, cache_mem_k, cache_mem_v, state_pool, state_rnn_conv,
              state_rnn_h, state_ffn_conv, g_mix, w_in, w_gate, b_gate, pool_w, pool_scale,
              rnn_conv_w, rnn_conv_b, rnn_wa, rnn_ba, rnn_wx, rnn_bx, rnn_lambda, g_mem, w_k, w_v,
              w_br_pool, w_br_rnn, w_br_attn, w_out, g_ffn, w_up, ffn_conv_w, ffn_conv_b, w_down, g_final):
    Bp = x_prompt.shape[0]
    hp = x_prompt
    hs = x_sample
    p_pool, p_rconv, p_rh, p_fconv, p_mk, p_mv = [], [], [], [], [], []
    s_pool, s_rconv, s_rh, s_fconv = [], [], [], []
    for l in range(DEPTH):
        lw = {
            'g_mix': g_mix[l], 'w_in': w_in[l], 'w_gate': w_gate[l], 'b_gate': b_gate[l],
            'pool_w': pool_w[l], 'pool_scale': pool_scale[l],
            'rnn_conv_w': rnn_conv_w[l], 'rnn_conv_b': rnn_conv_b[l], 'rnn_wa': rnn_wa[l], 'rnn_ba': rnn_ba[l],
            'rnn_wx': rnn_wx[l], 'rnn_bx': rnn_bx[l], 'rnn_lambda': rnn_lambda[l],
            'w_br_pool': w_br_pool[l], 'w_br_rnn': w_br_rnn[l], 'w_br_attn': w_br_attn[l], 'w_out': w_out[l],
            'g_ffn': g_ffn[l], 'w_up': w_up[l], 'ffn_conv_w': ffn_conv_w[l], 'ffn_conv_b': ffn_conv_b[l],
            'w_down': w_down[l],
        }
        mk, mv = mem_kv(mem_prompt, g_mem[l], w_k[l], w_v[l])
        hp, npool, nrc, nrh, nfc = layer(
            hp, mk, mv,
            jnp.zeros((Bp, POOL_BUF, D_POOL), x_prompt.dtype),
            jnp.zeros((Bp, RNN_CONV - 1, D_RNN), x_prompt.dtype),
            jnp.zeros((Bp, D_RNN), state_rnn_h.dtype),
            jnp.zeros((Bp, FFN_CONV - 1, D_FF), x_prompt.dtype),
            0, lw)
        p_pool.append(npool); p_rconv.append(nrc); p_rh.append(nrh); p_fconv.append(nfc)
        p_mk.append(mk); p_mv.append(mv)
        hs, spool, src, srh, sfc = layer(
            hs, cache_mem_k[l], cache_mem_v[l], state_pool[l], state_rnn_conv[l], state_rnn_h[l],
            state_ffn_conv[l], PAST_LEN, lw)
        s_pool.append(spool); s_rconv.append(src); s_rh.append(srh); s_fconv.append(sfc)
    y_prompt = rmsnorm(hp, g_final)
    y_sample = rmsnorm(hs, g_final)
    return (y_prompt, y_sample,
            jnp.stack(p_pool), jnp.stack(p_rconv), jnp.stack(p_rh), jnp.stack(p_fconv),
            jnp.stack(p_mk), jnp.stack(p_mv),
            jnp.stack(s_pool), jnp.stack(s_rconv), jnp.stack(s_rh), jnp.stack(s_fconv))
```

```python
import functools

import jax
import jax.numpy as jnp
from jax import lax
from jax.experimental import pallas as pl
from jax.experimental.pallas import tpu as pltpu

D_MODEL = 1024
DEPTH = 2
PAST_LEN = 16384
POOL_WINDOWS = (2, 4, 8, 16)
D_POOL = D_MODEL // 2
POOL_GROUP = D_POOL // len(POOL_WINDOWS)
POOL_BUF = max(POOL_WINDOWS) - 1
D_RNN = D_MODEL
N_RNN_BLOCKS = 8
RNN_BLOCK = D_RNN // N_RNN_BLOCKS
RNN_CONV = 4
RG_C = 8.0
N_HEADS = 4
HEAD_DIM = 128
D_Q = N_HEADS * HEAD_DIM
D_FF = 3 * D_MODEL
FFN_CONV = 3
EPS = 1e-6

_V7X_VMEM_BYTES = 64 * 1024 * 1024
_VMEM_LIMIT_BYTES = _V7X_VMEM_BYTES - 8 * 1024 * 1024

_PROMPT_STEPS = 32
_SAMPLE_SEQS = 32
_ATTN_SEQS = 8
_FF_CHUNK = 512

_F32 = jnp.float32
_BF16 = jnp.bfloat16


def _dot(a, b):
    return jnp.dot(a, b, preferred_element_type=_F32)


def _dot_nt(a, b):
    return lax.dot_general(a, b, (((1,), (1,)), ((), ())), preferred_element_type=_F32)


def _rmsnorm(x, g):
    return x * lax.rsqrt(jnp.mean(x * x, axis=-1, keepdims=True) + EPS) * g


def _softplus(x):
    return jnp.maximum(x, 0.0) + jnp.log1p(jnp.exp(-jnp.abs(x)))


def _softmax_rows(s):
    e = jnp.exp(s - jnp.max(s, axis=-1, keepdims=True))
    return e / jnp.sum(e, axis=-1, keepdims=True)


def _attend_head(q, k, v):
    p = _softmax_rows(_dot_nt(q, k) * (HEAD_DIM ** -0.5))
    return _dot(p.astype(_BF16), v)


def _head(h):
    return slice(h * HEAD_DIM, (h + 1) * HEAD_DIM)


def _pool_window_sums(ext, rows, nb):
    sums = []
    for g, w in enumerate(POOL_WINDOWS):
        s = ext[:, g * POOL_GROUP:(g + 1) * POOL_GROUP]
        off, k = 0, 1
        while k < w:
            s = s[k * nb:] + s[:-k * nb]
            off += k * nb
            k *= 2
        start = (POOL_BUF + 1) * nb - off
        sums.append(s[start:start + rows])
    return sums


def _mixer_kernel(*refs, nb, nt, carried, pos0):
    rows = nb * nt
    (x_ref, g_mix, w_in, w_gate, b_gate, pool_w, pool_scale, conv_w, conv_b, w_ax,
     b_a, b_x, lam, w_brp, w_brr, w_bra, w_out) = refs[:17]
    if carried:
        k_ref, v_ref, out_ref, pool_st, conv_st, h_st, q_scr, o_scr = refs[17:]

        @pl.when(pl.program_id(0) == 0)
        def _():
            pool_st[...] = jnp.zeros_like(pool_st)
            conv_st[...] = jnp.zeros_like(conv_st)
            h_st[...] = jnp.zeros_like(h_st)

        x = x_ref[...]
        pool_hist = pool_st[...]
        conv_hist = conv_st[...]
        h0 = h_st[...]
    else:
        (yattn_ref, pool_in, conv_in, h_in, out_ref, pool_out, conv_out, h_out) = refs[17:]
        x = x_ref[...].reshape(rows, D_MODEL)
        pool_hist = jnp.concatenate(
            [jnp.zeros((nb, D_POOL), _F32), pool_in[...].reshape(POOL_BUF * nb, D_POOL)], axis=0)
        conv_hist = conv_in[...].reshape((RNN_CONV - 1) * nb, D_RNN)
        h0 = h_in[...]

    xn = _rmsnorm(x, g_mix[...]).astype(_BF16)

    u_pool = _dot(xn, w_in[:, 0:D_POOL])
    ext = jnp.concatenate([pool_hist, u_pool], axis=0)
    wins = _pool_window_sums(ext, rows, nb)
    if pos0 < POOL_BUF:
        row = lax.broadcasted_iota(jnp.int32, (rows, POOL_GROUP), 0)
        pos1 = pos0 + pl.program_id(0) * nt + (row >> (nb.bit_length() - 1)) + 1
    ys = []
    for g, w in enumerate(POOL_WINDOWS):
        if pos0 < POOL_BUF:
            mean = wins[g] / jnp.minimum(pos1, w).astype(_F32)
        else:
            mean = wins[g] * (1.0 / w)
        d = mean - u_pool[:, g * POOL_GROUP:(g + 1) * POOL_GROUP]
        ys.append(_dot(d.astype(_BF16), pool_w[g]))
    y_pool = jnp.concatenate(ys, axis=1) * pool_scale[...]
    new_pool_hist = ext[rows:]
    gate = jax.nn.sigmoid(_dot(xn, w_gate[:, 0:D_MODEL]) + b_gate[:, 0:D_MODEL])
    merged = gate * _dot(y_pool.astype(_BF16), w_brp[...])

    u_rnn = _dot(xn, w_in[:, D_POOL:D_POOL + D_RNN])
    ext_r = jnp.concatenate([conv_hist, u_rnn], axis=0)
    xc = conv_b[...]
    for j in range(RNN_CONV):
        xc = xc + ext_r[j * nb:j * nb + rows] * conv_w[j:j + 1, :]
    new_conv_hist = ext_r[rows:]
    xcb = xc.astype(_BF16)
    log_a_unit = -RG_C * _softplus(-lam[...])
    cols, h_last = [], []
    for n in range(N_RNN_BLOCKS):
        sl = slice(n * RNN_BLOCK, (n + 1) * RNN_BLOCK)
        ri = _dot(xcb[:, sl], w_ax[n])
        r = jax.nn.sigmoid(ri[:, :RNN_BLOCK] + b_a[:, sl])
        i = jax.nn.sigmoid(ri[:, RNN_BLOCK:] + b_x[:, sl])
        log_a = r * log_a_unit[:, sl]
        a = jnp.exp(log_a)
        th = jnp.tanh(log_a)
        b = jnp.sqrt(-2.0 * th / (1.0 - th)) * (i * xc[:, sl])
        h = h0[:, sl]
        hs = []
        for t in range(nt):
            h = a[t * nb:(t + 1) * nb] * h + b[t * nb:(t + 1) * nb]
            hs.append(h)
        cols.append(jnp.concatenate(hs, axis=0))
        h_last.append(h)
    y_rnn = jnp.concatenate(cols, axis=1)
    new_h = jnp.concatenate(h_last, axis=1)
    gate = jax.nn.sigmoid(_dot(xn, w_gate[:, D_MODEL:2 * D_MODEL]) + b_gate[:, D_MODEL:2 * D_MODEL])
    merged = merged + gate * _dot(y_rnn.astype(_BF16), w_brr[...])

    if carried:
        q = _dot(xn, w_in[:, D_POOL + D_RNN:])
        for h in range(N_HEADS):
            q_scr[h] = q[:, _head(h)]
        for b in range(nb):
            seq_rows = pl.ds(b, nt, stride=nb)
            for h in range(N_HEADS):
                qb = q_scr[h, seq_rows, :].astype(_BF16)
                o_scr[h, seq_rows, :] = _attend_head(qb, k_ref[b, :, _head(h)], v_ref[b, :, _head(h)])
        y_attn = jnp.concatenate([o_scr[h] for h in range(N_HEADS)], axis=1)
    else:
        y_attn = yattn_ref[...].reshape(rows, D_Q)
    gate = jax.nn.sigmoid(_dot(xn, w_gate[:, 2 * D_MODEL:]) + b_gate[:, 2 * D_MODEL:])
    merged = merged + gate * _dot(y_attn.astype(_BF16), w_bra[...])

    out = x + _dot(merged.astype(_BF16), w_out[...])
    if carried:
        out_ref[...] = out
        pool_st[...] = new_pool_hist
        conv_st[...] = new_conv_hist
        h_st[...] = new_h
    else:
        out_ref[...] = out.reshape(nt, nb, D_MODEL)
        pool_out[...] = new_pool_hist[nb:].reshape(POOL_BUF, nb, D_POOL)
        conv_out[...] = new_conv_hist.reshape(RNN_CONV - 1, nb, D_RNN)
        h_out[...] = new_h


def _ffn_kernel(*refs, nb, nt, carried, final_norm):
    rows = nb * nt
    x_ref, g_ffn, w_up, conv_w, conv_b, w_down = refs[:6]
    rest = list(refs[6:])
    g_final = rest.pop(0) if final_norm else None
    st_in = None if carried else rest.pop(0)
    out_ref = rest.pop(0)
    y_ref = rest.pop(0) if final_norm else None
    st_out, act_scr = rest

    if carried:
        @pl.when(pl.program_id(0) == 0)
        def _():
            st_out[...] = jnp.zeros_like(st_out)

        x = x_ref[...]
    else:
        x = x_ref[...].reshape(rows, D_MODEL)

    xn = _rmsnorm(x, g_ffn[...]).astype(_BF16)
    hist_rows = (FFN_CONV - 1) * nb
    for c in range(0, D_FF, _FF_CHUNK):
        sl = slice(c, c + _FF_CHUNK)
        g_pre = _dot(xn, w_up[:, sl])
        val = _dot(xn, w_up[:, D_FF + c:D_FF + c + _FF_CHUNK])
        if carried:
            hist = st_out[:, sl]
        else:
            hist = st_in[:, :, sl].reshape(hist_rows, _FF_CHUNK)
        ext = jnp.concatenate([hist, g_pre], axis=0)
        g_conv = conv_b[:, sl]
        for j in range(FFN_CONV):
            g_conv = g_conv + ext[j * nb:j * nb + rows] * conv_w[j:j + 1, sl]
        if carried:
            st_out[:, sl] = ext[rows:]
        else:
            st_out[:, :, sl] = ext[rows:].reshape(FFN_CONV - 1, nb, _FF_CHUNK)
        act_scr[:, sl] = (jax.nn.gelu(g_conv) * val).astype(_BF16)
    out = x + _dot(act_scr[...], w_down[...])
    if carried:
        out_ref[...] = out
        if final_norm:
            y_ref[...] = _rmsnorm(out, g_final[...])
    else:
        out_ref[...] = out.reshape(nt, nb, D_MODEL)
        if final_norm:
            y_ref[...] = _rmsnorm(out, g_final[...]).reshape(nt, nb, D_MODEL)


def _mem_kv_kernel(mem_ref, g_ref, wk_ref, wv_ref, k_ref, v_ref, kb_ref, vb_ref):
    mn = _rmsnorm(mem_ref[0], g_ref[0]).astype(_BF16)
    k = _dot(mn, wk_ref[0])
    v = _dot(mn, wv_ref[0])
    k_ref[0, 0] = k
    v_ref[0, 0] = v
    kb_ref[0, 0] = k.astype(_BF16)
    vb_ref[0, 0] = v.astype(_BF16)


def _sample_attn_kernel(x_ref, g_ref, wq_ref, k_ref, v_ref, o_ref, *, n_seq, t_seq):
    xn = _rmsnorm(x_ref[...], g_ref[...]).astype(_BF16)
    q = _dot(xn, wq_ref[...]).astype(_BF16)
    for b in range(n_seq):
        rs = slice(b * t_seq, (b + 1) * t_seq)
        for h in range(N_HEADS):
            o_ref[rs, _head(h)] = _attend_head(q[rs, _head(h)], k_ref[b, :, _head(h)].astype(_BF16),
                                               v_ref[b, :, _head(h)].astype(_BF16))


def _resident(shape):
    return pl.BlockSpec(shape, lambda *_: (0,) * len(shape), pipeline_mode=pl.Buffered(1))


def _params(semantics):
    return pltpu.CompilerParams(dimension_semantics=semantics, vmem_limit_bytes=_VMEM_LIMIT_BYTES)


def _mixer_weights(lw):
    names = ('g_mix', 'w_in', 'w_gate', 'b_gate', 'pool_w', 'pool_scale', 'rnn_conv_w', 'rnn_conv_b',
             'rnn_wax', 'rnn_ba', 'rnn_bx', 'rnn_lambda', 'w_br_pool', 'w_br_rnn', 'w_br_attn', 'w_out')
    arrs = [lw[n] for n in names]
    return arrs, [_resident(a.shape) for a in arrs]


def _mixer_prompt(h, lw, kb, vb, n_seq):
    nt, nb = _PROMPT_STEPS, n_seq
    rows = nt * nb
    w_arrs, w_specs = _mixer_weights(lw)
    tile = pl.BlockSpec((rows, D_MODEL), lambda i: (i, 0))
    out_shapes = (jax.ShapeDtypeStruct(h.shape, _F32),
                  jax.ShapeDtypeStruct(((POOL_BUF + 1) * nb, D_POOL), _F32),
                  jax.ShapeDtypeStruct(((RNN_CONV - 1) * nb, D_RNN), _F32),
                  jax.ShapeDtypeStruct((nb, D_RNN), _F32))
    whole = lambda s: pl.BlockSpec(s.shape, lambda i: (0,) * len(s.shape))
    return pl.pallas_call(
        functools.partial(_mixer_kernel, nb=nb, nt=nt, carried=True, pos0=0),
        grid=(h.shape[0] // rows,),
        in_specs=[tile] + w_specs + [_resident(kb.shape), _resident(vb.shape)],
        out_specs=(tile,) + tuple(whole(s) for s in out_shapes[1:]),
        out_shape=out_shapes,
        scratch_shapes=[pltpu.VMEM((N_HEADS, rows, HEAD_DIM), _F32)] * 2,
        compiler_params=_params(("arbitrary",)),
        name="mixer_prompt",
    )(h, *w_arrs, kb, vb)


def _mixer_sample(h, lw, y_attn, pool_st, conv_st, h_st):
    nt, n_seq = h.shape[0], h.shape[1]
    nb = _SAMPLE_SEQS
    w_arrs, w_specs = _mixer_weights(lw)
    seq3 = lambda a: pl.BlockSpec((a.shape[0], nb, a.shape[2]), lambda i: (0, i, 0))
    h_spec = pl.BlockSpec((nb, D_RNN), lambda i: (i, 0))
    outs = (h, pool_st, conv_st, h_st)
    return pl.pallas_call(
        functools.partial(_mixer_kernel, nb=nb, nt=nt, carried=False, pos0=PAST_LEN),
        grid=(n_seq // nb,),
        in_specs=[seq3(h)] + w_specs + [seq3(y_attn), seq3(pool_st), seq3(conv_st), h_spec],
        out_specs=(seq3(h), seq3(pool_st), seq3(conv_st), h_spec),
        out_shape=tuple(jax.ShapeDtypeStruct(a.shape, _F32) for a in outs),
        compiler_params=_params(("arbitrary",)),
        name="mixer_sample",
    )(h, *w_arrs, y_attn, pool_st, conv_st, h_st)


def _ffn_weights(lw, g_final):
    arrs = [lw['g_ffn'], lw['w_up'], lw['ffn_conv_w'], lw['ffn_conv_b'], lw['w_down']]
    if g_final is not None:
        arrs.append(g_final)
    return arrs, [_resident(a.shape) for a in arrs]


def _ffn_prompt(h, lw, n_seq, g_final):
    nt, nb = _PROMPT_STEPS, n_seq
    rows = nt * nb
    w_arrs, w_specs = _ffn_weights(lw, g_final)
    tile = pl.BlockSpec((rows, D_MODEL), lambda i: (i, 0))
    st_shape = ((FFN_CONV - 1) * nb, D_FF)
    n_main = 2 if g_final is not None else 1
    return pl.pallas_call(
        functools.partial(_ffn_kernel, nb=nb, nt=nt, carried=True, final_norm=g_final is not None),
        grid=(h.shape[0] // rows,),
        in_specs=[tile] + w_specs,
        out_specs=(tile,) * n_main + (pl.BlockSpec(st_shape, lambda i: (0, 0)),),
        out_shape=(jax.ShapeDtypeStruct(h.shape, _F32),) * n_main + (jax.ShapeDtypeStruct(st_shape, _F32),),
        scratch_shapes=[pltpu.VMEM((rows, D_FF), _BF16)],
        compiler_params=_params(("arbitrary",)),
        name="ffn_prompt",
    )(h, *w_arrs)


def _ffn_sample(h, lw, st, g_final):
    nt, n_seq = h.shape[0], h.shape[1]
    nb = _SAMPLE_SEQS
    w_arrs, w_specs = _ffn_weights(lw, g_final)
    seq3 = lambda a: pl.BlockSpec((a.shape[0], nb, a.shape[2]), lambda i: (0, i, 0))
    n_main = 2 if g_final is not None else 1
    return pl.pallas_call(
        functools.partial(_ffn_kernel, nb=nb, nt=nt, carried=False, final_norm=g_final is not None),
        grid=(n_seq // nb,),
        in_specs=[seq3(h)] + w_specs + [seq3(st)],
        out_specs=(seq3(h),) * n_main + (seq3(st),),
        out_shape=(jax.ShapeDtypeStruct(h.shape, _F32),) * n_main + (jax.ShapeDtypeStruct(st.shape, _F32),),
        scratch_shapes=[pltpu.VMEM((nt * nb, D_FF), _BF16)],
        compiler_params=_params(("arbitrary",)),
        name="ffn_sample",
    )(h, *w_arrs, st)


def _mem_kv(mem, g_mem, w_k, w_v):
    n_seq, n_mem = mem.shape[0], mem.shape[1]
    per_layer = lambda a: pl.BlockSpec((1,) + a.shape[1:], lambda l, b: (l, 0, 0))
    kv_spec = pl.BlockSpec((1, 1, n_mem, D_Q), lambda l, b: (l, b, 0, 0))
    kv_shape = (DEPTH, n_seq, n_mem, D_Q)
    return pl.pallas_call(
        _mem_kv_kernel,
        grid=(DEPTH, n_seq),
        in_specs=[pl.BlockSpec((1, n_mem, D_MODEL), lambda l, b: (b, 0, 0)),
                  per_layer(g_mem), per_layer(w_k), per_layer(w_v)],
        out_specs=(kv_spec,) * 4,
        out_shape=(jax.ShapeDtypeStruct(kv_shape, _F32),) * 2 + (jax.ShapeDtypeStruct(kv_shape, _BF16),) * 2,
        compiler_params=_params(("arbitrary", "arbitrary")),
        name="mem_kv",
    )(mem, g_mem, w_k, w_v)


def _sample_attn(h_nat, g_mix, w_q, k, v, t_seq):
    n_seq, n_mem = k.shape[0], k.shape[1]
    nb = _ATTN_SEQS
    rows = nb * t_seq
    kv_spec = pl.BlockSpec((nb, n_mem, D_Q), lambda i: (i, 0, 0))
    return pl.pallas_call(
        functools.partial(_sample_attn_kernel, n_seq=nb, t_seq=t_seq),
        grid=(n_seq // nb,),
        in_specs=[pl.BlockSpec((rows, D_MODEL), lambda i: (i, 0)), _resident(g_mix.shape),
                  _resident(w_q.shape), kv_spec, kv_spec],
        out_specs=pl.BlockSpec((rows, D_Q), lambda i: (i, 0)),
        out_shape=jax.ShapeDtypeStruct((n_seq * t_seq, D_Q), _F32),
        compiler_params=_params(("arbitrary",)),
        name="sample_attn",
    )(h_nat, g_mix, w_q, k, v)


def kernel(x_prompt, x_sample, mem_prompt, cache_mem_k, cache_mem_v, state_pool, state_rnn_conv, state_rnn_h, state_ffn_conv, g_mix, w_in, w_gate, b_gate, pool_w, pool_scale, rnn_conv_w, rnn_conv_b, rnn_wa, rnn_ba, rnn_wx, rnn_bx, rnn_lambda, g_mem, w_k, w_v, w_br_pool, w_br_rnn, w_br_attn, w_out, g_ffn, w_up, ffn_conv_w, ffn_conv_b, w_down, g_final):
    bp, tp, _ = x_prompt.shape
    bs, ts, _ = x_sample.shape
    n_mem = mem_prompt.shape[1]
    bf = lambda a: a.astype(_BF16)
    row = lambda a: a[:, None, :]
    layers = []
    for l in range(DEPTH):
        layers.append({
            'g_mix': row(g_mix)[l], 'w_in': bf(w_in[l]), 'w_gate': bf(w_gate[l]), 'b_gate': row(b_gate)[l],
            'pool_w': bf(pool_w[l]), 'pool_scale': row(pool_scale)[l],
            'rnn_conv_w': rnn_conv_w[l], 'rnn_conv_b': row(rnn_conv_b)[l],
            'rnn_wax': bf(jnp.concatenate([rnn_wa[l], rnn_wx[l]], axis=-1)),
            'rnn_ba': row(rnn_ba)[l], 'rnn_bx': row(rnn_bx)[l], 'rnn_lambda': row(rnn_lambda)[l],
            'w_br_pool': bf(w_br_pool[l]), 'w_br_rnn': bf(w_br_rnn[l]), 'w_br_attn': bf(w_br_attn[l]),
            'w_out': bf(w_out[l]),
            'g_ffn': row(g_ffn)[l], 'w_up': bf(w_up[l]), 'ffn_conv_w': ffn_conv_w[l],
            'ffn_conv_b': row(ffn_conv_b)[l], 'w_down': bf(w_down[l]),
        })
    g_fin = g_final[None, :]

    p_mk, p_mv, kb, vb = _mem_kv(mem_prompt, row(g_mem), bf(w_k), bf(w_v))

    time_major = lambda a: jnp.swapaxes(a, 0, 1)
    hp = time_major(x_prompt).reshape(tp * bp, D_MODEL)
    hs = time_major(x_sample)
    p_pool, p_rconv, p_rh, p_fconv = [], [], [], []
    s_pool, s_rconv, s_rh, s_fconv = [], [], [], []
    yp = ys = None
    for l, lw in enumerate(layers):
        last = l == DEPTH - 1
        hp, pool_st, conv_st, h_st = _mixer_prompt(hp, lw, kb[l], vb[l], bp)
        res = _ffn_prompt(hp, lw, bp, g_fin if last else None)
        hp, ffn_st = res[0], res[-1]
        if last:
            yp = res[1]
        p_pool.append(time_major(pool_st.reshape(POOL_BUF + 1, bp, D_POOL)[1:]))
        p_rconv.append(time_major(conv_st.reshape(RNN_CONV - 1, bp, D_RNN)))
        p_rh.append(h_st)
        p_fconv.append(time_major(ffn_st.reshape(FFN_CONV - 1, bp, D_FF)))

        hs_nat = time_major(hs).reshape(bs * ts, D_MODEL)
        o_nat = _sample_attn(hs_nat, lw['g_mix'], lw['w_in'][:, D_POOL + D_RNN:],
                             cache_mem_k[l].reshape(bs, n_mem, D_Q), cache_mem_v[l].reshape(bs, n_mem, D_Q), ts)
        y_attn = time_major(o_nat.reshape(bs, ts, D_Q))
        hs, pool_o, conv_o, h_o = _mixer_sample(
            hs, lw, y_attn, time_major(state_pool[l]), time_major(state_rnn_conv[l]), state_rnn_h[l])
        res = _ffn_sample(hs, lw, time_major(state_ffn_conv[l]), g_fin if last else None)
        hs, ffn_o = res[0], res[-1]
        if last:
            ys = res[1]
        s_pool.append(time_major(pool_o))
        s_rconv.append(time_major(conv_o))
        s_rh.append(h_o)
        s_fconv.append(time_major(ffn_o))

    y_prompt = time_major(yp.reshape(tp, bp, D_MODEL))
    y_sample = time_major(ys)
    head = (N_HEADS, HEAD_DIM)
    return (y_prompt, y_sample,
            jnp.stack(p_pool), jnp.stack(p_rconv), jnp.stack(p_rh), jnp.stack(p_fconv),
            p_mk.reshape(DEPTH, bp, n_mem, *head), p_mv.reshape(DEPTH, bp, n_mem, *head),
            jnp.stack(s_pool), jnp.stack(s_rconv), jnp.stack(s_rh), jnp.stack(s_fconv))
```

```python
import functools

import jax
import jax.numpy as jnp
from jax import lax
from jax.experimental import pallas as pl
from jax.experimental.pallas import tpu as pltpu

D_MODEL = 1024
DEPTH = 2
PAST_LEN = 16384
POOL_WINDOWS = (2, 4, 8, 16)
D_POOL = D_MODEL // 2
POOL_GROUP = D_POOL // len(POOL_WINDOWS)
POOL_BUF = max(POOL_WINDOWS) - 1
D_RNN = D_MODEL
N_RNN_BLOCKS = 8
RNN_BLOCK = D_RNN // N_RNN_BLOCKS
RNN_CONV = 4
RG_C = 8.0
N_HEADS = 4
HEAD_DIM = 128
D_Q = N_HEADS * HEAD_DIM
D_FF = 3 * D_MODEL
FFN_CONV = 3
EPS = 1e-6

_V7X_VMEM_BYTES = 64 * 1024 * 1024
_VMEM_LIMIT_BYTES = _V7X_VMEM_BYTES - 8 * 1024 * 1024

_PROMPT_STEPS = 32
_SAMPLE_SEQS = 32
_ATTN_SEQS = 8
_FF_CHUNK = 512

_F32 = jnp.float32
_BF16 = jnp.bfloat16


def _dot(a, b):
    return jnp.dot(a, b, preferred_element_type=_F32)


def _dot_nt(a, b):
    return lax.dot_general(a, b, (((1,), (1,)), ((), ())), preferred_element_type=_F32)


def _rmsnorm(x, g):
    return x * lax.rsqrt(jnp.mean(x * x, axis=-1, keepdims=True) + EPS) * g


def _softplus(x):
    return jnp.maximum(x, 0.0) + jnp.log1p(jnp.exp(-jnp.abs(x)))


def _softmax_rows(s):
    e = jnp.exp(s - jnp.max(s, axis=-1, keepdims=True))
    return e / jnp.sum(e, axis=-1, keepdims=True)


def _head(h):
    return slice(h * HEAD_DIM, (h + 1) * HEAD_DIM)


def _cross_attention(q, k_ref, v_ref, n_seq, t_seq):
    pairs = [(b, h) for b in range(n_seq) for h in range(N_HEADS)]
    rows = lambda b: slice(b * t_seq, (b + 1) * t_seq)
    s = jnp.concatenate(
        [_dot_nt(q[rows(b), _head(h)].astype(_BF16), k_ref[b, :, _head(h)].astype(_BF16)) for b, h in pairs],
        axis=0)
    p = _softmax_rows(s * (HEAD_DIM ** -0.5))
    o = [_dot(p[rows(n)].astype(_BF16), v_ref[b, :, _head(h)].astype(_BF16)) for n, (b, h) in enumerate(pairs)]
    return jnp.concatenate(
        [jnp.concatenate(o[b * N_HEADS:(b + 1) * N_HEADS], axis=1) for b in range(n_seq)], axis=0)


def _row_permutation(rows, inner):
    i = lax.broadcasted_iota(jnp.int32, (rows, rows), 0)
    j = lax.broadcasted_iota(jnp.int32, (rows, rows), 1)
    shift = inner.bit_length() - 1
    src = (i & (inner - 1)) * (rows // inner) + (i >> shift)
    return jnp.where(j == src, 1.0, 0.0).astype(_BF16)


def _pool_window_sums(ext, rows, nb):
    sums = []
    for g, w in enumerate(POOL_WINDOWS):
        s = ext[:, g * POOL_GROUP:(g + 1) * POOL_GROUP]
        off, k = 0, 1
        while k < w:
            s = s[k * nb:] + s[:-k * nb]
            off += k * nb
            k *= 2
        start = (POOL_BUF + 1) * nb - off
        sums.append(s[start:start + rows])
    return sums


def _mixer_kernel(*refs, nb, nt, carried, pos0):
    rows = nb * nt
    (x_ref, g_mix, w_in, w_gate, b_gate, pool_w, pool_scale, conv_w, conv_b, w_ax,
     b_a, b_x, lam, w_brp, w_brr, w_bra, w_out) = refs[:17]
    if carried:
        k_ref, v_ref, out_ref, pool_st, conv_st, h_st, perm_scr = refs[17:]

        @pl.when(pl.program_id(0) == 0)
        def _():
            pool_st[...] = jnp.zeros_like(pool_st)
            conv_st[...] = jnp.zeros_like(conv_st)
            h_st[...] = jnp.zeros_like(h_st)
            perm_scr[0] = _row_permutation(rows, nt)
            perm_scr[1] = _row_permutation(rows, nb)

        x = x_ref[...]
        pool_hist = pool_st[...]
        conv_hist = conv_st[...]
        h0 = h_st[...]
    else:
        (yattn_ref, pool_in, conv_in, h_in, out_ref, pool_out, conv_out, h_out) = refs[17:]
        x = x_ref[...].reshape(rows, D_MODEL)
        pool_hist = jnp.concatenate(
            [jnp.zeros((nb, D_POOL), _F32), pool_in[...].reshape(POOL_BUF * nb, D_POOL)], axis=0)
        conv_hist = conv_in[...].reshape((RNN_CONV - 1) * nb, D_RNN)
        h0 = h_in[...]

    xn = _rmsnorm(x, g_mix[...]).astype(_BF16)

    u_pool = _dot(xn, w_in[:, 0:D_POOL])
    ext = jnp.concatenate([pool_hist, u_pool], axis=0)
    wins = _pool_window_sums(ext, rows, nb)
    if pos0 < POOL_BUF:
        row = lax.broadcasted_iota(jnp.int32, (rows, POOL_GROUP), 0)
        pos1 = pos0 + pl.program_id(0) * nt + (row >> (nb.bit_length() - 1)) + 1
    ys = []
    for g, w in enumerate(POOL_WINDOWS):
        if pos0 < POOL_BUF:
            mean = wins[g] / jnp.minimum(pos1, w).astype(_F32)
        else:
            mean = wins[g] * (1.0 / w)
        d = mean - u_pool[:, g * POOL_GROUP:(g + 1) * POOL_GROUP]
        ys.append(_dot(d.astype(_BF16), pool_w[g]))
    y_pool = jnp.concatenate(ys, axis=1) * pool_scale[...]
    new_pool_hist = ext[rows:]
    gate = jax.nn.sigmoid(_dot(xn, w_gate[:, 0:D_MODEL]) + b_gate[:, 0:D_MODEL])
    merged = gate * _dot(y_pool.astype(_BF16), w_brp[...])

    u_rnn = _dot(xn, w_in[:, D_POOL:D_POOL + D_RNN])
    ext_r = jnp.concatenate([conv_hist, u_rnn], axis=0)
    xc = conv_b[...]
    for j in range(RNN_CONV):
        xc = xc + ext_r[j * nb:j * nb + rows] * conv_w[j:j + 1, :]
    new_conv_hist = ext_r[rows:]
    xcb = xc.astype(_BF16)
    log_a_unit = -RG_C * _softplus(-lam[...])
    cols, h_last = [], []
    for n in range(N_RNN_BLOCKS):
        sl = slice(n * RNN_BLOCK, (n + 1) * RNN_BLOCK)
        ri = _dot(xcb[:, sl], w_ax[n])
        r = jax.nn.sigmoid(ri[:, :RNN_BLOCK] + b_a[:, sl])
        i = jax.nn.sigmoid(ri[:, RNN_BLOCK:] + b_x[:, sl])
        log_a = r * log_a_unit[:, sl]
        a = jnp.exp(log_a)
        th = jnp.tanh(log_a)
        b = jnp.sqrt(-2.0 * th / (1.0 - th)) * (i * xc[:, sl])
        h = h0[:, sl]
        hs = []
        for t in range(nt):
            h = a[t * nb:(t + 1) * nb] * h + b[t * nb:(t + 1) * nb]
            hs.append(h)
        cols.append(jnp.concatenate(hs, axis=0))
        h_last.append(h)
    y_rnn = jnp.concatenate(cols, axis=1)
    new_h = jnp.concatenate(h_last, axis=1)
    gate = jax.nn.sigmoid(_dot(xn, w_gate[:, D_MODEL:2 * D_MODEL]) + b_gate[:, D_MODEL:2 * D_MODEL])
    merged = merged + gate * _dot(y_rnn.astype(_BF16), w_brr[...])

    if carried:
        q = _dot(xn, w_in[:, D_POOL + D_RNN:]).astype(_BF16)
        q_seq = _dot(perm_scr[0], q)
        o_seq = _cross_attention(q_seq, k_ref, v_ref, nb, nt).astype(_BF16)
        y_attn = _dot(perm_scr[1], o_seq)
    else:
        y_attn = yattn_ref[...].reshape(rows, D_Q)
    gate = jax.nn.sigmoid(_dot(xn, w_gate[:, 2 * D_MODEL:]) + b_gate[:, 2 * D_MODEL:])
    merged = merged + gate * _dot(y_attn.astype(_BF16), w_bra[...])

    out = x + _dot(merged.astype(_BF16), w_out[...])
    if carried:
        out_ref[...] = out
        pool_st[...] = new_pool_hist
        conv_st[...] = new_conv_hist
        h_st[...] = new_h
    else:
        out_ref[...] = out.reshape(nt, nb, D_MODEL)
        pool_out[...] = new_pool_hist[nb:].reshape(POOL_BUF, nb, D_POOL)
        conv_out[...] = new_conv_hist.reshape(RNN_CONV - 1, nb, D_RNN)
        h_out[...] = new_h


def _ffn_kernel(*refs, nb, nt, carried, final_norm):
    rows = nb * nt
    x_ref, g_ffn, w_up, conv_w, conv_b, w_down = refs[:6]
    rest = list(refs[6:])
    g_final = rest.pop(0) if final_norm else None
    st_in = None if carried else rest.pop(0)
    out_ref = rest.pop(0)
    y_ref = rest.pop(0) if final_norm else None
    st_out, act_scr = rest

    if carried:
        @pl.when(pl.program_id(0) == 0)
        def _():
            st_out[...] = jnp.zeros_like(st_out)

        x = x_ref[...]
    else:
        x = x_ref[...].reshape(rows, D_MODEL)

    xn = _rmsnorm(x, g_ffn[...]).astype(_BF16)
    hist_rows = (FFN_CONV - 1) * nb
    for c in range(0, D_FF, _FF_CHUNK):
        sl = slice(c, c + _FF_CHUNK)
        g_pre = _dot(xn, w_up[:, sl])
        val = _dot(xn, w_up[:, D_FF + c:D_FF + c + _FF_CHUNK])
        if carried:
            hist = st_out[:, sl]
        else:
            hist = st_in[:, :, sl].reshape(hist_rows, _FF_CHUNK)
        ext = jnp.concatenate([hist, g_pre], axis=0)
        g_conv = conv_b[:, sl]
        for j in range(FFN_CONV):
            g_conv = g_conv + ext[j * nb:j * nb + rows] * conv_w[j:j + 1, sl]
        if carried:
            st_out[:, sl] = ext[rows:]
        else:
            st_out[:, :, sl] = ext[rows:].reshape(FFN_CONV - 1, nb, _FF_CHUNK)
        act_scr[:, sl] = (jax.nn.gelu(g_conv) * val).astype(_BF16)
    out = x + _dot(act_scr[...], w_down[...])
    if carried:
        out_ref[...] = out
        if final_norm:
            y_ref[...] = _rmsnorm(out, g_final[...])
    else:
        out_ref[...] = out.reshape(nt, nb, D_MODEL)
        if final_norm:
            y_ref[...] = _rmsnorm(out, g_final[...]).reshape(nt, nb, D_MODEL)


def _mem_kv_kernel(mem_ref, g_ref, wk_ref, wv_ref, k_ref, v_ref, kb_ref, vb_ref):
    mn = _rmsnorm(mem_ref[0], g_ref[0]).astype(_BF16)
    k = _dot(mn, wk_ref[0])
    v = _dot(mn, wv_ref[0])
    k_ref[0, 0] = k
    v_ref[0, 0] = v
    kb_ref[0, 0] = k.astype(_BF16)
    vb_ref[0, 0] = v.astype(_BF16)


def _sample_attn_kernel(x_ref, g_ref, wq_ref, k_ref, v_ref, o_ref, *, n_seq, t_seq):
    xn = _rmsnorm(x_ref[...], g_ref[...]).astype(_BF16)
    o_ref[...] = _cross_attention(_dot(xn, wq_ref[...]), k_ref, v_ref, n_seq, t_seq)


def _resident(shape):
    return pl.BlockSpec(shape, lambda *_: (0,) * len(shape), pipeline_mode=pl.Buffered(1))


def _params(semantics):
    return pltpu.CompilerParams(dimension_semantics=semantics, vmem_limit_bytes=_VMEM_LIMIT_BYTES)


def _mixer_weights(lw):
    names = ('g_mix', 'w_in', 'w_gate', 'b_gate', 'pool_w', 'pool_scale', 'rnn_conv_w', 'rnn_conv_b',
             'rnn_wax', 'rnn_ba', 'rnn_bx', 'rnn_lambda', 'w_br_pool', 'w_br_rnn', 'w_br_attn', 'w_out')
    arrs = [lw[n] for n in names]
    return arrs, [_resident(a.shape) for a in arrs]


def _mixer_prompt(h, lw, kb, vb, layer, n_seq):
    nt, nb = _PROMPT_STEPS, n_seq
    rows = nt * nb
    w_arrs, w_specs = _mixer_weights(lw)
    kv_spec = pl.BlockSpec((None,) + kb.shape[1:], lambda i: (layer, 0, 0, 0), pipeline_mode=pl.Buffered(1))
    tile = pl.BlockSpec((rows, D_MODEL), lambda i: (i, 0))
    out_shapes = (jax.ShapeDtypeStruct(h.shape, _F32),
                  jax.ShapeDtypeStruct(((POOL_BUF + 1) * nb, D_POOL), _F32),
                  jax.ShapeDtypeStruct(((RNN_CONV - 1) * nb, D_RNN), _F32),
                  jax.ShapeDtypeStruct((nb, D_RNN), _F32))
    whole = lambda s: pl.BlockSpec(s.shape, lambda i: (0,) * len(s.shape))
    return pl.pallas_call(
        functools.partial(_mixer_kernel, nb=nb, nt=nt, carried=True, pos0=0),
        grid=(h.shape[0] // rows,),
        in_specs=[tile] + w_specs + [kv_spec, kv_spec],
        out_specs=(tile,) + tuple(whole(s) for s in out_shapes[1:]),
        out_shape=out_shapes,
        scratch_shapes=[pltpu.VMEM((2, rows, rows), _BF16)],
        compiler_params=_params(("arbitrary",)),
        name="mixer_prompt",
    )(h, *w_arrs, kb, vb)


def _mixer_sample(h, lw, y_attn, pool_st, conv_st, h_st):
    nt, n_seq = h.shape[0], h.shape[1]
    nb = _SAMPLE_SEQS
    w_arrs, w_specs = _mixer_weights(lw)
    seq3 = lambda a: pl.BlockSpec((a.shape[0], nb, a.shape[2]), lambda i: (0, i, 0))
    h_spec = pl.BlockSpec((nb, D_RNN), lambda i: (i, 0))
    outs = (h, pool_st, conv_st, h_st)
    return pl.pallas_call(
        functools.partial(_mixer_kernel, nb=nb, nt=nt, carried=False, pos0=PAST_LEN),
        grid=(n_seq // nb,),
        in_specs=[seq3(h)] + w_specs + [seq3(y_attn), seq3(pool_st), seq3(conv_st), h_spec],
        out_specs=(seq3(h), seq3(pool_st), seq3(conv_st), h_spec),
        out_shape=tuple(jax.ShapeDtypeStruct(a.shape, _F32) for a in outs),
        compiler_params=_params(("arbitrary",)),
        name="mixer_sample",
    )(h, *w_arrs, y_attn, pool_st, conv_st, h_st)


def _ffn_weights(lw, g_final):
    arrs = [lw['g_ffn'], lw['w_up'], lw['ffn_conv_w'], lw['ffn_conv_b'], lw['w_down']]
    if g_final is not None:
        arrs.append(g_final)
    return arrs, [_resident(a.shape) for a in arrs]


def _ffn_prompt(h, lw, n_seq, g_final):
    nt, nb = _PROMPT_STEPS, n_seq
    rows = nt * nb
    w_arrs, w_specs = _ffn_weights(lw, g_final)
    tile = pl.BlockSpec((rows, D_MODEL), lambda i: (i, 0))
    st_shape = ((FFN_CONV - 1) * nb, D_FF)
    n_main = 2 if g_final is not None else 1
    return pl.pallas_call(
        functools.partial(_ffn_kernel, nb=nb, nt=nt, carried=True, final_norm=g_final is not None),
        grid=(h.shape[0] // rows,),
        in_specs=[tile] + w_specs,
        out_specs=(tile,) * n_main + (pl.BlockSpec(st_shape, lambda i: (0, 0)),),
        out_shape=(jax.ShapeDtypeStruct(h.shape, _F32),) * n_main + (jax.ShapeDtypeStruct(st_shape, _F32),),
        scratch_shapes=[pltpu.VMEM((rows, D_FF), _BF16)],
        compiler_params=_params(("arbitrary",)),
        name="ffn_prompt",
    )(h, *w_arrs)


def _ffn_sample(h, lw, st, g_final):
    nt, n_seq = h.shape[0], h.shape[1]
    nb = _SAMPLE_SEQS
    w_arrs, w_specs = _ffn_weights(lw, g_final)
    seq3 = lambda a: pl.BlockSpec((a.shape[0], nb, a.shape[2]), lambda i: (0, i, 0))
    n_main = 2 if g_final is not None else 1
    return pl.pallas_call(
        functools.partial(_ffn_kernel, nb=nb, nt=nt, carried=False, final_norm=g_final is not None),
        grid=(n_seq // nb,),
        in_specs=[seq3(h)] + w_specs + [seq3(st)],
        out_specs=(seq3(h),) * n_main + (seq3(st),),
        out_shape=(jax.ShapeDtypeStruct(h.shape, _F32),) * n_main + (jax.ShapeDtypeStruct(st.shape, _F32),),
        scratch_shapes=[pltpu.VMEM((nt * nb, D_FF), _BF16)],
        compiler_params=_params(("arbitrary",)),
        name="ffn_sample",
    )(h, *w_arrs, st)


def _mem_kv(mem, g_mem, w_k, w_v):
    n_seq, n_mem = mem.shape[0], mem.shape[1]
    per_layer = lambda a: pl.BlockSpec((1,) + a.shape[1:], lambda l, b: (l, 0, 0))
    kv_spec = pl.BlockSpec((1, 1, n_mem, D_Q), lambda l, b: (l, b, 0, 0))
    kv_shape = (DEPTH, n_seq, n_mem, D_Q)
    return pl.pallas_call(
        _mem_kv_kernel,
        grid=(DEPTH, n_seq),
        in_specs=[pl.BlockSpec((1, n_mem, D_MODEL), lambda l, b: (b, 0, 0)),
                  per_layer(g_mem), per_layer(w_k), per_layer(w_v)],
        out_specs=(kv_spec,) * 4,
        out_shape=(jax.ShapeDtypeStruct(kv_shape, _F32),) * 2 + (jax.ShapeDtypeStruct(kv_shape, _BF16),) * 2,
        compiler_params=_params(("arbitrary", "arbitrary")),
        name="mem_kv",
    )(mem, g_mem, w_k, w_v)


def _sample_attn(h_nat, g_mix, w_in, k, v, layer, t_seq):
    n_seq, n_mem = k.shape[1], k.shape[2]
    nb = _ATTN_SEQS
    rows = nb * t_seq
    kv_spec = pl.BlockSpec((None, nb, n_mem, D_Q), lambda i: (layer, i, 0, 0))
    q_cols = pl.BlockSpec((D_MODEL, D_Q), lambda i: (0, (D_POOL + D_RNN) // D_Q),
                          pipeline_mode=pl.Buffered(1))
    return pl.pallas_call(
        functools.partial(_sample_attn_kernel, n_seq=nb, t_seq=t_seq),
        grid=(n_seq // nb,),
        in_specs=[pl.BlockSpec((rows, D_MODEL), lambda i: (i, 0)), _resident(g_mix.shape),
                  q_cols, kv_spec, kv_spec],
        out_specs=pl.BlockSpec((rows, D_Q), lambda i: (i, 0)),
        out_shape=jax.ShapeDtypeStruct((n_seq * t_seq, D_Q), _F32),
        compiler_params=_params(("arbitrary",)),
        name="sample_attn",
    )(h_nat, g_mix, w_in, k, v)


def kernel(x_prompt, x_sample, mem_prompt, cache_mem_k, cache_mem_v, state_pool, state_rnn_conv, state_rnn_h, state_ffn_conv, g_mix, w_in, w_gate, b_gate, pool_w, pool_scale, rnn_conv_w, rnn_conv_b, rnn_wa, rnn_ba, rnn_wx, rnn_bx, rnn_lambda, g_mem, w_k, w_v, w_br_pool, w_br_rnn, w_br_attn, w_out, g_ffn, w_up, ffn_conv_w, ffn_conv_b, w_down, g_final):
    bp, tp, _ = x_prompt.shape
    bs, ts, _ = x_sample.shape
    n_mem = mem_prompt.shape[1]
    bf = lambda a: a.astype(_BF16)
    row = lambda a: a[:, None, :]
    layers = []
    for l in range(DEPTH):
        layers.append({
            'g_mix': row(g_mix)[l], 'w_in': bf(w_in[l]), 'w_gate': bf(w_gate[l]), 'b_gate': row(b_gate)[l],
            'pool_w': bf(pool_w[l]), 'pool_scale': row(pool_scale)[l],
            'rnn_conv_w': rnn_conv_w[l], 'rnn_conv_b': row(rnn_conv_b)[l],
            'rnn_wax': bf(jnp.concatenate([rnn_wa[l], rnn_wx[l]], axis=-1)),
            'rnn_ba': row(rnn_ba)[l], 'rnn_bx': row(rnn_bx)[l], 'rnn_lambda': row(rnn_lambda)[l],
            'w_br_pool': bf(w_br_pool[l]), 'w_br_rnn': bf(w_br_rnn[l]), 'w_br_attn': bf(w_br_attn[l]),
            'w_out': bf(w_out[l]),
            'g_ffn': row(g_ffn)[l], 'w_up': bf(w_up[l]), 'ffn_conv_w': ffn_conv_w[l],
            'ffn_conv_b': row(ffn_conv_b)[l], 'w_down': bf(w_down[l]),
        })
    g_fin = g_final[None, :]

    p_mk, p_mv, kb, vb = _mem_kv(mem_prompt, row(g_mem), bf(w_k), bf(w_v))
    cache_k = cache_mem_k.reshape(DEPTH, bs, n_mem, D_Q)
    cache_v = cache_mem_v.reshape(DEPTH, bs, n_mem, D_Q)

    time_major = lambda a: jnp.swapaxes(a, 0, 1)
    hp = time_major(x_prompt).reshape(tp * bp, D_MODEL)
    hs = time_major(x_sample)
    p_pool, p_rconv, p_rh, p_fconv = [], [], [], []
    s_pool, s_rconv, s_rh, s_fconv = [], [], [], []
    yp = ys = None
    for l, lw in enumerate(layers):
        last = l == DEPTH - 1
        hp, pool_st, conv_st, h_st = _mixer_prompt(hp, lw, kb, vb, l, bp)
        res = _ffn_prompt(hp, lw, bp, g_fin if last else None)
        hp, ffn_st = res[0], res[-1]
        if last:
            yp = res[1]
        p_pool.append(time_major(pool_st.reshape(POOL_BUF + 1, bp, D_POOL)[1:]))
        p_rconv.append(time_major(conv_st.reshape(RNN_CONV - 1, bp, D_RNN)))
        p_rh.append(h_st)
        p_fconv.append(time_major(ffn_st.reshape(FFN_CONV - 1, bp, D_FF)))

        hs_nat = time_major(hs).reshape(bs * ts, D_MODEL)
        o_nat = _sample_attn(hs_nat, lw['g_mix'], lw['w_in'], cache_k, cache_v, l, ts)
        y_attn = time_major(o_nat.reshape(bs, ts, D_Q))
        hs, pool_o, conv_o, h_o = _mixer_sample(
            hs, lw, y_attn, time_major(state_pool[l]), time_major(state_rnn_conv[l]), state_rnn_h[l])
        res = _ffn_sample(hs, lw, time_major(state_ffn_conv[l]), g_fin if last else None)
        hs, ffn_o = res[0], res[-1]
        if last:
            ys = res[1]
        s_pool.append(time_major(pool_o))
        s_rconv.append(time_major(conv_o))
        s_rh.append(h_o)
        s_fconv.append(time_major(ffn_o))

    y_prompt = time_major(yp.reshape(tp, bp, D_MODEL))
    y_sample = time_major(ys)
    head = (N_HEADS, HEAD_DIM)
    return (y_prompt, y_sample,
            jnp.stack(p_pool), jnp.stack(p_rconv), jnp.stack(p_rh), jnp.stack(p_fconv),
            p_mk.reshape(DEPTH, bp, n_mem, *head), p_mv.reshape(DEPTH, bp, n_mem, *head),
            jnp.stack(s_pool), jnp.stack(s_rconv), jnp.stack(s_rh), jnp.stack(s_fconv))
```

```python
import functools

import jax
import jax.numpy as jnp
from jax import lax
from jax.experimental import pallas as pl
from jax.experimental.pallas import tpu as pltpu

D_MODEL = 1024
DEPTH = 2
PAST_LEN = 16384
POOL_WINDOWS = (2, 4, 8, 16)
D_POOL = D_MODEL // 2
POOL_GROUP = D_POOL // len(POOL_WINDOWS)
POOL_BUF = max(POOL_WINDOWS) - 1
D_RNN = D_MODEL
N_RNN_BLOCKS = 8
RNN_BLOCK = D_RNN // N_RNN_BLOCKS
RNN_CONV = 4
RG_C = 8.0
N_HEADS = 4
HEAD_DIM = 128
D_Q = N_HEADS * HEAD_DIM
D_FF = 3 * D_MODEL
FFN_CONV = 3
EPS = 1e-6

_V7X_VMEM_BYTES = 64 * 1024 * 1024
_VMEM_LIMIT_BYTES = _V7X_VMEM_BYTES - 8 * 1024 * 1024

_PROMPT_STEPS = 32
_SAMPLE_SEQS = 32
_ATTN_SEQS = 8
_FF_CHUNK = 512

_F32 = jnp.float32
_BF16 = jnp.bfloat16


def _dot(a, b):
    return jnp.dot(a, b, preferred_element_type=_F32)


def _dot_nt(a, b):
    return lax.dot_general(a, b, (((1,), (1,)), ((), ())), preferred_element_type=_F32)


def _rmsnorm(x, g):
    return x * lax.rsqrt(jnp.mean(x * x, axis=-1, keepdims=True) + EPS) * g


def _softplus(x):
    return jnp.maximum(x, 0.0) + jnp.log1p(jnp.exp(-jnp.abs(x)))


def _softmax_rows(s):
    e = jnp.exp(s - jnp.max(s, axis=-1, keepdims=True))
    return e / jnp.sum(e, axis=-1, keepdims=True)


def _head(h):
    return slice(h * HEAD_DIM, (h + 1) * HEAD_DIM)


def _cross_attention(q, k_ref, v_ref, n_seq, t_seq):
    pairs = [(b, h) for b in range(n_seq) for h in range(N_HEADS)]
    rows = lambda b: slice(b * t_seq, (b + 1) * t_seq)
    s = jnp.concatenate(
        [_dot_nt(q[rows(b), _head(h)].astype(_BF16), k_ref[b, :, _head(h)].astype(_BF16)) for b, h in pairs],
        axis=0)
    p = _softmax_rows(s * (HEAD_DIM ** -0.5))
    o = [_dot(p[rows(n)].astype(_BF16), v_ref[b, :, _head(h)].astype(_BF16)) for n, (b, h) in enumerate(pairs)]
    return jnp.concatenate(
        [jnp.concatenate(o[b * N_HEADS:(b + 1) * N_HEADS], axis=1) for b in range(n_seq)], axis=0)


def _row_permutation(rows, inner):
    i = lax.broadcasted_iota(jnp.int32, (rows, rows), 0)
    j = lax.broadcasted_iota(jnp.int32, (rows, rows), 1)
    shift = inner.bit_length() - 1
    src = (i & (inner - 1)) * (rows // inner) + (i >> shift)
    return jnp.where(j == src, 1.0, 0.0).astype(_BF16)


def _pool_window_sums(ext, rows, nb):
    sums = []
    for g, w in enumerate(POOL_WINDOWS):
        s = ext[:, g * POOL_GROUP:(g + 1) * POOL_GROUP]
        off, k = 0, 1
        while k < w:
            s = s[k * nb:] + s[:-k * nb]
            off += k * nb
            k *= 2
        start = (POOL_BUF + 1) * nb - off
        sums.append(s[start:start + rows])
    return sums


_LANES = 128


def _load_time_major(x_ref, scr, nb, nt):
    for b in range(nb):
        for c in range(scr.shape[0]):
            scr[c, pl.ds(b, nt, stride=nb), :] = x_ref[b, :, c * _LANES:(c + 1) * _LANES]
    return jnp.concatenate([scr[c] for c in range(scr.shape[0])], axis=1)


def _store_from_time_major(y, y_ref, scr, nb, nt):
    for c in range(scr.shape[0]):
        scr[c] = y[:, c * _LANES:(c + 1) * _LANES]
    for b in range(nb):
        for c in range(scr.shape[0]):
            y_ref[b, :, c * _LANES:(c + 1) * _LANES] = scr[c, pl.ds(b, nt, stride=nb), :]


def _mixer_kernel(*refs, nb, nt, carried, pos0, natural_in=False):
    rows = nb * nt
    (x_ref, g_mix, w_in, w_gate, b_gate, pool_w, pool_scale, conv_w, conv_b, w_ax,
     b_a, b_x, lam, w_brp, w_brr, w_bra, w_out) = refs[:17]
    if carried:
        k_ref, v_ref, out_ref, pool_st, conv_st, h_st, perm_scr = refs[17:24]

        @pl.when(pl.program_id(0) == 0)
        def _():
            pool_st[...] = jnp.zeros_like(pool_st)
            conv_st[...] = jnp.zeros_like(conv_st)
            h_st[...] = jnp.zeros_like(h_st)
            perm_scr[0] = _row_permutation(rows, nt)
            perm_scr[1] = _row_permutation(rows, nb)

        x = _load_time_major(x_ref, refs[24], nb, nt) if natural_in else x_ref[...]
        pool_hist = pool_st[...]
        conv_hist = conv_st[...]
        h0 = h_st[...]
    else:
        (yattn_ref, pool_in, conv_in, h_in, out_ref, pool_out, conv_out, h_out) = refs[17:]
        x = x_ref[...].reshape(rows, D_MODEL)
        pool_hist = jnp.concatenate(
            [jnp.zeros((nb, D_POOL), _F32), pool_in[...].reshape(POOL_BUF * nb, D_POOL)], axis=0)
        conv_hist = conv_in[...].reshape((RNN_CONV - 1) * nb, D_RNN)
        h0 = h_in[...]

    xn = _rmsnorm(x, g_mix[...]).astype(_BF16)

    u_pool = _dot(xn, w_in[:, 0:D_POOL])
    ext = jnp.concatenate([pool_hist, u_pool], axis=0)
    wins = _pool_window_sums(ext, rows, nb)
    if pos0 < POOL_BUF:
        row = lax.broadcasted_iota(jnp.int32, (rows, POOL_GROUP), 0)
        pos1 = pos0 + pl.program_id(0) * nt + (row >> (nb.bit_length() - 1)) + 1
    ys = []
    for g, w in enumerate(POOL_WINDOWS):
        if pos0 < POOL_BUF:
            mean = wins[g] / jnp.minimum(pos1, w).astype(_F32)
        else:
            mean = wins[g] * (1.0 / w)
        d = mean - u_pool[:, g * POOL_GROUP:(g + 1) * POOL_GROUP]
        ys.append(_dot(d.astype(_BF16), pool_w[g]))
    y_pool = jnp.concatenate(ys, axis=1) * pool_scale[...]
    new_pool_hist = ext[rows:]
    gate = jax.nn.sigmoid(_dot(xn, w_gate[:, 0:D_MODEL]) + b_gate[:, 0:D_MODEL])
    merged = gate * _dot(y_pool.astype(_BF16), w_brp[...])

    u_rnn = _dot(xn, w_in[:, D_POOL:D_POOL + D_RNN])
    ext_r = jnp.concatenate([conv_hist, u_rnn], axis=0)
    xc = conv_b[...]
    for j in range(RNN_CONV):
        xc = xc + ext_r[j * nb:j * nb + rows] * conv_w[j:j + 1, :]
    new_conv_hist = ext_r[rows:]
    xcb = xc.astype(_BF16)
    log_a_unit = -RG_C * _softplus(-lam[...])
    cols, h_last = [], []
    for n in range(N_RNN_BLOCKS):
        sl = slice(n * RNN_BLOCK, (n + 1) * RNN_BLOCK)
        ri = _dot(xcb[:, sl], w_ax[n])
        r = jax.nn.sigmoid(ri[:, :RNN_BLOCK] + b_a[:, sl])
        i = jax.nn.sigmoid(ri[:, RNN_BLOCK:] + b_x[:, sl])
        log_a = r * log_a_unit[:, sl]
        a = jnp.exp(log_a)
        th = jnp.tanh(log_a)
        b = jnp.sqrt(-2.0 * th / (1.0 - th)) * (i * xc[:, sl])
        h = h0[:, sl]
        hs = []
        for t in range(nt):
            h = a[t * nb:(t + 1) * nb] * h + b[t * nb:(t + 1) * nb]
            hs.append(h)
        cols.append(jnp.concatenate(hs, axis=0))
        h_last.append(h)
    y_rnn = jnp.concatenate(cols, axis=1)
    new_h = jnp.concatenate(h_last, axis=1)
    gate = jax.nn.sigmoid(_dot(xn, w_gate[:, D_MODEL:2 * D_MODEL]) + b_gate[:, D_MODEL:2 * D_MODEL])
    merged = merged + gate * _dot(y_rnn.astype(_BF16), w_brr[...])

    if carried:
        q = _dot(xn, w_in[:, D_POOL + D_RNN:]).astype(_BF16)
        q_seq = _dot(perm_scr[0], q)
        o_seq = _cross_attention(q_seq, k_ref, v_ref, nb, nt).astype(_BF16)
        y_attn = _dot(perm_scr[1], o_seq)
    else:
        y_attn = yattn_ref[...].reshape(rows, D_Q)
    gate = jax.nn.sigmoid(_dot(xn, w_gate[:, 2 * D_MODEL:]) + b_gate[:, 2 * D_MODEL:])
    merged = merged + gate * _dot(y_attn.astype(_BF16), w_bra[...])

    out = x + _dot(merged.astype(_BF16), w_out[...])
    if carried:
        out_ref[...] = out
        pool_st[...] = new_pool_hist
        conv_st[...] = new_conv_hist
        h_st[...] = new_h
    else:
        out_ref[...] = out.reshape(nt, nb, D_MODEL)
        pool_out[...] = new_pool_hist[nb:].reshape(POOL_BUF, nb, D_POOL)
        conv_out[...] = new_conv_hist.reshape(RNN_CONV - 1, nb, D_RNN)
        h_out[...] = new_h


def _ffn_kernel(*refs, nb, nt, carried, final_norm):
    rows = nb * nt
    x_ref, g_ffn, w_up, conv_w, conv_b, w_down = refs[:6]
    rest = list(refs[6:])
    g_final = rest.pop(0) if final_norm else None
    st_in = None if carried else rest.pop(0)
    out_ref, st_out, act_scr = rest[:3]

    if carried:
        @pl.when(pl.program_id(0) == 0)
        def _():
            st_out[...] = jnp.zeros_like(st_out)

        x = x_ref[...]
    else:
        x = x_ref[...].reshape(rows, D_MODEL)

    xn = _rmsnorm(x, g_ffn[...]).astype(_BF16)
    hist_rows = (FFN_CONV - 1) * nb
    for c in range(0, D_FF, _FF_CHUNK):
        sl = slice(c, c + _FF_CHUNK)
        g_pre = _dot(xn, w_up[:, sl])
        val = _dot(xn, w_up[:, D_FF + c:D_FF + c + _FF_CHUNK])
        if carried:
            hist = st_out[:, sl]
        else:
            hist = st_in[:, :, sl].reshape(hist_rows, _FF_CHUNK)
        ext = jnp.concatenate([hist, g_pre], axis=0)
        g_conv = conv_b[:, sl]
        for j in range(FFN_CONV):
            g_conv = g_conv + ext[j * nb:j * nb + rows] * conv_w[j:j + 1, sl]
        if carried:
            st_out[:, sl] = ext[rows:]
        else:
            st_out[:, :, sl] = ext[rows:].reshape(FFN_CONV - 1, nb, _FF_CHUNK)
        act_scr[:, sl] = (jax.nn.gelu(g_conv) * val).astype(_BF16)
    out = x + _dot(act_scr[...], w_down[...])
    if final_norm:
        out = _rmsnorm(out, g_final[...])
    if not carried:
        out_ref[...] = out.reshape(nt, nb, D_MODEL)
    elif final_norm:
        _store_from_time_major(out, out_ref, rest[3], nb, nt)
    else:
        out_ref[...] = out


def _mem_kv_kernel(mem_ref, g_ref, wk_ref, wv_ref, k_ref, v_ref, kb_ref, vb_ref):
    mn = _rmsnorm(mem_ref[0], g_ref[0]).astype(_BF16)
    k = _dot(mn, wk_ref[0])
    v = _dot(mn, wv_ref[0])
    k_ref[0, 0] = k
    v_ref[0, 0] = v
    kb_ref[0, 0] = k.astype(_BF16)
    vb_ref[0, 0] = v.astype(_BF16)


def _sample_attn_kernel(x_ref, g_ref, wq_ref, k_ref, v_ref, o_ref, *, n_seq, t_seq):
    xn = _rmsnorm(x_ref[...], g_ref[...]).astype(_BF16)
    q = _dot(xn, wq_ref[...])
    rows = lambda b: slice(b * t_seq, (b + 1) * t_seq)
    hq = N_HEADS * t_seq
    s = jnp.concatenate(
        [_dot_nt(jnp.concatenate([q[rows(b), _head(h)] for h in range(N_HEADS)], axis=0).astype(_BF16),
                 k_ref[b].astype(_BF16)) for b in range(n_seq)], axis=0)
    r = lax.broadcasted_iota(jnp.int32, s.shape, 0)
    c = lax.broadcasted_iota(jnp.int32, s.shape, 1)
    same_head = ((r >> (t_seq.bit_length() - 1)) & (N_HEADS - 1)) == (c & (N_HEADS - 1))
    p = _softmax_rows(jnp.where(same_head, s * (HEAD_DIM ** -0.5), -jnp.inf))
    for b in range(n_seq):
        o = _dot(p[b * hq:(b + 1) * hq].astype(_BF16), v_ref[b].astype(_BF16))
        o_ref[rows(b), :] = jnp.concatenate([o[rows(h)] for h in range(N_HEADS)], axis=1)


def _resident(shape):
    return pl.BlockSpec(shape, lambda *_: (0,) * len(shape), pipeline_mode=pl.Buffered(1))


def _params(semantics):
    return pltpu.CompilerParams(dimension_semantics=semantics, vmem_limit_bytes=_VMEM_LIMIT_BYTES)


def _mixer_weights(lw):
    names = ('g_mix', 'w_in', 'w_gate', 'b_gate', 'pool_w', 'pool_scale', 'rnn_conv_w', 'rnn_conv_b',
             'rnn_wax', 'rnn_ba', 'rnn_bx', 'rnn_lambda', 'w_br_pool', 'w_br_rnn', 'w_br_attn', 'w_out')
    arrs = [lw[n] for n in names]
    return arrs, [_resident(a.shape) for a in arrs]


def _mixer_prompt(h, lw, kb, vb, layer, n_seq):
    nt, nb = _PROMPT_STEPS, n_seq
    rows = nt * nb
    natural_in = h.ndim == 3
    n_tok = h.shape[0] * h.shape[1] if natural_in else h.shape[0]
    w_arrs, w_specs = _mixer_weights(lw)
    kv_spec = pl.BlockSpec((None,) + kb.shape[1:], lambda i: (layer, 0, 0, 0), pipeline_mode=pl.Buffered(1))
    tile = pl.BlockSpec((rows, D_MODEL), lambda i: (i, 0))
    in_tile = pl.BlockSpec((nb, nt, D_MODEL), lambda i: (0, i, 0)) if natural_in else tile
    scratch = [pltpu.VMEM((2, rows, rows), _BF16)]
    if natural_in:
        scratch.append(pltpu.VMEM((D_MODEL // _LANES, rows, _LANES), _F32))
    out_shapes = (jax.ShapeDtypeStruct((n_tok, D_MODEL), _F32),
                  jax.ShapeDtypeStruct(((POOL_BUF + 1) * nb, D_POOL), _F32),
                  jax.ShapeDtypeStruct(((RNN_CONV - 1) * nb, D_RNN), _F32),
                  jax.ShapeDtypeStruct((nb, D_RNN), _F32))
    whole = lambda s: pl.BlockSpec(s.shape, lambda i: (0,) * len(s.shape))
    return pl.pallas_call(
        functools.partial(_mixer_kernel, nb=nb, nt=nt, carried=True, pos0=0, natural_in=natural_in),
        grid=(n_tok // rows,),
        in_specs=[in_tile] + w_specs + [kv_spec, kv_spec],
        out_specs=(tile,) + tuple(whole(s) for s in out_shapes[1:]),
        out_shape=out_shapes,
        scratch_shapes=scratch,
        compiler_params=_params(("arbitrary",)),
        name="mixer_prompt",
    )(h, *w_arrs, kb, vb)


def _mixer_sample(h, lw, y_attn, pool_st, conv_st, h_st):
    nt, n_seq = h.shape[0], h.shape[1]
    nb = _SAMPLE_SEQS
    w_arrs, w_specs = _mixer_weights(lw)
    seq3 = lambda a: pl.BlockSpec((a.shape[0], nb, a.shape[2]), lambda i: (0, i, 0))
    h_spec = pl.BlockSpec((nb, D_RNN), lambda i: (i, 0))
    outs = (h, pool_st, conv_st, h_st)
    return pl.pallas_call(
        functools.partial(_mixer_kernel, nb=nb, nt=nt, carried=False, pos0=PAST_LEN),
        grid=(n_seq // nb,),
        in_specs=[seq3(h)] + w_specs + [seq3(y_attn), seq3(pool_st), seq3(conv_st), h_spec],
        out_specs=(seq3(h), seq3(pool_st), seq3(conv_st), h_spec),
        out_shape=tuple(jax.ShapeDtypeStruct(a.shape, _F32) for a in outs),
        compiler_params=_params(("arbitrary",)),
        name="mixer_sample",
    )(h, *w_arrs, y_attn, pool_st, conv_st, h_st)


def _ffn_weights(lw, g_final):
    arrs = [lw['g_ffn'], lw['w_up'], lw['ffn_conv_w'], lw['ffn_conv_b'], lw['w_down']]
    if g_final is not None:
        arrs.append(g_final)
    return arrs, [_resident(a.shape) for a in arrs]


def _ffn_prompt(h, lw, n_seq, g_final):
    nt, nb = _PROMPT_STEPS, n_seq
    rows = nt * nb
    w_arrs, w_specs = _ffn_weights(lw, g_final)
    tile = pl.BlockSpec((rows, D_MODEL), lambda i: (i, 0))
    st_shape = ((FFN_CONV - 1) * nb, D_FF)
    scratch = [pltpu.VMEM((rows, D_FF), _BF16)]
    if g_final is None:
        out_tile, out_shape = tile, h.shape
    else:
        out_tile, out_shape = pl.BlockSpec((nb, nt, D_MODEL), lambda i: (0, i, 0)), (nb, h.shape[0] // nb, D_MODEL)
        scratch.append(pltpu.VMEM((D_MODEL // _LANES, rows, _LANES), _F32))
    return pl.pallas_call(
        functools.partial(_ffn_kernel, nb=nb, nt=nt, carried=True, final_norm=g_final is not None),
        grid=(h.shape[0] // rows,),
        in_specs=[tile] + w_specs,
        out_specs=(out_tile, pl.BlockSpec(st_shape, lambda i: (0, 0))),
        out_shape=(jax.ShapeDtypeStruct(out_shape, _F32), jax.ShapeDtypeStruct(st_shape, _F32)),
        scratch_shapes=scratch,
        compiler_params=_params(("arbitrary",)),
        name="ffn_prompt",
    )(h, *w_arrs)


def _ffn_sample(h, lw, st, g_final):
    nt, n_seq = h.shape[0], h.shape[1]
    nb = _SAMPLE_SEQS
    w_arrs, w_specs = _ffn_weights(lw, g_final)
    seq3 = lambda a: pl.BlockSpec((a.shape[0], nb, a.shape[2]), lambda i: (0, i, 0))
    return pl.pallas_call(
        functools.partial(_ffn_kernel, nb=nb, nt=nt, carried=False, final_norm=g_final is not None),
        grid=(n_seq // nb,),
        in_specs=[seq3(h)] + w_specs + [seq3(st)],
        out_specs=(seq3(h), seq3(st)),
        out_shape=(jax.ShapeDtypeStruct(h.shape, _F32), jax.ShapeDtypeStruct(st.shape, _F32)),
        scratch_shapes=[pltpu.VMEM((nt * nb, D_FF), _BF16)],
        compiler_params=_params(("arbitrary",)),
        name="ffn_sample",
    )(h, *w_arrs, st)


def _mem_kv(mem, g_mem, w_k, w_v):
    n_seq, n_mem = mem.shape[0], mem.shape[1]
    per_layer = lambda a: pl.BlockSpec((1,) + a.shape[1:], lambda l, b: (l, 0, 0))
    kv_spec = pl.BlockSpec((1, 1, n_mem, D_Q), lambda l, b: (l, b, 0, 0))
    kv_shape = (DEPTH, n_seq, n_mem, D_Q)
    return pl.pallas_call(
        _mem_kv_kernel,
        grid=(DEPTH, n_seq),
        in_specs=[pl.BlockSpec((1, n_mem, D_MODEL), lambda l, b: (b, 0, 0)),
                  per_layer(g_mem), per_layer(w_k), per_layer(w_v)],
        out_specs=(kv_spec,) * 4,
        out_shape=(jax.ShapeDtypeStruct(kv_shape, _F32),) * 2 + (jax.ShapeDtypeStruct(kv_shape, _BF16),) * 2,
        compiler_params=_params(("arbitrary", "arbitrary")),
        name="mem_kv",
    )(mem, g_mem, w_k, w_v)


def _sample_attn(h_nat, g_mix, w_in, k, v, layer, t_seq):
    n_seq = k.shape[1]
    nb = _ATTN_SEQS
    rows = nb * t_seq
    kv_spec = pl.BlockSpec((None, nb) + k.shape[2:], lambda i: (layer, i, 0, 0))
    q_cols = pl.BlockSpec((D_MODEL, D_Q), lambda i: (0, (D_POOL + D_RNN) // D_Q),
                          pipeline_mode=pl.Buffered(1))
    return pl.pallas_call(
        functools.partial(_sample_attn_kernel, n_seq=nb, t_seq=t_seq),
        grid=(n_seq // nb,),
        in_specs=[pl.BlockSpec((rows, D_MODEL), lambda i: (i, 0)), _resident(g_mix.shape),
                  q_cols, kv_spec, kv_spec],
        out_specs=pl.BlockSpec((rows, D_Q), lambda i: (i, 0)),
        out_shape=jax.ShapeDtypeStruct((n_seq * t_seq, D_Q), _F32),
        compiler_params=_params(("arbitrary",)),
        name="sample_attn",
    )(h_nat, g_mix, w_in, k, v)


def kernel(x_prompt, x_sample, mem_prompt, cache_mem_k, cache_mem_v, state_pool, state_rnn_conv, state_rnn_h, state_ffn_conv, g_mix, w_in, w_gate, b_gate, pool_w, pool_scale, rnn_conv_w, rnn_conv_b, rnn_wa, rnn_ba, rnn_wx, rnn_bx, rnn_lambda, g_mem, w_k, w_v, w_br_pool, w_br_rnn, w_br_attn, w_out, g_ffn, w_up, ffn_conv_w, ffn_conv_b, w_down, g_final):
    bp, tp, _ = x_prompt.shape
    bs, ts, _ = x_sample.shape
    n_mem = mem_prompt.shape[1]
    bf = lambda a: a.astype(_BF16)
    row = lambda a: a[:, None, :]
    layers = []
    for l in range(DEPTH):
        layers.append({
            'g_mix': row(g_mix)[l], 'w_in': bf(w_in[l]), 'w_gate': bf(w_gate[l]), 'b_gate': row(b_gate)[l],
            'pool_w': bf(pool_w[l]), 'pool_scale': row(pool_scale)[l],
            'rnn_conv_w': rnn_conv_w[l], 'rnn_conv_b': row(rnn_conv_b)[l],
            'rnn_wax': bf(jnp.concatenate([rnn_wa[l], rnn_wx[l]], axis=-1)),
            'rnn_ba': row(rnn_ba)[l], 'rnn_bx': row(rnn_bx)[l], 'rnn_lambda': row(rnn_lambda)[l],
            'w_br_pool': bf(w_br_pool[l]), 'w_br_rnn': bf(w_br_rnn[l]), 'w_br_attn': bf(w_br_attn[l]),
            'w_out': bf(w_out[l]),
            'g_ffn': row(g_ffn)[l], 'w_up': bf(w_up[l]), 'ffn_conv_w': ffn_conv_w[l],
            'ffn_conv_b': row(ffn_conv_b)[l], 'w_down': bf(w_down[l]),
        })
    g_fin = g_final[None, :]

    p_mk, p_mv, kb, vb = _mem_kv(mem_prompt, row(g_mem), bf(w_k), bf(w_v))
    cache_k = cache_mem_k.reshape(DEPTH, bs, n_mem * N_HEADS, HEAD_DIM)
    cache_v = cache_mem_v.reshape(DEPTH, bs, n_mem * N_HEADS, HEAD_DIM)

    time_major = lambda a: jnp.swapaxes(a, 0, 1)
    hp = x_prompt
    hs = time_major(x_sample)
    p_pool, p_rconv, p_rh, p_fconv = [], [], [], []
    s_pool, s_rconv, s_rh, s_fconv = [], [], [], []
    for l, lw in enumerate(layers):
        last = l == DEPTH - 1
        hp, pool_st, conv_st, h_st = _mixer_prompt(hp, lw, kb, vb, l, bp)
        hp, ffn_st = _ffn_prompt(hp, lw, bp, g_fin if last else None)
        p_pool.append(time_major(pool_st.reshape(POOL_BUF + 1, bp, D_POOL)[1:]))
        p_rconv.append(time_major(conv_st.reshape(RNN_CONV - 1, bp, D_RNN)))
        p_rh.append(h_st)
        p_fconv.append(time_major(ffn_st.reshape(FFN_CONV - 1, bp, D_FF)))

        hs_nat = time_major(hs).reshape(bs * ts, D_MODEL)
        o_nat = _sample_attn(hs_nat, lw['g_mix'], lw['w_in'], cache_k, cache_v, l, ts)
        y_attn = time_major(o_nat.reshape(bs, ts, D_Q))
        hs, pool_o, conv_o, h_o = _mixer_sample(
            hs, lw, y_attn, time_major(state_pool[l]), time_major(state_rnn_conv[l]), state_rnn_h[l])
        hs, ffn_o = _ffn_sample(hs, lw, time_major(state_ffn_conv[l]), g_fin if last else None)
        s_pool.append(time_major(pool_o))
        s_rconv.append(time_major(conv_o))
        s_rh.append(h_o)
        s_fconv.append(time_major(ffn_o))

    head = (N_HEADS, HEAD_DIM)
    return (hp, time_major(hs),
            jnp.stack(p_pool), jnp.stack(p_rconv), jnp.stack(p_rh), jnp.stack(p_fconv),
            p_mk.reshape(DEPTH, bp, n_mem, *head), p_mv.reshape(DEPTH, bp, n_mem, *head),
            jnp.stack(s_pool), jnp.stack(s_rconv), jnp.stack(s_rh), jnp.stack(s_fconv))
```

```python
import functools

import jax
import jax.numpy as jnp
from jax import lax
from jax.experimental import pallas as pl
from jax.experimental.pallas import tpu as pltpu

D_MODEL = 1024
DEPTH = 2
PAST_LEN = 16384
POOL_WINDOWS = (2, 4, 8, 16)
D_POOL = D_MODEL // 2
POOL_GROUP = D_POOL // len(POOL_WINDOWS)
POOL_BUF = max(POOL_WINDOWS) - 1
D_RNN = D_MODEL
N_RNN_BLOCKS = 8
RNN_BLOCK = D_RNN // N_RNN_BLOCKS
RNN_CONV = 4
RG_C = 8.0
N_HEADS = 4
HEAD_DIM = 128
D_Q = N_HEADS * HEAD_DIM
N_BRANCH = 3
D_FF = 3 * D_MODEL
FFN_CONV = 3
EPS = 1e-6

_V7X_VMEM_BYTES = 64 * 1024 * 1024
_VMEM_LIMIT_BYTES = _V7X_VMEM_BYTES - 8 * 1024 * 1024

_PROMPT_STEPS = 64
_SAMPLE_SEQS = 32
_ATTN_SEQS = 8
_FF_CHUNK = 512

_MXU_COLS = 256

_F32 = jnp.float32
_BF16 = jnp.bfloat16


def _dot(a, b):
    return jnp.dot(a, b, preferred_element_type=_F32)


def _dot_nt(a, b):
    return lax.dot_general(a, b, (((1,), (1,)), ((), ())), preferred_element_type=_F32)


def _rmsnorm(x, g):
    return x * lax.rsqrt(jnp.mean(x * x, axis=-1, keepdims=True) + EPS) * g


def _softplus(x):
    return jnp.maximum(x, 0.0) + jnp.log1p(jnp.exp(-jnp.abs(x)))


def _softmax_rows(s):
    e = jnp.exp(s - jnp.max(s, axis=-1, keepdims=True))
    return e / jnp.sum(e, axis=-1, keepdims=True)


def _head(h):
    return slice(h * HEAD_DIM, (h + 1) * HEAD_DIM)


def _attention_scores(q, k_ref, n_seq, t_seq):
    rows = lambda b: slice(b * t_seq, (b + 1) * t_seq)
    return jnp.concatenate(
        [_dot_nt(q[rows(b), _head(h)].astype(_BF16), k_ref[b, :, _head(h)])
         for b in range(n_seq) for h in range(N_HEADS)], axis=0)


def _attention_values(p, v_ref, n_seq, t_seq):
    rows = lambda n: slice(n * t_seq, (n + 1) * t_seq)
    return jnp.concatenate(
        [jnp.concatenate([_dot(p[rows(b * N_HEADS + h)].astype(_BF16), v_ref[b, :, _head(h)])
                          for h in range(N_HEADS)], axis=1) for b in range(n_seq)], axis=0)


def _row_permutation(rows, inner):
    i = lax.broadcasted_iota(jnp.int32, (rows, rows), 0)
    j = lax.broadcasted_iota(jnp.int32, (rows, rows), 1)
    shift = inner.bit_length() - 1
    src = (i & (inner - 1)) * (rows // inner) + (i >> shift)
    return jnp.where(j == src, 1.0, 0.0).astype(_BF16)


def _pool_window_sums(ext, rows, nb):
    sums = []
    for g, w in enumerate(POOL_WINDOWS):
        s = ext[:, g * POOL_GROUP:(g + 1) * POOL_GROUP]
        off, k = 0, 1
        while k < w:
            s = s[k * nb:] + s[:-k * nb]
            off += k * nb
            k *= 2
        start = (POOL_BUF + 1) * nb - off
        sums.append(s[start:start + rows])
    return sums


_LANES = 128


def _load_time_major(x_ref, scr, nb, nt):
    for b in range(nb):
        for c in range(scr.shape[0]):
            scr[c, pl.ds(b, nt, stride=nb), :] = x_ref[b, :, c * _LANES:(c + 1) * _LANES]
    return jnp.concatenate([scr[c] for c in range(scr.shape[0])], axis=1)


def _store_from_time_major(y, y_ref, scr, nb, nt):
    for c in range(scr.shape[0]):
        scr[c] = y[:, c * _LANES:(c + 1) * _LANES]
    for b in range(nb):
        for c in range(scr.shape[0]):
            y_ref[b, :, c * _LANES:(c + 1) * _LANES] = scr[c, pl.ds(b, nt, stride=nb), :]


def _mixer_kernel(*refs, nb, nt, carried, pos0, natural_in=False):
    rows = nb * nt
    (x_ref, g_mix, w_in, w_gate, b_gate, pool_w, pool_scale, conv_w, conv_b, w_ax,
     b_a, b_x, lam, w_brp, w_brr, w_bra, w_out) = refs[:17]
    if carried:
        k_ref, v_ref, out_ref, pool_st, conv_st, h_st, perm_scr = refs[17:24]

        @pl.when(pl.program_id(0) == 0)
        def _():
            pool_st[...] = jnp.zeros_like(pool_st)
            conv_st[...] = jnp.zeros_like(conv_st)
            h_st[...] = jnp.zeros_like(h_st)
            perm_scr[0] = _row_permutation(rows, nt)
            perm_scr[1] = _row_permutation(rows, nb)

        x = _load_time_major(x_ref, refs[24], nb, nt) if natural_in else x_ref[...]
        pool_hist = pool_st[...]
        conv_hist = conv_st[...]
        h0 = h_st[...]
    else:
        (yattn_ref, pool_in, conv_in, h_in, out_ref, pool_out, conv_out, h_out) = refs[17:]
        x = x_ref[...].reshape(rows, D_MODEL)
        pool_hist = jnp.concatenate(
            [jnp.zeros((nb, D_POOL), _F32), pool_in[...].reshape(POOL_BUF * nb, D_POOL)], axis=0)
        conv_hist = conv_in[...].reshape((RNN_CONV - 1) * nb, D_RNN)
        h0 = h_in[...]

    xn = _rmsnorm(x, g_mix[...]).astype(_BF16)

    n_gate_chunks = N_BRANCH * D_MODEL // _MXU_COLS
    gate = [None] * n_gate_chunks

    def gate_chunk(j):
        sl = slice(j * _MXU_COLS, (j + 1) * _MXU_COLS)
        return jax.nn.sigmoid(_dot(xn, w_gate[:, sl]) + b_gate[:, sl])

    u_rnn = _dot(xn, w_in[:, D_POOL:D_POOL + D_RNN])
    u_pool = _dot(xn, w_in[:, 0:D_POOL])
    if carried:
        q = _dot(xn, w_in[:, D_POOL + D_RNN:]).astype(_BF16)

    ext_r = jnp.concatenate([conv_hist, u_rnn], axis=0)
    xc = conv_b[...]
    for j in range(RNN_CONV):
        xc = xc + ext_r[j * nb:j * nb + rows] * conv_w[j:j + 1, :]
    new_conv_hist = ext_r[rows:]
    xcb = xc.astype(_BF16)
    log_a_unit = -RG_C * _softplus(-lam[...])
    cols, h_last = [], []
    for n in range(N_RNN_BLOCKS):
        sl = slice(n * RNN_BLOCK, (n + 1) * RNN_BLOCK)
        ri = _dot(xcb[:, sl], w_ax[n])
        r = jax.nn.sigmoid(ri[:, :RNN_BLOCK] + b_a[:, sl])
        i = jax.nn.sigmoid(ri[:, RNN_BLOCK:] + b_x[:, sl])
        log_a = r * log_a_unit[:, sl]
        a = jnp.exp(log_a)
        th = jnp.tanh(log_a)
        b = jnp.sqrt(-2.0 * th / (1.0 - th)) * (i * xc[:, sl])
        h = h0[:, sl]
        hs = []
        for t in range(nt):
            h = a[t * nb:(t + 1) * nb] * h + b[t * nb:(t + 1) * nb]
            hs.append(h)
        cols.append(jnp.concatenate(hs, axis=0))
        h_last.append(h)
        for j in range(n * n_gate_chunks // N_RNN_BLOCKS, (n + 1) * n_gate_chunks // N_RNN_BLOCKS):
            gate[j] = gate_chunk(j)
    y_rnn = jnp.concatenate(cols, axis=1).astype(_BF16)
    new_h = jnp.concatenate(h_last, axis=1)

    if carried:
        s = _attention_scores(_dot(perm_scr[0], q), k_ref, nb, nt)

    ext = jnp.concatenate([pool_hist, u_pool], axis=0)
    wins = _pool_window_sums(ext, rows, nb)
    if pos0 < POOL_BUF:
        row = lax.broadcasted_iota(jnp.int32, (rows, POOL_GROUP), 0)
        pos1 = pos0 + pl.program_id(0) * nt + (row >> (nb.bit_length() - 1)) + 1
    ys = []
    for g, w in enumerate(POOL_WINDOWS):
        if pos0 < POOL_BUF:
            mean = wins[g] / jnp.minimum(pos1, w).astype(_F32)
        else:
            mean = wins[g] * (1.0 / w)
        d = mean - u_pool[:, g * POOL_GROUP:(g + 1) * POOL_GROUP]
        ys.append(_dot(d.astype(_BF16), pool_w[g]))
    y_pool = (jnp.concatenate(ys, axis=1) * pool_scale[...]).astype(_BF16)
    new_pool_hist = ext[rows:]

    if carried:
        o_seq = _attention_values(_softmax_rows(s * (HEAD_DIM ** -0.5)), v_ref, nb, nt).astype(_BF16)
        y_attn = _dot(perm_scr[1], o_seq).astype(_BF16)
    else:
        y_attn = yattn_ref[...].reshape(rows, D_Q).astype(_BF16)

    out = x
    chunks_per_branch = D_MODEL // _MXU_COLS
    for c in range(chunks_per_branch):
        sl = slice(c * _MXU_COLS, (c + 1) * _MXU_COLS)
        merged = None
        for k, (y, w) in enumerate(((y_pool, w_brp), (y_rnn, w_brr), (y_attn, w_bra))):
            term = gate[k * chunks_per_branch + c] * _dot(y, w[:, sl])
            merged = term if merged is None else merged + term
        out = out + _dot(merged.astype(_BF16), w_out[sl, :])
    if carried:
        out_ref[...] = out
        pool_st[...] = new_pool_hist
        conv_st[...] = new_conv_hist
        h_st[...] = new_h
    else:
        out_ref[...] = out.reshape(nt, nb, D_MODEL)
        pool_out[...] = new_pool_hist[nb:].reshape(POOL_BUF, nb, D_POOL)
        conv_out[...] = new_conv_hist.reshape(RNN_CONV - 1, nb, D_RNN)
        h_out[...] = new_h


def _ffn_kernel(*refs, nb, nt, carried, final_norm):
    rows = nb * nt
    x_ref, g_ffn, w_up, conv_w, conv_b, w_down = refs[:6]
    rest = list(refs[6:])
    g_final = rest.pop(0) if final_norm else None
    st_in = None if carried else rest.pop(0)
    out_ref, st_out, act_scr = rest[:3]

    if carried:
        @pl.when(pl.program_id(0) == 0)
        def _():
            st_out[...] = jnp.zeros_like(st_out)

        x = x_ref[...]
    else:
        x = x_ref[...].reshape(rows, D_MODEL)

    xn = _rmsnorm(x, g_ffn[...]).astype(_BF16)
    hist_rows = (FFN_CONV - 1) * nb
    for c in range(0, D_FF, _FF_CHUNK):
        sl = slice(c, c + _FF_CHUNK)
        g_pre = _dot(xn, w_up[:, sl])
        val = _dot(xn, w_up[:, D_FF + c:D_FF + c + _FF_CHUNK])
        if carried:
            hist = st_out[:, sl]
        else:
            hist = st_in[:, :, sl].reshape(hist_rows, _FF_CHUNK)
        ext = jnp.concatenate([hist, g_pre], axis=0)
        g_conv = conv_b[:, sl]
        for j in range(FFN_CONV):
            g_conv = g_conv + ext[j * nb:j * nb + rows] * conv_w[j:j + 1, sl]
        if carried:
            st_out[:, sl] = ext[rows:]
        else:
            st_out[:, :, sl] = ext[rows:].reshape(FFN_CONV - 1, nb, _FF_CHUNK)
        act_scr[:, sl] = (jax.nn.gelu(g_conv) * val).astype(_BF16)
    out = x + _dot(act_scr[...], w_down[...])
    if final_norm:
        out = _rmsnorm(out, g_final[...])
    if not carried:
        out_ref[...] = out.reshape(nt, nb, D_MODEL)
    elif final_norm:
        _store_from_time_major(out, out_ref, rest[3], nb, nt)
    else:
        out_ref[...] = out


def _mem_kv_kernel(mem_ref, g_ref, wk_ref, wv_ref, k_ref, v_ref, kb_ref, vb_ref):
    mn = _rmsnorm(mem_ref[0], g_ref[0]).astype(_BF16)
    k = _dot(mn, wk_ref[0])
    v = _dot(mn, wv_ref[0])
    k_ref[0, 0] = k
    v_ref[0, 0] = v
    kb_ref[0, 0] = k.astype(_BF16)
    vb_ref[0, 0] = v.astype(_BF16)


def _sample_attn_kernel(x_ref, g_ref, wq_ref, k_ref, v_ref, o_ref, *, n_seq, t_seq):
    xn = _rmsnorm(x_ref[...], g_ref[...]).astype(_BF16)
    q = _dot(xn, wq_ref[...])
    rows = lambda b: slice(b * t_seq, (b + 1) * t_seq)
    hq = N_HEADS * t_seq
    s = jnp.concatenate(
        [_dot_nt(jnp.concatenate([q[rows(b), _head(h)] for h in range(N_HEADS)], axis=0).astype(_BF16),
                 k_ref[b].astype(_BF16)) for b in range(n_seq)], axis=0)
    r = lax.broadcasted_iota(jnp.int32, s.shape, 0)
    c = lax.broadcasted_iota(jnp.int32, s.shape, 1)
    same_head = ((r >> (t_seq.bit_length() - 1)) & (N_HEADS - 1)) == (c & (N_HEADS - 1))
    p = _softmax_rows(jnp.where(same_head, s * (HEAD_DIM ** -0.5), -jnp.inf))
    for b in range(n_seq):
        o = _dot(p[b * hq:(b + 1) * hq].astype(_BF16), v_ref[b].astype(_BF16))
        o_ref[rows(b), :] = jnp.concatenate([o[rows(h)] for h in range(N_HEADS)], axis=1)


def _resident(shape):
    return pl.BlockSpec(shape, lambda *_: (0,) * len(shape), pipeline_mode=pl.Buffered(1))


def _params(semantics):
    return pltpu.CompilerParams(dimension_semantics=semantics, vmem_limit_bytes=_VMEM_LIMIT_BYTES)


def _mixer_weights(lw):
    names = ('g_mix', 'w_in', 'w_gate', 'b_gate', 'pool_w', 'pool_scale', 'rnn_conv_w', 'rnn_conv_b',
             'rnn_wax', 'rnn_ba', 'rnn_bx', 'rnn_lambda', 'w_br_pool', 'w_br_rnn', 'w_br_attn', 'w_out')
    arrs = [lw[n] for n in names]
    return arrs, [_resident(a.shape) for a in arrs]


def _mixer_prompt(h, lw, kb, vb, layer, n_seq):
    nt, nb = _PROMPT_STEPS, n_seq
    rows = nt * nb
    natural_in = h.ndim == 3
    n_tok = h.shape[0] * h.shape[1] if natural_in else h.shape[0]
    w_arrs, w_specs = _mixer_weights(lw)
    kv_spec = pl.BlockSpec((None,) + kb.shape[1:], lambda i: (layer, 0, 0, 0), pipeline_mode=pl.Buffered(1))
    tile = pl.BlockSpec((rows, D_MODEL), lambda i: (i, 0))
    in_tile = pl.BlockSpec((nb, nt, D_MODEL), lambda i: (0, i, 0)) if natural_in else tile
    scratch = [pltpu.VMEM((2, rows, rows), _BF16)]
    if natural_in:
        scratch.append(pltpu.VMEM((D_MODEL // _LANES, rows, _LANES), _F32))
    out_shapes = (jax.ShapeDtypeStruct((n_tok, D_MODEL), _F32),
                  jax.ShapeDtypeStruct(((POOL_BUF + 1) * nb, D_POOL), _F32),
                  jax.ShapeDtypeStruct(((RNN_CONV - 1) * nb, D_RNN), _F32),
                  jax.ShapeDtypeStruct((nb, D_RNN), _F32))
    whole = lambda s: pl.BlockSpec(s.shape, lambda i: (0,) * len(s.shape))
    return pl.pallas_call(
        functools.partial(_mixer_kernel, nb=nb, nt=nt, carried=True, pos0=0, natural_in=natural_in),
        grid=(n_tok // rows,),
        in_specs=[in_tile] + w_specs + [kv_spec, kv_spec],
        out_specs=(tile,) + tuple(whole(s) for s in out_shapes[1:]),
        out_shape=out_shapes,
        scratch_shapes=scratch,
        compiler_params=_params(("arbitrary",)),
        name="mixer_prompt",
    )(h, *w_arrs, kb, vb)


def _mixer_sample(h, lw, y_attn, pool_st, conv_st, h_st):
    nt, n_seq = h.shape[0], h.shape[1]
    nb = _SAMPLE_SEQS
    w_arrs, w_specs = _mixer_weights(lw)
    seq3 = lambda a: pl.BlockSpec((a.shape[0], nb, a.shape[2]), lambda i: (0, i, 0))
    h_spec = pl.BlockSpec((nb, D_RNN), lambda i: (i, 0))
    outs = (h, pool_st, conv_st, h_st)
    return pl.pallas_call(
        functools.partial(_mixer_kernel, nb=nb, nt=nt, carried=False, pos0=PAST_LEN),
        grid=(n_seq // nb,),
        in_specs=[seq3(h)] + w_specs + [seq3(y_attn), seq3(pool_st), seq3(conv_st), h_spec],
        out_specs=(seq3(h), seq3(pool_st), seq3(conv_st), h_spec),
        out_shape=tuple(jax.ShapeDtypeStruct(a.shape, _F32) for a in outs),
        compiler_params=_params(("arbitrary",)),
        name="mixer_sample",
    )(h, *w_arrs, y_attn, pool_st, conv_st, h_st)


def _ffn_weights(lw, g_final):
    arrs = [lw['g_ffn'], lw['w_up'], lw['ffn_conv_w'], lw['ffn_conv_b'], lw['w_down']]
    if g_final is not None:
        arrs.append(g_final)
    return arrs, [_resident(a.shape) for a in arrs]


def _ffn_prompt(h, lw, n_seq, g_final):
    nt, nb = _PROMPT_STEPS, n_seq
    rows = nt * nb
    w_arrs, w_specs = _ffn_weights(lw, g_final)
    tile = pl.BlockSpec((rows, D_MODEL), lambda i: (i, 0))
    st_shape = ((FFN_CONV - 1) * nb, D_FF)
    scratch = [pltpu.VMEM((rows, D_FF), _BF16)]
    if g_final is None:
        out_tile, out_shape = tile, h.shape
    else:
        out_tile, out_shape = pl.BlockSpec((nb, nt, D_MODEL), lambda i: (0, i, 0)), (nb, h.shape[0] // nb, D_MODEL)
        scratch.append(pltpu.VMEM((D_MODEL // _LANES, rows, _LANES), _F32))
    return pl.pallas_call(
        functools.partial(_ffn_kernel, nb=nb, nt=nt, carried=True, final_norm=g_final is not None),
        grid=(h.shape[0] // rows,),
        in_specs=[tile] + w_specs,
        out_specs=(out_tile, pl.BlockSpec(st_shape, lambda i: (0, 0))),
        out_shape=(jax.ShapeDtypeStruct(out_shape, _F32), jax.ShapeDtypeStruct(st_shape, _F32)),
        scratch_shapes=scratch,
        compiler_params=_params(("arbitrary",)),
        name="ffn_prompt",
    )(h, *w_arrs)


def _ffn_sample(h, lw, st, g_final):
    nt, n_seq = h.shape[0], h.shape[1]
    nb = _SAMPLE_SEQS
    w_arrs, w_specs = _ffn_weights(lw, g_final)
    seq3 = lambda a: pl.BlockSpec((a.shape[0], nb, a.shape[2]), lambda i: (0, i, 0))
    return pl.pallas_call(
        functools.partial(_ffn_kernel, nb=nb, nt=nt, carried=False, final_norm=g_final is not None),
        grid=(n_seq // nb,),
        in_specs=[seq3(h)] + w_specs + [seq3(st)],
        out_specs=(seq3(h), seq3(st)),
        out_shape=(jax.ShapeDtypeStruct(h.shape, _F32), jax.ShapeDtypeStruct(st.shape, _F32)),
        scratch_shapes=[pltpu.VMEM((nt * nb, D_FF), _BF16)],
        compiler_params=_params(("arbitrary",)),
        name="ffn_sample",
    )(h, *w_arrs, st)


def _mem_kv(mem, g_mem, w_k, w_v):
    n_seq, n_mem = mem.shape[0], mem.shape[1]
    per_layer = lambda a: pl.BlockSpec((1,) + a.shape[1:], lambda l, b: (l, 0, 0))
    kv_spec = pl.BlockSpec((1, 1, n_mem, D_Q), lambda l, b: (l, b, 0, 0))
    kv_shape = (DEPTH, n_seq, n_mem, D_Q)
    return pl.pallas_call(
        _mem_kv_kernel,
        grid=(DEPTH, n_seq),
        in_specs=[pl.BlockSpec((1, n_mem, D_MODEL), lambda l, b: (b, 0, 0)),
                  per_layer(g_mem), per_layer(w_k), per_layer(w_v)],
        out_specs=(kv_spec,) * 4,
        out_shape=(jax.ShapeDtypeStruct(kv_shape, _F32),) * 2 + (jax.ShapeDtypeStruct(kv_shape, _BF16),) * 2,
        compiler_params=_params(("arbitrary", "arbitrary")),
        name="mem_kv",
    )(mem, g_mem, w_k, w_v)


def _sample_attn(h_nat, g_mix, w_in, k, v, layer, t_seq):
    n_seq = k.shape[1]
    nb = _ATTN_SEQS
    rows = nb * t_seq
    kv_spec = pl.BlockSpec((None, nb) + k.shape[2:], lambda i: (layer, i, 0, 0))
    q_cols = pl.BlockSpec((D_MODEL, D_Q), lambda i: (0, (D_POOL + D_RNN) // D_Q),
                          pipeline_mode=pl.Buffered(1))
    return pl.pallas_call(
        functools.partial(_sample_attn_kernel, n_seq=nb, t_seq=t_seq),
        grid=(n_seq // nb,),
        in_specs=[pl.BlockSpec((rows, D_MODEL), lambda i: (i, 0)), _resident(g_mix.shape),
                  q_cols, kv_spec, kv_spec],
        out_specs=pl.BlockSpec((rows, D_Q), lambda i: (i, 0)),
        out_shape=jax.ShapeDtypeStruct((n_seq * t_seq, D_Q), _F32),
        compiler_params=_params(("arbitrary",)),
        name="sample_attn",
    )(h_nat, g_mix, w_in, k, v)


def kernel(x_prompt, x_sample, mem_prompt, cache_mem_k, cache_mem_v, state_pool, state_rnn_conv, state_rnn_h, state_ffn_conv, g_mix, w_in, w_gate, b_gate, pool_w, pool_scale, rnn_conv_w, rnn_conv_b, rnn_wa, rnn_ba, rnn_wx, rnn_bx, rnn_lambda, g_mem, w_k, w_v, w_br_pool, w_br_rnn, w_br_attn, w_out, g_ffn, w_up, ffn_conv_w, ffn_conv_b, w_down, g_final):
    bp, tp, _ = x_prompt.shape
    bs, ts, _ = x_sample.shape
    n_mem = mem_prompt.shape[1]
    bf = lambda a: a.astype(_BF16)
    row = lambda a: a[:, None, :]
    layers = []
    for l in range(DEPTH):
        layers.append({
            'g_mix': row(g_mix)[l], 'w_in': bf(w_in[l]), 'w_gate': bf(w_gate[l]), 'b_gate': row(b_gate)[l],
            'pool_w': bf(pool_w[l]), 'pool_scale': row(pool_scale)[l],
            'rnn_conv_w': rnn_conv_w[l], 'rnn_conv_b': row(rnn_conv_b)[l],
            'rnn_wax': bf(jnp.concatenate([rnn_wa[l], rnn_wx[l]], axis=-1)),
            'rnn_ba': row(rnn_ba)[l], 'rnn_bx': row(rnn_bx)[l], 'rnn_lambda': row(rnn_lambda)[l],
            'w_br_pool': bf(w_br_pool[l]), 'w_br_rnn': bf(w_br_rnn[l]), 'w_br_attn': bf(w_br_attn[l]),
            'w_out': bf(w_out[l]),
            'g_ffn': row(g_ffn)[l], 'w_up': bf(w_up[l]), 'ffn_conv_w': ffn_conv_w[l],
            'ffn_conv_b': row(ffn_conv_b)[l], 'w_down': bf(w_down[l]),
        })
    g_fin = g_final[None, :]

    p_mk, p_mv, kb, vb = _mem_kv(mem_prompt, row(g_mem), bf(w_k), bf(w_v))
    cache_k = cache_mem_k.reshape(DEPTH, bs, n_mem * N_HEADS, HEAD_DIM)
    cache_v = cache_mem_v.reshape(DEPTH, bs, n_mem * N_HEADS, HEAD_DIM)

    time_major = lambda a: jnp.swapaxes(a, 0, 1)
    hp = x_prompt
    hs = time_major(x_sample)
    p_pool, p_rconv, p_rh, p_fconv = [], [], [], []
    s_pool, s_rconv, s_rh, s_fconv = [], [], [], []
    for l, lw in enumerate(layers):
        last = l == DEPTH - 1
        hp, pool_st, conv_st, h_st = _mixer_prompt(hp, lw, kb, vb, l, bp)
        hp, ffn_st = _ffn_prompt(hp, lw, bp, g_fin if last else None)
        p_pool.append(time_major(pool_st.reshape(POOL_BUF + 1, bp, D_POOL)[1:]))
        p_rconv.append(time_major(conv_st.reshape(RNN_CONV - 1, bp, D_RNN)))
        p_rh.append(h_st)
        p_fconv.append(time_major(ffn_st.reshape(FFN_CONV - 1, bp, D_FF)))

        hs_nat = time_major(hs).reshape(bs * ts, D_MODEL)
        o_nat = _sample_attn(hs_nat, lw['g_mix'], lw['w_in'], cache_k, cache_v, l, ts)
        y_attn = time_major(o_nat.reshape(bs, ts, D_Q))
        hs, pool_o, conv_o, h_o = _mixer_sample(
            hs, lw, y_attn, time_major(state_pool[l]), time_major(state_rnn_conv[l]), state_rnn_h[l])
        hs, ffn_o = _ffn_sample(hs, lw, time_major(state_ffn_conv[l]), g_fin if last else None)
        s_pool.append(time_major(pool_o))
        s_rconv.append(time_major(conv_o))
        s_rh.append(h_o)
        s_fconv.append(time_major(ffn_o))

    head = (N_HEADS, HEAD_DIM)
    return (hp, time_major(hs),
            jnp.stack(p_pool), jnp.stack(p_rconv), jnp.stack(p_rh), jnp.stack(p_fconv),
            p_mk.reshape(DEPTH, bp, n_mem, *head), p_mv.reshape(DEPTH, bp, n_mem, *head),
            jnp.stack(s_pool), jnp.stack(s_rconv), jnp.stack(s_rh), jnp.stack(s_fconv))
```

```python
import functools

import jax
import jax.numpy as jnp
from jax import lax
from jax.experimental import pallas as pl
from jax.experimental.pallas import tpu as pltpu

D_MODEL = 1024
DEPTH = 2
PAST_LEN = 16384
POOL_WINDOWS = (2, 4, 8, 16)
D_POOL = D_MODEL // 2
POOL_GROUP = D_POOL // len(POOL_WINDOWS)
POOL_BUF = max(POOL_WINDOWS) - 1
D_RNN = D_MODEL
N_RNN_BLOCKS = 8
RNN_BLOCK = D_RNN // N_RNN_BLOCKS
RNN_CONV = 4
RG_C = 8.0
N_HEADS = 4
HEAD_DIM = 128
D_Q = N_HEADS * HEAD_DIM
N_BRANCH = 3
D_FF = 3 * D_MODEL
FFN_CONV = 3
EPS = 1e-6

_V7X_VMEM_BYTES = 64 * 1024 * 1024
_VMEM_LIMIT_BYTES = _V7X_VMEM_BYTES - 8 * 1024 * 1024

_PROMPT_STEPS = 64
_SAMPLE_SEQS = 32
_ATTN_SEQS = 8
_FF_CHUNK = 512

_MXU_COLS = 256

_F32 = jnp.float32
_BF16 = jnp.bfloat16


def _dot(a, b):
    return jnp.dot(a, b, preferred_element_type=_F32)


def _dot_nt(a, b):
    return lax.dot_general(a, b, (((1,), (1,)), ((), ())), preferred_element_type=_F32)


def _rmsnorm(x, g):
    return x * lax.rsqrt(jnp.mean(x * x, axis=-1, keepdims=True) + EPS) * g


def _softplus(x):
    return jnp.maximum(x, 0.0) + jnp.log1p(jnp.exp(-jnp.abs(x)))


def _softmax_rows(s):
    e = jnp.exp(s - jnp.max(s, axis=-1, keepdims=True))
    return e / jnp.sum(e, axis=-1, keepdims=True)


def _head(h):
    return slice(h * HEAD_DIM, (h + 1) * HEAD_DIM)


def _pool_window_sums(ext, rows, nb):
    sums = []
    for g, w in enumerate(POOL_WINDOWS):
        s = ext[:, g * POOL_GROUP:(g + 1) * POOL_GROUP]
        off, k = 0, 1
        while k < w:
            s = s[k * nb:] + s[:-k * nb]
            off += k * nb
            k *= 2
        start = (POOL_BUF + 1) * nb - off
        sums.append(s[start:start + rows])
    return sums


_LANES = 128


def _load_time_major(x_ref, scr, nb, nt):
    for b in range(nb):
        for c in range(scr.shape[0]):
            scr[c, pl.ds(b, nt, stride=nb), :] = x_ref[b, :, c * _LANES:(c + 1) * _LANES]
    return jnp.concatenate([scr[c] for c in range(scr.shape[0])], axis=1)


def _store_from_time_major(y, y_ref, scr, nb, nt):
    for c in range(scr.shape[0]):
        scr[c] = y[:, c * _LANES:(c + 1) * _LANES]
    for b in range(nb):
        for c in range(scr.shape[0]):
            y_ref[b, :, c * _LANES:(c + 1) * _LANES] = scr[c, pl.ds(b, nt, stride=nb), :]


def _mixer_kernel(*refs, nb, nt, carried, pos0, natural_in=False):
    rows = nb * nt
    (x_ref, g_mix, w_in, w_gate, b_gate, pool_w, pool_scale, conv_w, conv_b, w_ax,
     b_a, b_x, lam, w_brp, w_brr, w_bra, w_out) = refs[:17]
    if carried:
        k_ref, v_ref, out_ref, pool_st, conv_st, h_st, qo_scr = refs[17:24]

        @pl.when(pl.program_id(0) == 0)
        def _():
            pool_st[...] = jnp.zeros_like(pool_st)
            conv_st[...] = jnp.zeros_like(conv_st)
            h_st[...] = jnp.zeros_like(h_st)

        x = _load_time_major(x_ref, refs[24], nb, nt) if natural_in else x_ref[...]
        pool_hist = pool_st[...]
        conv_hist = conv_st[...]
        h0 = h_st[...]
    else:
        (yattn_ref, pool_in, conv_in, h_in, out_ref, pool_out, conv_out, h_out) = refs[17:]
        x = x_ref[...].reshape(rows, D_MODEL)
        pool_hist = jnp.concatenate(
            [jnp.zeros((nb, D_POOL), _F32), pool_in[...].reshape(POOL_BUF * nb, D_POOL)], axis=0)
        conv_hist = conv_in[...].reshape((RNN_CONV - 1) * nb, D_RNN)
        h0 = h_in[...]

    xn = _rmsnorm(x, g_mix[...]).astype(_BF16)

    n_gate_chunks = N_BRANCH * D_MODEL // _MXU_COLS
    gate = [None] * n_gate_chunks

    def gate_chunk(j):
        sl = slice(j * _MXU_COLS, (j + 1) * _MXU_COLS)
        return jax.nn.sigmoid(_dot(xn, w_gate[:, sl]) + b_gate[:, sl])

    u_rnn = _dot(xn, w_in[:, D_POOL:D_POOL + D_RNN])
    u_pool = _dot(xn, w_in[:, 0:D_POOL])
    if carried:
        q = _dot(xn, w_in[:, D_POOL + D_RNN:])

    ext_r = jnp.concatenate([conv_hist, u_rnn], axis=0)
    xc = conv_b[...]
    for j in range(RNN_CONV):
        xc = xc + ext_r[j * nb:j * nb + rows] * conv_w[j:j + 1, :]
    new_conv_hist = ext_r[rows:]
    xcb = xc.astype(_BF16)
    log_a_unit = -RG_C * _softplus(-lam[...])
    cols, h_last = [], []
    for n in range(N_RNN_BLOCKS):
        sl = slice(n * RNN_BLOCK, (n + 1) * RNN_BLOCK)
        ri = _dot(xcb[:, sl], w_ax[n])
        r = jax.nn.sigmoid(ri[:, :RNN_BLOCK] + b_a[:, sl])
        i = jax.nn.sigmoid(ri[:, RNN_BLOCK:] + b_x[:, sl])
        log_a = r * log_a_unit[:, sl]
        a = jnp.exp(log_a)
        th = jnp.tanh(log_a)
        b = jnp.sqrt(-2.0 * th / (1.0 - th)) * (i * xc[:, sl])
        h = h0[:, sl]
        hs = []
        for t in range(nt):
            h = a[t * nb:(t + 1) * nb] * h + b[t * nb:(t + 1) * nb]
            hs.append(h)
        cols.append(jnp.concatenate(hs, axis=0))
        h_last.append(h)
        for j in range(n * n_gate_chunks // N_RNN_BLOCKS, (n + 1) * n_gate_chunks // N_RNN_BLOCKS):
            gate[j] = gate_chunk(j)
    y_rnn = jnp.concatenate(cols, axis=1).astype(_BF16)
    new_h = jnp.concatenate(h_last, axis=1)

    if carried:
        for h in range(N_HEADS):
            qo_scr[0, h] = q[:, _head(h)]
        s = jnp.concatenate(
            [_dot_nt(qo_scr[0, h, pl.ds(b, nt, stride=nb), :].astype(_BF16), k_ref[b, :, _head(h)])
             for b in range(nb) for h in range(N_HEADS)], axis=0)
    rnn_branch = _dot(y_rnn, w_brr[...])
    if carried:
        p = _softmax_rows(s * (HEAD_DIM ** -0.5))
        for b in range(nb):
            for h in range(N_HEADS):
                n = b * N_HEADS + h
                qo_scr[1, h, pl.ds(b, nt, stride=nb), :] = _dot(
                    p[n * nt:(n + 1) * nt].astype(_BF16), v_ref[b, :, _head(h)])

    ext = jnp.concatenate([pool_hist, u_pool], axis=0)
    wins = _pool_window_sums(ext, rows, nb)
    if pos0 < POOL_BUF:
        row = lax.broadcasted_iota(jnp.int32, (rows, POOL_GROUP), 0)
        pos1 = pos0 + pl.program_id(0) * nt + (row >> (nb.bit_length() - 1)) + 1
    ys = []
    for g, w in enumerate(POOL_WINDOWS):
        if pos0 < POOL_BUF:
            mean = wins[g] / jnp.minimum(pos1, w).astype(_F32)
        else:
            mean = wins[g] * (1.0 / w)
        d = mean - u_pool[:, g * POOL_GROUP:(g + 1) * POOL_GROUP]
        ys.append(_dot(d.astype(_BF16), pool_w[g]))
    y_pool = (jnp.concatenate(ys, axis=1) * pool_scale[...]).astype(_BF16)
    new_pool_hist = ext[rows:]
    pool_branch = _dot(y_pool, w_brp[...])

    if carried:
        y_attn = jnp.concatenate([qo_scr[1, h] for h in range(N_HEADS)], axis=1).astype(_BF16)
    else:
        y_attn = yattn_ref[...].reshape(rows, D_Q).astype(_BF16)

    out = x
    chunks_per_branch = D_MODEL // _MXU_COLS
    for c in range(chunks_per_branch):
        sl = slice(c * _MXU_COLS, (c + 1) * _MXU_COLS)
        branches = (pool_branch[:, sl], rnn_branch[:, sl], _dot(y_attn, w_bra[:, sl]))
        merged = None
        for k, branch in enumerate(branches):
            term = gate[k * chunks_per_branch + c] * branch
            merged = term if merged is None else merged + term
        out = out + _dot(merged.astype(_BF16), w_out[sl, :])
    if carried:
        out_ref[...] = out
        pool_st[...] = new_pool_hist
        conv_st[...] = new_conv_hist
        h_st[...] = new_h
    else:
        out_ref[...] = out.reshape(nt, nb, D_MODEL)
        pool_out[...] = new_pool_hist[nb:].reshape(POOL_BUF, nb, D_POOL)
        conv_out[...] = new_conv_hist.reshape(RNN_CONV - 1, nb, D_RNN)
        h_out[...] = new_h


def _ffn_kernel(*refs, nb, nt, carried, final_norm):
    rows = nb * nt
    x_ref, g_ffn, w_up, conv_w, conv_b, w_down = refs[:6]
    rest = list(refs[6:])
    g_final = rest.pop(0) if final_norm else None
    st_in = None if carried else rest.pop(0)
    out_ref, st_out, act_scr = rest[:3]

    if carried:
        @pl.when(pl.program_id(0) == 0)
        def _():
            st_out[...] = jnp.zeros_like(st_out)

        x = x_ref[...]
    else:
        x = x_ref[...].reshape(rows, D_MODEL)

    xn = _rmsnorm(x, g_ffn[...]).astype(_BF16)
    hist_rows = (FFN_CONV - 1) * nb
    for c in range(0, D_FF, _FF_CHUNK):
        sl = slice(c, c + _FF_CHUNK)
        g_pre = _dot(xn, w_up[:, sl])
        val = _dot(xn, w_up[:, D_FF + c:D_FF + c + _FF_CHUNK])
        if carried:
            hist = st_out[:, sl]
        else:
            hist = st_in[:, :, sl].reshape(hist_rows, _FF_CHUNK)
        ext = jnp.concatenate([hist, g_pre], axis=0)
        g_conv = conv_b[:, sl]
        for j in range(FFN_CONV):
            g_conv = g_conv + ext[j * nb:j * nb + rows] * conv_w[j:j + 1, sl]
        if carried:
            st_out[:, sl] = ext[rows:]
        else:
            st_out[:, :, sl] = ext[rows:].reshape(FFN_CONV - 1, nb, _FF_CHUNK)
        act_scr[:, sl] = (jax.nn.gelu(g_conv) * val).astype(_BF16)
    out = x + _dot(act_scr[...], w_down[...])
    if final_norm:
        out = _rmsnorm(out, g_final[...])
    if not carried:
        out_ref[...] = out.reshape(nt, nb, D_MODEL)
    elif final_norm:
        _store_from_time_major(out, out_ref, rest[3], nb, nt)
    else:
        out_ref[...] = out


def _mem_kv_kernel(mem_ref, g_ref, wk_ref, wv_ref, k_ref, v_ref, kb_ref, vb_ref):
    n_mem = mem_ref.shape[1]
    for b in range(mem_ref.shape[0]):
        mn = _rmsnorm(mem_ref[b], g_ref[...]).astype(_BF16)
        for w_ref, out_ref, out_bf_ref in ((wk_ref, k_ref, kb_ref), (wv_ref, v_ref, vb_ref)):
            kv = _dot(mn, w_ref[...])
            out_bf_ref[b] = kv.astype(_BF16)
            for h in range(N_HEADS):
                out_ref[b, pl.ds(h, n_mem, stride=N_HEADS), :] = kv[:, _head(h)]


def _sample_attn_kernel(x_ref, g_ref, wq_ref, k_ref, v_ref, o_ref, *, n_seq, t_seq):
    xn = _rmsnorm(x_ref[...], g_ref[...]).astype(_BF16)
    q = _dot(xn, wq_ref[...])
    rows = lambda b: slice(b * t_seq, (b + 1) * t_seq)
    hq = N_HEADS * t_seq
    s = jnp.concatenate(
        [_dot_nt(jnp.concatenate([q[rows(b), _head(h)] for h in range(N_HEADS)], axis=0).astype(_BF16),
                 k_ref[b].astype(_BF16)) for b in range(n_seq)], axis=0)
    r = lax.broadcasted_iota(jnp.int32, s.shape, 0)
    c = lax.broadcasted_iota(jnp.int32, s.shape, 1)
    same_head = ((r >> (t_seq.bit_length() - 1)) & (N_HEADS - 1)) == (c & (N_HEADS - 1))
    p = _softmax_rows(jnp.where(same_head, s * (HEAD_DIM ** -0.5), -jnp.inf))
    for b in range(n_seq):
        o = _dot(p[b * hq:(b + 1) * hq].astype(_BF16), v_ref[b].astype(_BF16))
        o_ref[rows(b), :] = jnp.concatenate([o[rows(h)] for h in range(N_HEADS)], axis=1)


def _resident(shape):
    return pl.BlockSpec(shape, lambda *_: (0,) * len(shape), pipeline_mode=pl.Buffered(1))


def _params(semantics):
    return pltpu.CompilerParams(dimension_semantics=semantics, vmem_limit_bytes=_VMEM_LIMIT_BYTES)


def _layer_resident(a, layer):
    return pl.BlockSpec((None,) + a.shape[1:], lambda *_: (layer,) + (0,) * (a.ndim - 1),
                        pipeline_mode=pl.Buffered(1))


def _mixer_weights(pw, layer):
    names = ('g_mix', 'w_in', 'w_gate', 'b_gate', 'pool_w', 'pool_scale', 'rnn_conv_w', 'rnn_conv_b',
             'rnn_wax', 'rnn_ba', 'rnn_bx', 'rnn_lambda', 'w_br_pool', 'w_br_rnn', 'w_br_attn', 'w_out')
    arrs = [pw[n] for n in names]
    return arrs, [_layer_resident(a, layer) for a in arrs]


def _mixer_prompt(h, pw, kb, vb, layer, n_seq):
    nt, nb = _PROMPT_STEPS, n_seq
    rows = nt * nb
    natural_in = h.ndim == 3
    n_tok = h.shape[0] * h.shape[1] if natural_in else h.shape[0]
    w_arrs, w_specs = _mixer_weights(pw, layer)
    kv_spec = pl.BlockSpec((None,) + kb.shape[1:], lambda i: (layer, 0, 0, 0), pipeline_mode=pl.Buffered(1))
    tile = pl.BlockSpec((rows, D_MODEL), lambda i: (i, 0))
    in_tile = pl.BlockSpec((nb, nt, D_MODEL), lambda i: (0, i, 0)) if natural_in else tile
    scratch = [pltpu.VMEM((2, N_HEADS, rows, HEAD_DIM), _F32)]
    if natural_in:
        scratch.append(pltpu.VMEM((D_MODEL // _LANES, rows, _LANES), _F32))
    out_shapes = (jax.ShapeDtypeStruct((n_tok, D_MODEL), _F32),
                  jax.ShapeDtypeStruct(((POOL_BUF + 1) * nb, D_POOL), _F32),
                  jax.ShapeDtypeStruct(((RNN_CONV - 1) * nb, D_RNN), _F32),
                  jax.ShapeDtypeStruct((nb, D_RNN), _F32))
    whole = lambda s: pl.BlockSpec(s.shape, lambda i: (0,) * len(s.shape))
    return pl.pallas_call(
        functools.partial(_mixer_kernel, nb=nb, nt=nt, carried=True, pos0=0, natural_in=natural_in),
        grid=(n_tok // rows,),
        in_specs=[in_tile] + w_specs + [kv_spec, kv_spec],
        out_specs=(tile,) + tuple(whole(s) for s in out_shapes[1:]),
        out_shape=out_shapes,
        scratch_shapes=scratch,
        compiler_params=_params(("arbitrary",)),
        name="mixer_prompt",
    )(h, *w_arrs, kb, vb)


def _mixer_sample(h, pw, layer, y_attn, pool_st, conv_st, h_st):
    nt, n_seq = h.shape[0], h.shape[1]
    nb = _SAMPLE_SEQS
    w_arrs, w_specs = _mixer_weights(pw, layer)
    seq3 = lambda a: pl.BlockSpec((a.shape[0], nb, a.shape[2]), lambda i: (0, i, 0))
    h_spec = pl.BlockSpec((nb, D_RNN), lambda i: (i, 0))
    outs = (h, pool_st, conv_st, h_st)
    return pl.pallas_call(
        functools.partial(_mixer_kernel, nb=nb, nt=nt, carried=False, pos0=PAST_LEN),
        grid=(n_seq // nb,),
        in_specs=[seq3(h)] + w_specs + [seq3(y_attn), seq3(pool_st), seq3(conv_st), h_spec],
        out_specs=(seq3(h), seq3(pool_st), seq3(conv_st), h_spec),
        out_shape=tuple(jax.ShapeDtypeStruct(a.shape, _F32) for a in outs),
        compiler_params=_params(("arbitrary",)),
        name="mixer_sample",
    )(h, *w_arrs, y_attn, pool_st, conv_st, h_st)


def _ffn_weights(pw, layer, g_final):
    arrs = [pw[n] for n in ('g_ffn', 'w_up', 'ffn_conv_w', 'ffn_conv_b', 'w_down')]
    specs = [_layer_resident(a, layer) for a in arrs]
    if g_final is not None:
        arrs.append(g_final)
        specs.append(_resident(g_final.shape))
    return arrs, specs


def _ffn_prompt(h, pw, layer, n_seq, g_final):
    nt, nb = _PROMPT_STEPS, n_seq
    rows = nt * nb
    w_arrs, w_specs = _ffn_weights(pw, layer, g_final)
    tile = pl.BlockSpec((rows, D_MODEL), lambda i: (i, 0))
    st_shape = ((FFN_CONV - 1) * nb, D_FF)
    scratch = [pltpu.VMEM((rows, D_FF), _BF16)]
    if g_final is None:
        out_tile, out_shape = tile, h.shape
    else:
        out_tile, out_shape = pl.BlockSpec((nb, nt, D_MODEL), lambda i: (0, i, 0)), (nb, h.shape[0] // nb, D_MODEL)
        scratch.append(pltpu.VMEM((D_MODEL // _LANES, rows, _LANES), _F32))
    return pl.pallas_call(
        functools.partial(_ffn_kernel, nb=nb, nt=nt, carried=True, final_norm=g_final is not None),
        grid=(h.shape[0] // rows,),
        in_specs=[tile] + w_specs,
        out_specs=(out_tile, pl.BlockSpec(st_shape, lambda i: (0, 0))),
        out_shape=(jax.ShapeDtypeStruct(out_shape, _F32), jax.ShapeDtypeStruct(st_shape, _F32)),
        scratch_shapes=scratch,
        compiler_params=_params(("arbitrary",)),
        name="ffn_prompt",
    )(h, *w_arrs)


def _ffn_sample(h, pw, layer, st, g_final):
    nt, n_seq = h.shape[0], h.shape[1]
    nb = _SAMPLE_SEQS
    w_arrs, w_specs = _ffn_weights(pw, layer, g_final)
    seq3 = lambda a: pl.BlockSpec((a.shape[0], nb, a.shape[2]), lambda i: (0, i, 0))
    return pl.pallas_call(
        functools.partial(_ffn_kernel, nb=nb, nt=nt, carried=False, final_norm=g_final is not None),
        grid=(n_seq // nb,),
        in_specs=[seq3(h)] + w_specs + [seq3(st)],
        out_specs=(seq3(h), seq3(st)),
        out_shape=(jax.ShapeDtypeStruct(h.shape, _F32), jax.ShapeDtypeStruct(st.shape, _F32)),
        scratch_shapes=[pltpu.VMEM((nt * nb, D_FF), _BF16)],
        compiler_params=_params(("arbitrary",)),
        name="ffn_sample",
    )(h, *w_arrs, st)


def _mem_kv(mem, g_mem, w_k, w_v):
    n_seq, n_mem = mem.shape[0], mem.shape[1]
    per_layer = lambda a: pl.BlockSpec((None,) + a.shape[1:], lambda l: (l,) + (0,) * (a.ndim - 1))
    out_shapes = ((DEPTH, n_seq, n_mem * N_HEADS, HEAD_DIM),) * 2 + ((DEPTH, n_seq, n_mem, D_Q),) * 2
    dtypes = (_F32, _F32, _BF16, _BF16)
    return pl.pallas_call(
        _mem_kv_kernel,
        grid=(DEPTH,),
        in_specs=[pl.BlockSpec(mem.shape, lambda l: (0, 0, 0)), per_layer(g_mem), per_layer(w_k), per_layer(w_v)],
        out_specs=tuple(pl.BlockSpec((None,) + s[1:], lambda l: (l, 0, 0, 0)) for s in out_shapes),
        out_shape=tuple(jax.ShapeDtypeStruct(s, d) for s, d in zip(out_shapes, dtypes)),
        compiler_params=_params(("arbitrary",)),
        name="mem_kv",
    )(mem, g_mem, w_k, w_v)


def _sample_attn(h_nat, g_mix, w_in, k, v, layer, t_seq):
    n_seq = k.shape[1]
    nb = _ATTN_SEQS
    rows = nb * t_seq
    kv_spec = pl.BlockSpec((None, nb) + k.shape[2:], lambda i: (layer, i, 0, 0))
    q_cols = pl.BlockSpec((None, D_MODEL, D_Q), lambda i: (layer, 0, (D_POOL + D_RNN) // D_Q),
                          pipeline_mode=pl.Buffered(1))
    return pl.pallas_call(
        functools.partial(_sample_attn_kernel, n_seq=nb, t_seq=t_seq),
        grid=(n_seq // nb,),
        in_specs=[pl.BlockSpec((rows, D_MODEL), lambda i: (i, 0)), _layer_resident(g_mix, layer),
                  q_cols, kv_spec, kv_spec],
        out_specs=pl.BlockSpec((rows, D_Q), lambda i: (i, 0)),
        out_shape=jax.ShapeDtypeStruct((n_seq * t_seq, D_Q), _F32),
        compiler_params=_params(("arbitrary",)),
        name="sample_attn",
    )(h_nat, g_mix, w_in, k, v)


def kernel(x_prompt, x_sample, mem_prompt, cache_mem_k, cache_mem_v, state_pool, state_rnn_conv, state_rnn_h, state_ffn_conv, g_mix, w_in, w_gate, b_gate, pool_w, pool_scale, rnn_conv_w, rnn_conv_b, rnn_wa, rnn_ba, rnn_wx, rnn_bx, rnn_lambda, g_mem, w_k, w_v, w_br_pool, w_br_rnn, w_br_attn, w_out, g_ffn, w_up, ffn_conv_w, ffn_conv_b, w_down, g_final):
    bp, tp, _ = x_prompt.shape
    bs, ts, _ = x_sample.shape
    n_mem = mem_prompt.shape[1]
    bf = lambda a: a.astype(_BF16)
    row = lambda a: a[:, None, :]
    pw = {
        'g_mix': row(g_mix), 'w_in': bf(w_in), 'w_gate': bf(w_gate), 'b_gate': row(b_gate),
        'pool_w': bf(pool_w), 'pool_scale': row(pool_scale),
        'rnn_conv_w': rnn_conv_w, 'rnn_conv_b': row(rnn_conv_b),
        'rnn_wax': bf(jnp.concatenate([rnn_wa, rnn_wx], axis=-1)),
        'rnn_ba': row(rnn_ba), 'rnn_bx': row(rnn_bx), 'rnn_lambda': row(rnn_lambda),
        'w_br_pool': bf(w_br_pool), 'w_br_rnn': bf(w_br_rnn), 'w_br_attn': bf(w_br_attn), 'w_out': bf(w_out),
        'g_ffn': row(g_ffn), 'w_up': bf(w_up), 'ffn_conv_w': ffn_conv_w, 'ffn_conv_b': row(ffn_conv_b),
        'w_down': bf(w_down),
    }
    g_fin = g_final[None, :]

    p_mk, p_mv, kb, vb = _mem_kv(mem_prompt, row(g_mem), bf(w_k), bf(w_v))
    cache_k = cache_mem_k.reshape(DEPTH, bs, n_mem * N_HEADS, HEAD_DIM)
    cache_v = cache_mem_v.reshape(DEPTH, bs, n_mem * N_HEADS, HEAD_DIM)

    time_major = lambda a: jnp.swapaxes(a, 0, 1)
    hp = x_prompt
    hs = time_major(x_sample)
    p_pool, p_rconv, p_rh, p_fconv = [], [], [], []
    s_pool, s_rconv, s_rh, s_fconv = [], [], [], []
    for l in range(DEPTH):
        last = l == DEPTH - 1
        hp, pool_st, conv_st, h_st = _mixer_prompt(hp, pw, kb, vb, l, bp)
        hp, ffn_st = _ffn_prompt(hp, pw, l, bp, g_fin if last else None)
        p_pool.append(time_major(pool_st.reshape(POOL_BUF + 1, bp, D_POOL)[1:]))
        p_rconv.append(time_major(conv_st.reshape(RNN_CONV - 1, bp, D_RNN)))
        p_rh.append(h_st)
        p_fconv.append(time_major(ffn_st.reshape(FFN_CONV - 1, bp, D_FF)))

        hs_nat = time_major(hs).reshape(bs * ts, D_MODEL)
        o_nat = _sample_attn(hs_nat, pw['g_mix'], pw['w_in'], cache_k, cache_v, l, ts)
        y_attn = time_major(o_nat.reshape(bs, ts, D_Q))
        hs, pool_o, conv_o, h_o = _mixer_sample(
            hs, pw, l, y_attn, time_major(state_pool[l]), time_major(state_rnn_conv[l]), state_rnn_h[l])
        hs, ffn_o = _ffn_sample(hs, pw, l, time_major(state_ffn_conv[l]), g_fin if last else None)
        s_pool.append(time_major(pool_o))
        s_rconv.append(time_major(conv_o))
        s_rh.append(h_o)
        s_fconv.append(time_major(ffn_o))

    head = (N_HEADS, HEAD_DIM)
    return (hp, time_major(hs),
            jnp.stack(p_pool), jnp.stack(p_rconv), jnp.stack(p_rh), jnp.stack(p_fconv),
            p_mk.reshape(DEPTH, bp, n_mem, *head), p_mv.reshape(DEPTH, bp, n_mem, *head),
            jnp.stack(s_pool), jnp.stack(s_rconv), jnp.stack(s_rh), jnp.stack(s_fconv))
```

```python
import functools

import jax
import jax.numpy as jnp
from jax import lax
from jax.experimental import pallas as pl
from jax.experimental.pallas import tpu as pltpu

D_MODEL = 1024
DEPTH = 2
PAST_LEN = 16384
POOL_WINDOWS = (2, 4, 8, 16)
D_POOL = D_MODEL // 2
POOL_GROUP = D_POOL // len(POOL_WINDOWS)
POOL_BUF = max(POOL_WINDOWS) - 1
D_RNN = D_MODEL
N_RNN_BLOCKS = 8
RNN_BLOCK = D_RNN // N_RNN_BLOCKS
RNN_CONV = 4
RG_C = 8.0
N_HEADS = 4
HEAD_DIM = 128
D_Q = N_HEADS * HEAD_DIM
N_BRANCH = 3
D_FF = 3 * D_MODEL
FFN_CONV = 3
EPS = 1e-6

_V7X_VMEM_BYTES = 64 * 1024 * 1024
_VMEM_LIMIT_BYTES = _V7X_VMEM_BYTES - 8 * 1024 * 1024

_PROMPT_STEPS = 64
_SAMPLE_SEQS = 32
_ATTN_SEQS = 8
_FF_CHUNK = 512

_MXU_COLS = 256
_LANES = 128

_F32 = jnp.float32
_BF16 = jnp.bfloat16


def _dot(a, b):
    return jnp.dot(a, b, preferred_element_type=_F32)


def _dot_nt(a, b):
    return lax.dot_general(a, b, (((1,), (1,)), ((), ())), preferred_element_type=_F32)


def _rmsnorm(x, g):
    return x * lax.rsqrt(jnp.mean(x * x, axis=-1, keepdims=True) + EPS) * g


def _sigmoid(x):
    return 0.5 * jnp.tanh(0.5 * x) + 0.5


def _softplus(x):
    return jnp.maximum(x, 0.0) + jnp.log1p(jnp.exp(-jnp.abs(x)))


def _softmax_rows(s):
    e = jnp.exp(s - jnp.max(s, axis=-1, keepdims=True))
    return e / jnp.sum(e, axis=-1, keepdims=True)


def _head(h):
    return slice(h * HEAD_DIM, (h + 1) * HEAD_DIM)


def _pool_window_sums(ext, rows, nb):
    sums = []
    for g, w in enumerate(POOL_WINDOWS):
        s = ext[:, g * POOL_GROUP:(g + 1) * POOL_GROUP]
        off, k = 0, 1
        while k < w:
            s = s[k * nb:] + s[:-k * nb]
            off += k * nb
            k *= 2
        start = (POOL_BUF + 1) * nb - off
        sums.append(s[start:start + rows])
    return sums


def _load_time_major(x_ref, scr, nb, nt):
    for b in range(nb):
        for c in range(scr.shape[0]):
            scr[c, pl.ds(b, nt, stride=nb), :] = x_ref[b, :, c * _LANES:(c + 1) * _LANES]
    return jnp.concatenate([scr[c] for c in range(scr.shape[0])], axis=1)


def _store_from_time_major(y, y_ref, scr, nb, nt):
    for c in range(scr.shape[0]):
        scr[c] = y[:, c * _LANES:(c + 1) * _LANES]
    for b in range(nb):
        for c in range(scr.shape[0]):
            y_ref[b, :, c * _LANES:(c + 1) * _LANES] = scr[c, pl.ds(b, nt, stride=nb), :]


def _mixer_tile(x, xn, pool_hist, conv_hist, h0, weights, nb, nt, first_pos, attention, y_attn):
    rows = nb * nt
    (w_in, w_gate, b_gate, pool_w, pool_scale, conv_w, conv_b, w_ax,
     b_a, b_x, lam, w_brp, w_brr, w_bra, w_out) = weights

    n_gate_chunks = N_BRANCH * D_MODEL // _MXU_COLS
    gate = [None] * n_gate_chunks

    def gate_chunk(j):
        sl = slice(j * _MXU_COLS, (j + 1) * _MXU_COLS)
        return _sigmoid(_dot(xn, w_gate[:, sl]) + b_gate[:, sl])

    u_rnn = _dot(xn, w_in[:, D_POOL:D_POOL + D_RNN])
    u_pool = _dot(xn, w_in[:, 0:D_POOL])
    if attention is not None:
        q = _dot(xn, w_in[:, D_POOL + D_RNN:])

    ext_r = jnp.concatenate([conv_hist, u_rnn], axis=0)
    xc = conv_b[...]
    for j in range(RNN_CONV):
        xc = xc + ext_r[j * nb:j * nb + rows] * conv_w[j:j + 1, :]
    new_conv_hist = ext_r[rows:]
    xcb = xc.astype(_BF16)
    log_a_unit = -RG_C * _softplus(-lam[...])
    cols, h_last = [], []
    for n in range(N_RNN_BLOCKS):
        sl = slice(n * RNN_BLOCK, (n + 1) * RNN_BLOCK)
        ri = _dot(xcb[:, sl], w_ax[n])
        r = _sigmoid(ri[:, :RNN_BLOCK] + b_a[:, sl])
        i = _sigmoid(ri[:, RNN_BLOCK:] + b_x[:, sl])
        log_a = r * log_a_unit[:, sl]
        a = jnp.exp(log_a)
        th = jnp.tanh(log_a)
        b = jnp.sqrt(-2.0 * th / (1.0 - th)) * (i * xc[:, sl])
        h = h0[:, sl]
        hs = []
        for t in range(nt):
            h = a[t * nb:(t + 1) * nb] * h + b[t * nb:(t + 1) * nb]
            hs.append(h)
        cols.append(jnp.concatenate(hs, axis=0))
        h_last.append(h)
        for j in range(n * n_gate_chunks // N_RNN_BLOCKS, (n + 1) * n_gate_chunks // N_RNN_BLOCKS):
            gate[j] = gate_chunk(j)
    y_rnn = jnp.concatenate(cols, axis=1).astype(_BF16)
    new_h = jnp.concatenate(h_last, axis=1)

    if attention is not None:
        k_ref, v_ref, qo_scr = attention
        for h in range(N_HEADS):
            qo_scr[0, h] = q[:, _head(h)]
        s = jnp.concatenate(
            [_dot(qo_scr[0, h, pl.ds(b, nt, stride=nb), :].astype(_BF16), k_ref[b, h])
             for b in range(nb) for h in range(N_HEADS)], axis=0)
    rnn_branch = _dot(y_rnn, w_brr[...])
    if attention is not None:
        p = _softmax_rows(s * (HEAD_DIM ** -0.5))
        for b in range(nb):
            for h in range(N_HEADS):
                n = b * N_HEADS + h
                qo_scr[1, h, pl.ds(b, nt, stride=nb), :] = _dot(
                    p[n * nt:(n + 1) * nt].astype(_BF16), v_ref[b, :, _head(h)])

    ext = jnp.concatenate([pool_hist, u_pool], axis=0)
    wins = _pool_window_sums(ext, rows, nb)
    if first_pos is not None:
        row = lax.broadcasted_iota(jnp.int32, (rows, POOL_GROUP), 0)
        pos1 = first_pos + (row >> (nb.bit_length() - 1)) + 1
    ys = []
    for g, w in enumerate(POOL_WINDOWS):
        if first_pos is not None:
            mean = wins[g] / jnp.minimum(pos1, w).astype(_F32)
        else:
            mean = wins[g] * (1.0 / w)
        d = mean - u_pool[:, g * POOL_GROUP:(g + 1) * POOL_GROUP]
        ys.append(_dot(d.astype(_BF16), pool_w[g]))
    y_pool = (jnp.concatenate(ys, axis=1) * pool_scale[...]).astype(_BF16)
    new_pool_hist = ext[rows:]
    pool_branch = _dot(y_pool, w_brp[...])

    if attention is not None:
        y_attn = jnp.concatenate([qo_scr[1, h] for h in range(N_HEADS)], axis=1)
    y_attn = y_attn.astype(_BF16)

    out = x
    chunks_per_branch = D_MODEL // _MXU_COLS
    for c in range(chunks_per_branch):
        sl = slice(c * _MXU_COLS, (c + 1) * _MXU_COLS)
        branches = (pool_branch[:, sl], rnn_branch[:, sl], _dot(y_attn, w_bra[:, sl]))
        merged = None
        for k, branch in enumerate(branches):
            term = gate[k * chunks_per_branch + c] * branch
            merged = term if merged is None else merged + term
        out = out + _dot(merged.astype(_BF16), w_out[sl, :])
    return out, new_pool_hist, new_conv_hist, new_h


def _mixer_kernel(*refs, nb, nt, carried, pos0, natural_in=False):
    rows = nb * nt
    x_ref, g_mix = refs[:2]
    weights = refs[2:17]
    if carried:
        k_ref, v_ref, out_ref, pool_st, conv_st, h_st, qo_scr = refs[17:24]
        step = pl.program_id(0)

        @pl.when(step == 0)
        def _():
            pool_st[...] = jnp.zeros_like(pool_st)
            conv_st[...] = jnp.zeros_like(conv_st)
            h_st[...] = jnp.zeros_like(h_st)

        x = _load_time_major(x_ref, refs[24], nb, nt) if natural_in else x_ref[...]
        first_pos = pos0 + step * nt if pos0 < POOL_BUF else None
        out, pool_hist, conv_hist, h = _mixer_tile(
            x, _rmsnorm(x, g_mix[...]).astype(_BF16), pool_st[...], conv_st[...], h_st[...], weights, nb, nt,
            first_pos, (k_ref, v_ref, qo_scr), None)
        out_ref[...] = out
        pool_st[...] = pool_hist
        conv_st[...] = conv_hist
        h_st[...] = h
    else:
        assert pos0 >= POOL_BUF
        (yattn_ref, pool_in, conv_in, h_in, out_ref, pool_out, conv_out, h_out) = refs[17:]
        x = x_ref[...].reshape(rows, D_MODEL)
        pool_hist = jnp.concatenate(
            [jnp.zeros((nb, D_POOL), _F32), pool_in[...].reshape(POOL_BUF * nb, D_POOL)], axis=0)
        out, pool_hist, conv_hist, h = _mixer_tile(
            x, _rmsnorm(x, g_mix[...]).astype(_BF16), pool_hist,
            conv_in[...].reshape((RNN_CONV - 1) * nb, D_RNN), h_in[...], weights, nb, nt,
            None, None, yattn_ref[...].reshape(rows, D_Q))
        out_ref[...] = out.reshape(nt, nb, D_MODEL)
        pool_out[...] = pool_hist[nb:].reshape(POOL_BUF, nb, D_POOL)
        conv_out[...] = conv_hist.reshape(RNN_CONV - 1, nb, D_RNN)
        h_out[...] = h


def _ffn_tile(x, xn, read_hist, write_hist, weights, nb, nt, act_scr):
    rows = nb * nt
    w_up, conv_w, conv_b, w_down = weights
    for c in range(0, D_FF, _FF_CHUNK):
        sl = slice(c, c + _FF_CHUNK)
        g_pre = _dot(xn, w_up[:, sl])
        val = _dot(xn, w_up[:, D_FF + c:D_FF + c + _FF_CHUNK])
        ext = jnp.concatenate([read_hist(sl), g_pre], axis=0)
        g_conv = conv_b[:, sl]
        for j in range(FFN_CONV):
            g_conv = g_conv + ext[j * nb:j * nb + rows] * conv_w[j:j + 1, sl]
        write_hist(sl, ext[rows:])
        act_scr[:, sl] = (jax.nn.gelu(g_conv) * val).astype(_BF16)
    return x + _dot(act_scr[...], w_down[...])


def _ffn_kernel(*refs, nb, nt, carried, final_norm):
    rows = nb * nt
    x_ref, g_ffn = refs[:2]
    weights = refs[2:6]
    rest = list(refs[6:])
    g_final = rest.pop(0) if final_norm else None
    finish = (lambda y: _rmsnorm(y, g_final[...])) if final_norm else (lambda y: y)
    if carried:
        out_ref, st_out, act_scr = rest[:3]

        @pl.when(pl.program_id(0) == 0)
        def _():
            st_out[...] = jnp.zeros_like(st_out)

        def write_hist(sl, hist):
            st_out[:, sl] = hist

        x = x_ref[...]
        out = finish(_ffn_tile(x, _rmsnorm(x, g_ffn[...]).astype(_BF16), lambda sl: st_out[:, sl], write_hist,
                               weights, nb, nt, act_scr))
        if final_norm:
            _store_from_time_major(out, out_ref, rest[3], nb, nt)
        else:
            out_ref[...] = out
    else:
        st_in, out_ref, st_out, act_scr = rest
        x = x_ref[...].reshape(rows, D_MODEL)

        def write_hist(sl, hist):
            st_out[:, :, sl] = hist.reshape(FFN_CONV - 1, nb, _FF_CHUNK)

        out = finish(_ffn_tile(
            x, _rmsnorm(x, g_ffn[...]).astype(_BF16),
            lambda sl: st_in[:, :, sl].reshape((FFN_CONV - 1) * nb, _FF_CHUNK), write_hist,
            weights, nb, nt, act_scr))
        out_ref[...] = out.reshape(nt, nb, D_MODEL)


def _mem_kv_kernel(mem_ref, g_ref, wk_ref, wv_ref, k_ref, v_ref, kb_ref, vb_ref):
    n_mem = mem_ref.shape[1]
    for b in range(mem_ref.shape[0]):
        mn = _rmsnorm(mem_ref[b], g_ref[...]).astype(_BF16)
        for w_ref, out_ref, out_bf_ref in ((wk_ref, k_ref, kb_ref), (wv_ref, v_ref, vb_ref)):
            kv = _dot(mn, w_ref[...])
            for h in range(N_HEADS):
                out_ref[b, pl.ds(h, n_mem, stride=N_HEADS), :] = kv[:, _head(h)]
            if out_bf_ref is kb_ref:
                for h in range(N_HEADS):
                    kb_ref[b, h] = kv[:, _head(h)].T.astype(_BF16)
            else:
                out_bf_ref[b] = kv.astype(_BF16)


def _sample_attn_kernel(x_ref, g_ref, wq_ref, k_ref, v_ref, o_ref, *, n_seq, t_seq):
    xn = _rmsnorm(x_ref[...], g_ref[...]).astype(_BF16)
    q = _dot(xn, wq_ref[...])
    rows = lambda b: slice(b * t_seq, (b + 1) * t_seq)
    hq = N_HEADS * t_seq
    s = jnp.concatenate(
        [_dot_nt(jnp.concatenate([q[rows(b), _head(h)] for h in range(N_HEADS)], axis=0).astype(_BF16),
                 k_ref[b].astype(_BF16)) for b in range(n_seq)], axis=0)
    r = lax.broadcasted_iota(jnp.int32, s.shape, 0)
    c = lax.broadcasted_iota(jnp.int32, s.shape, 1)
    same_head = ((r >> (t_seq.bit_length() - 1)) & (N_HEADS - 1)) == (c & (N_HEADS - 1))
    p = _softmax_rows(jnp.where(same_head, s * (HEAD_DIM ** -0.5), -jnp.inf))
    for b in range(n_seq):
        o = _dot(p[b * hq:(b + 1) * hq].astype(_BF16), v_ref[b].astype(_BF16))
        o_ref[rows(b), :] = jnp.concatenate([o[rows(h)] for h in range(N_HEADS)], axis=1)


def _resident(shape):
    return pl.BlockSpec(shape, lambda *_: (0,) * len(shape), pipeline_mode=pl.Buffered(1))


def _params(semantics):
    return pltpu.CompilerParams(dimension_semantics=semantics, vmem_limit_bytes=_VMEM_LIMIT_BYTES)


def _layer_resident(a, layer):
    return pl.BlockSpec((None,) + a.shape[1:], lambda *_: (layer,) + (0,) * (a.ndim - 1),
                        pipeline_mode=pl.Buffered(1))


def _mixer_weights(pw, layer):
    names = ('g_mix', 'w_in', 'w_gate', 'b_gate', 'pool_w', 'pool_scale', 'rnn_conv_w', 'rnn_conv_b',
             'rnn_wax', 'rnn_ba', 'rnn_bx', 'rnn_lambda', 'w_br_pool', 'w_br_rnn', 'w_br_attn', 'w_out')
    arrs = [pw[n] for n in names]
    return arrs, [_layer_resident(a, layer) for a in arrs]


def _mixer_prompt(h, pw, kb, vb, layer, n_seq):
    nt, nb = _PROMPT_STEPS, n_seq
    rows = nt * nb
    natural_in = h.ndim == 3
    n_tok = h.shape[0] * h.shape[1] if natural_in else h.shape[0]
    w_arrs, w_specs = _mixer_weights(pw, layer)
    tile = pl.BlockSpec((rows, D_MODEL), lambda j: (j, 0))
    in_tile = pl.BlockSpec((nb, nt, D_MODEL), lambda j: (0, j, 0)) if natural_in else tile
    scratch = [pltpu.VMEM((2, N_HEADS, rows, HEAD_DIM), _F32)]
    if natural_in:
        scratch.append(pltpu.VMEM((D_MODEL // _LANES, rows, _LANES), _F32))
    out_shapes = (jax.ShapeDtypeStruct((n_tok, D_MODEL), _F32),
                  jax.ShapeDtypeStruct(((POOL_BUF + 1) * nb, D_POOL), _F32),
                  jax.ShapeDtypeStruct(((RNN_CONV - 1) * nb, D_RNN), _F32),
                  jax.ShapeDtypeStruct((nb, D_RNN), _F32))
    whole = lambda s: pl.BlockSpec(s.shape, lambda j: (0,) * len(s.shape))
    return pl.pallas_call(
        functools.partial(_mixer_kernel, nb=nb, nt=nt, carried=True, pos0=0, natural_in=natural_in),
        grid=(n_tok // rows,),
        in_specs=[in_tile] + w_specs + [_layer_resident(kb, layer), _layer_resident(vb, layer)],
        out_specs=(tile,) + tuple(whole(s) for s in out_shapes[1:]),
        out_shape=out_shapes,
        scratch_shapes=scratch,
        compiler_params=_params(("arbitrary",)),
        name="mixer_prompt",
    )(h, *w_arrs, kb, vb)


def _mixer_sample(h, pw, layer, y_attn, pool_st, conv_st, h_st):
    nt, n_seq = h.shape[0], h.shape[1]
    nb = _SAMPLE_SEQS
    w_arrs, w_specs = _mixer_weights(pw, layer)
    seq3 = lambda a: pl.BlockSpec((a.shape[0], nb, a.shape[2]), lambda i: (0, i, 0))
    h_spec = pl.BlockSpec((nb, D_RNN), lambda i: (i, 0))
    outs = (h, pool_st, conv_st, h_st)
    return pl.pallas_call(
        functools.partial(_mixer_kernel, nb=nb, nt=nt, carried=False, pos0=PAST_LEN),
        grid=(n_seq // nb,),
        in_specs=[seq3(h)] + w_specs + [seq3(y_attn), seq3(pool_st), seq3(conv_st), h_spec],
        out_specs=(seq3(h), seq3(pool_st), seq3(conv_st), h_spec),
        out_shape=tuple(jax.ShapeDtypeStruct(a.shape, _F32) for a in outs),
        compiler_params=_params(("arbitrary",)),
        name="mixer_sample",
    )(h, *w_arrs, y_attn, pool_st, conv_st, h_st)


def _ffn_weights(pw, layer, g_final):
    arrs = [pw[n] for n in ('g_ffn', 'w_up', 'ffn_conv_w', 'ffn_conv_b', 'w_down')]
    specs = [_layer_resident(a, layer) for a in arrs]
    if g_final is not None:
        arrs.append(g_final)
        specs.append(_resident(g_final.shape))
    return arrs, specs


def _ffn_prompt(h, pw, layer, n_seq, g_final):
    nt, nb = _PROMPT_STEPS, n_seq
    rows = nt * nb
    w_arrs, w_specs = _ffn_weights(pw, layer, g_final)
    tile = pl.BlockSpec((rows, D_MODEL), lambda j: (j, 0))
    st_shape = ((FFN_CONV - 1) * nb, D_FF)
    scratch = [pltpu.VMEM((rows, D_FF), _BF16)]
    if g_final is None:
        out_tile, out_shape = tile, h.shape
    else:
        out_tile = pl.BlockSpec((nb, nt, D_MODEL), lambda j: (0, j, 0))
        out_shape = (nb, h.shape[0] // nb, D_MODEL)
        scratch.append(pltpu.VMEM((D_MODEL // _LANES, rows, _LANES), _F32))
    return pl.pallas_call(
        functools.partial(_ffn_kernel, nb=nb, nt=nt, carried=True, final_norm=g_final is not None),
        grid=(h.shape[0] // rows,),
        in_specs=[tile] + w_specs,
        out_specs=(out_tile, pl.BlockSpec(st_shape, lambda j: (0, 0))),
        out_shape=(jax.ShapeDtypeStruct(out_shape, _F32), jax.ShapeDtypeStruct(st_shape, _F32)),
        scratch_shapes=scratch,
        compiler_params=_params(("arbitrary",)),
        name="ffn_prompt",
    )(h, *w_arrs)


def _ffn_sample(h, pw, layer, st, g_final):
    nt, n_seq = h.shape[0], h.shape[1]
    nb = _SAMPLE_SEQS
    w_arrs, w_specs = _ffn_weights(pw, layer, g_final)
    seq3 = lambda a: pl.BlockSpec((a.shape[0], nb, a.shape[2]), lambda i: (0, i, 0))
    return pl.pallas_call(
        functools.partial(_ffn_kernel, nb=nb, nt=nt, carried=False, final_norm=g_final is not None),
        grid=(n_seq // nb,),
        in_specs=[seq3(h)] + w_specs + [seq3(st)],
        out_specs=(seq3(h), seq3(st)),
        out_shape=(jax.ShapeDtypeStruct(h.shape, _F32), jax.ShapeDtypeStruct(st.shape, _F32)),
        scratch_shapes=[pltpu.VMEM((nt * nb, D_FF), _BF16)],
        compiler_params=_params(("arbitrary",)),
        name="ffn_sample",
    )(h, *w_arrs, st)


def _mem_kv(mem, g_mem, w_k, w_v):
    n_seq, n_mem = mem.shape[0], mem.shape[1]
    per_layer = lambda a: pl.BlockSpec((None,) + a.shape[1:], lambda l: (l,) + (0,) * (a.ndim - 1))
    out_shapes = ((DEPTH, n_seq, n_mem * N_HEADS, HEAD_DIM),) * 2 + (
        (DEPTH, n_seq, N_HEADS, HEAD_DIM, n_mem), (DEPTH, n_seq, n_mem, D_Q))
    dtypes = (_F32, _F32, _BF16, _BF16)
    return pl.pallas_call(
        _mem_kv_kernel,
        grid=(DEPTH,),
        in_specs=[pl.BlockSpec(mem.shape, lambda l: (0, 0, 0)), per_layer(g_mem), per_layer(w_k), per_layer(w_v)],
        out_specs=tuple(pl.BlockSpec((None,) + s[1:], lambda l, nd=len(s): (l,) + (0,) * (nd - 1))
                        for s in out_shapes),
        out_shape=tuple(jax.ShapeDtypeStruct(s, d) for s, d in zip(out_shapes, dtypes)),
        compiler_params=_params(("arbitrary",)),
        name="mem_kv",
    )(mem, g_mem, w_k, w_v)


def _sample_attn(h_nat, g_mix, w_in, k, v, layer, t_seq):
    n_seq = k.shape[1]
    nb = _ATTN_SEQS
    rows = nb * t_seq
    kv_spec = pl.BlockSpec((None, nb) + k.shape[2:], lambda i: (layer, i, 0, 0))
    q_cols = pl.BlockSpec((None, D_MODEL, D_Q), lambda i: (layer, 0, (D_POOL + D_RNN) // D_Q),
                          pipeline_mode=pl.Buffered(1))
    return pl.pallas_call(
        functools.partial(_sample_attn_kernel, n_seq=nb, t_seq=t_seq),
        grid=(n_seq // nb,),
        in_specs=[pl.BlockSpec((rows, D_MODEL), lambda i: (i, 0)), _layer_resident(g_mix, layer),
                  q_cols, kv_spec, kv_spec],
        out_specs=pl.BlockSpec((rows, D_Q), lambda i: (i, 0)),
        out_shape=jax.ShapeDtypeStruct((n_seq * t_seq, D_Q), _F32),
        compiler_params=_params(("arbitrary",)),
        name="sample_attn",
    )(h_nat, g_mix, w_in, k, v)


def kernel(x_prompt, x_sample, mem_prompt, cache_mem_k, cache_mem_v, state_pool, state_rnn_conv, state_rnn_h, state_ffn_conv, g_mix, w_in, w_gate, b_gate, pool_w, pool_scale, rnn_conv_w, rnn_conv_b, rnn_wa, rnn_ba, rnn_wx, rnn_bx, rnn_lambda, g_mem, w_k, w_v, w_br_pool, w_br_rnn, w_br_attn, w_out, g_ffn, w_up, ffn_conv_w, ffn_conv_b, w_down, g_final):
    bp = x_prompt.shape[0]
    bs, ts, _ = x_sample.shape
    n_mem = mem_prompt.shape[1]
    bf = lambda a: a.astype(_BF16)
    row = lambda a: a[:, None, :]
    pw = {
        'g_mix': row(g_mix), 'w_in': bf(w_in), 'w_gate': bf(w_gate), 'b_gate': row(b_gate),
        'pool_w': bf(pool_w), 'pool_scale': row(pool_scale),
        'rnn_conv_w': rnn_conv_w, 'rnn_conv_b': row(rnn_conv_b),
        'rnn_wax': bf(jnp.concatenate([rnn_wa, rnn_wx], axis=-1)),
        'rnn_ba': row(rnn_ba), 'rnn_bx': row(rnn_bx), 'rnn_lambda': row(rnn_lambda),
        'w_br_pool': bf(w_br_pool), 'w_br_rnn': bf(w_br_rnn), 'w_br_attn': bf(w_br_attn), 'w_out': bf(w_out),
        'g_ffn': row(g_ffn), 'w_up': bf(w_up), 'ffn_conv_w': ffn_conv_w, 'ffn_conv_b': row(ffn_conv_b),
        'w_down': bf(w_down),
    }
    g_fin = g_final[None, :]

    p_mk, p_mv, kb, vb = _mem_kv(mem_prompt, row(g_mem), bf(w_k), bf(w_v))
    cache_k = cache_mem_k.reshape(DEPTH, bs, n_mem * N_HEADS, HEAD_DIM)
    cache_v = cache_mem_v.reshape(DEPTH, bs, n_mem * N_HEADS, HEAD_DIM)

    time_major = lambda a: jnp.swapaxes(a, 0, 1)
    hp = x_prompt
    hs = time_major(x_sample)
    p_pool, p_rconv, p_rh, p_fconv = [], [], [], []
    s_pool, s_rconv, s_rh, s_fconv = [], [], [], []
    for l in range(DEPTH):
        last = l == DEPTH - 1
        hp, pool_st, conv_st, h_st = _mixer_prompt(hp, pw, kb, vb, l, bp)
        hp, ffn_st = _ffn_prompt(hp, pw, l, bp, g_fin if last else None)
        p_pool.append(time_major(pool_st.reshape(POOL_BUF + 1, bp, D_POOL)[1:]))
        p_rconv.append(time_major(conv_st.reshape(RNN_CONV - 1, bp, D_RNN)))
        p_rh.append(h_st)
        p_fconv.append(time_major(ffn_st.reshape(FFN_CONV - 1, bp, D_FF)))

        hs_nat = time_major(hs).reshape(bs * ts, D_MODEL)
        o_nat = _sample_attn(hs_nat, pw['g_mix'], pw['w_in'], cache_k, cache_v, l, ts)
        y_attn = time_major(o_nat.reshape(bs, ts, D_Q))
        hs, pool_o, conv_o, h_o = _mixer_sample(
            hs, pw, l, y_attn, time_major(state_pool[l]), time_major(state_rnn_conv[l]), state_rnn_h[l])
        hs, ffn_o = _ffn_sample(hs, pw, l, time_major(state_ffn_conv[l]), g_fin if last else None)
        s_pool.append(time_major(pool_o))
        s_rconv.append(time_major(conv_o))
        s_rh.append(h_o)
        s_fconv.append(time_major(ffn_o))

    head = (N_HEADS, HEAD_DIM)
    return (hp, time_major(hs),
            jnp.stack(p_pool), jnp.stack(p_rconv), jnp.stack(p_rh), jnp.stack(p_fconv),
            p_mk.reshape(DEPTH, bp, n_mem, *head), p_mv.reshape(DEPTH, bp, n_mem, *head),
            jnp.stack(s_pool), jnp.stack(s_rconv), jnp.stack(s_rh), jnp.stack(s_fconv))
```

```python
import functools

import jax
import jax.numpy as jnp
from jax import lax
from jax.experimental import pallas as pl
from jax.experimental.pallas import tpu as pltpu

D_MODEL = 1024
DEPTH = 2
PAST_LEN = 16384
POOL_WINDOWS = (2, 4, 8, 16)
D_POOL = D_MODEL // 2
POOL_GROUP = D_POOL // len(POOL_WINDOWS)
POOL_BUF = max(POOL_WINDOWS) - 1
D_RNN = D_MODEL
N_RNN_BLOCKS = 8
RNN_BLOCK = D_RNN // N_RNN_BLOCKS
RNN_CONV = 4
RG_C = 8.0
N_HEADS = 4
HEAD_DIM = 128
D_Q = N_HEADS * HEAD_DIM
N_BRANCH = 3
D_FF = 3 * D_MODEL
FFN_CONV = 3
EPS = 1e-6

_V7X_VMEM_BYTES = 64 * 1024 * 1024
_VMEM_LIMIT_BYTES = _V7X_VMEM_BYTES - 8 * 1024 * 1024
_VMEM_LIMIT_SAMPLE_MIXER_BYTES = _V7X_VMEM_BYTES - 3 * 1024 * 1024

_PROMPT_STEPS = 64
_SAMPLE_SEQS = 32
_SAMPLE_MIXER_SEQS = 16
_FF_CHUNK = 512

_MXU_COLS = 256
_LANES = 128

_F32 = jnp.float32
_BF16 = jnp.bfloat16


def _dot(a, b):
    return jnp.dot(a, b, preferred_element_type=_F32)


def _dot_nt(a, b):
    return lax.dot_general(a, b, (((1,), (1,)), ((), ())), preferred_element_type=_F32)


def _rmsnorm(x, g):
    return x * lax.rsqrt(jnp.mean(x * x, axis=-1, keepdims=True) + EPS) * g


def _sigmoid(x):
    return 0.5 * jnp.tanh(0.5 * x) + 0.5


def _softplus(x):
    return jnp.maximum(x, 0.0) + jnp.log1p(jnp.exp(-jnp.abs(x)))


def _softmax_rows(s):
    e = jnp.exp(s - jnp.max(s, axis=-1, keepdims=True))
    return e / jnp.sum(e, axis=-1, keepdims=True)


def _head(h):
    return slice(h * HEAD_DIM, (h + 1) * HEAD_DIM)


def _pool_window_sums(ext, rows, nb):
    sums = []
    for g, w in enumerate(POOL_WINDOWS):
        s = ext[:, g * POOL_GROUP:(g + 1) * POOL_GROUP]
        off, k = 0, 1
        while k < w:
            s = s[k * nb:] + s[:-k * nb]
            off += k * nb
            k *= 2
        start = (POOL_BUF + 1) * nb - off
        sums.append(s[start:start + rows])
    return sums


def _load_time_major(x_ref, scr, nb, nt):
    for b in range(nb):
        for c in range(scr.shape[0]):
            scr[c, pl.ds(b, nt, stride=nb), :] = x_ref[b, :, c * _LANES:(c + 1) * _LANES]
    return jnp.concatenate([scr[c] for c in range(scr.shape[0])], axis=1)


def _store_from_time_major(y, y_ref, scr, nb, nt):
    for c in range(scr.shape[0]):
        scr[c] = y[:, c * _LANES:(c + 1) * _LANES]
    for b in range(nb):
        for c in range(scr.shape[0]):
            y_ref[b, :, c * _LANES:(c + 1) * _LANES] = scr[c, pl.ds(b, nt, stride=nb), :]


def _scores_per_head(q_rows, k_ref, nb):
    return jnp.concatenate(
        [_dot(q_rows(b, h).astype(_BF16), k_ref[b, h]) for b in range(nb) for h in range(N_HEADS)], axis=0)


def _values_per_head(p, v_ref, nb, nt, put):
    for b in range(nb):
        for h in range(N_HEADS):
            n = b * N_HEADS + h
            put(b, h, _dot(p[n * nt:(n + 1) * nt].astype(_BF16), v_ref[b, :, _head(h)]))


def _scores_head_rows(q_rows, k_ref, nb, nt):
    s = jnp.concatenate(
        [_dot_nt(jnp.concatenate([q_rows(b, h) for h in range(N_HEADS)], axis=0).astype(_BF16),
                 k_ref[b].astype(_BF16)) for b in range(nb)], axis=0)
    r = lax.broadcasted_iota(jnp.int32, s.shape, 0)
    c = lax.broadcasted_iota(jnp.int32, s.shape, 1)
    same_head = ((r >> (nt.bit_length() - 1)) & (N_HEADS - 1)) == (c & (N_HEADS - 1))
    return jnp.where(same_head, s, -jnp.inf)


def _values_head_rows(p, v_ref, nb, nt, put):
    hq = N_HEADS * nt
    for b in range(nb):
        o = _dot(p[b * hq:(b + 1) * hq].astype(_BF16), v_ref[b].astype(_BF16))
        for h in range(N_HEADS):
            put(b, h, o[h * nt:(h + 1) * nt])


def _mixer_tile(x, xn, pool_hist, conv_hist, h0, weights, nb, nt, first_pos, k_ref, v_ref, qo_scr, head_rows):
    rows = nb * nt
    (w_in, w_gate, b_gate, pool_w, pool_scale, conv_w, conv_b, w_ax,
     b_a, b_x, lam, w_brp, w_brr, w_bra, w_out) = weights

    n_gate_chunks = N_BRANCH * D_MODEL // _MXU_COLS
    gate = [None] * n_gate_chunks

    def gate_chunk(j):
        sl = slice(j * _MXU_COLS, (j + 1) * _MXU_COLS)
        return _sigmoid(_dot(xn, w_gate[:, sl]) + b_gate[:, sl])

    u_rnn = _dot(xn, w_in[:, D_POOL:D_POOL + D_RNN])
    u_pool = _dot(xn, w_in[:, 0:D_POOL])
    q = _dot(xn, w_in[:, D_POOL + D_RNN:])

    ext_r = jnp.concatenate([conv_hist, u_rnn], axis=0)
    xc = conv_b[...]
    for j in range(RNN_CONV):
        xc = xc + ext_r[j * nb:j * nb + rows] * conv_w[j:j + 1, :]
    new_conv_hist = ext_r[rows:]
    xcb = xc.astype(_BF16)
    log_a_unit = -RG_C * _softplus(-lam[...])
    cols, h_last = [], []
    for n in range(N_RNN_BLOCKS):
        sl = slice(n * RNN_BLOCK, (n + 1) * RNN_BLOCK)
        ri = _dot(xcb[:, sl], w_ax[n])
        r = _sigmoid(ri[:, :RNN_BLOCK] + b_a[:, sl])
        i = _sigmoid(ri[:, RNN_BLOCK:] + b_x[:, sl])
        log_a = r * log_a_unit[:, sl]
        a = jnp.exp(log_a)
        th = jnp.tanh(log_a)
        b = jnp.sqrt(-2.0 * th / (1.0 - th)) * (i * xc[:, sl])
        h = h0[:, sl]
        hs = []
        for t in range(nt):
            h = a[t * nb:(t + 1) * nb] * h + b[t * nb:(t + 1) * nb]
            hs.append(h)
        cols.append(jnp.concatenate(hs, axis=0))
        h_last.append(h)
        for j in range(n * n_gate_chunks // N_RNN_BLOCKS, (n + 1) * n_gate_chunks // N_RNN_BLOCKS):
            gate[j] = gate_chunk(j)
    y_rnn = jnp.concatenate(cols, axis=1).astype(_BF16)
    new_h = jnp.concatenate(h_last, axis=1)

    for h in range(N_HEADS):
        qo_scr[0, h] = q[:, _head(h)]

    def q_rows(b, h):
        return qo_scr[0, h, pl.ds(b, nt, stride=nb), :]

    def put_rows(b, h, o):
        qo_scr[1, h, pl.ds(b, nt, stride=nb), :] = o

    s = _scores_head_rows(q_rows, k_ref, nb, nt) if head_rows else _scores_per_head(q_rows, k_ref, nb)
    rnn_branch = _dot(y_rnn, w_brr[...])
    p = _softmax_rows(s * (HEAD_DIM ** -0.5))
    (_values_head_rows if head_rows else _values_per_head)(p, v_ref, nb, nt, put_rows)

    ext = jnp.concatenate([pool_hist, u_pool], axis=0)
    wins = _pool_window_sums(ext, rows, nb)
    if first_pos is not None:
        row = lax.broadcasted_iota(jnp.int32, (rows, POOL_GROUP), 0)
        pos1 = first_pos + (row >> (nb.bit_length() - 1)) + 1
    ys = []
    for g, w in enumerate(POOL_WINDOWS):
        if first_pos is not None:
            mean = wins[g] / jnp.minimum(pos1, w).astype(_F32)
        else:
            mean = wins[g] * (1.0 / w)
        d = mean - u_pool[:, g * POOL_GROUP:(g + 1) * POOL_GROUP]
        ys.append(_dot(d.astype(_BF16), pool_w[g]))
    y_pool = (jnp.concatenate(ys, axis=1) * pool_scale[...]).astype(_BF16)
    new_pool_hist = ext[rows:]
    pool_branch = _dot(y_pool, w_brp[...])

    y_attn = jnp.concatenate([qo_scr[1, h] for h in range(N_HEADS)], axis=1).astype(_BF16)

    out = x
    chunks_per_branch = D_MODEL // _MXU_COLS
    for c in range(chunks_per_branch):
        sl = slice(c * _MXU_COLS, (c + 1) * _MXU_COLS)
        branches = (pool_branch[:, sl], rnn_branch[:, sl], _dot(y_attn, w_bra[:, sl]))
        merged = None
        for k, branch in enumerate(branches):
            term = gate[k * chunks_per_branch + c] * branch
            merged = term if merged is None else merged + term
        out = out + _dot(merged.astype(_BF16), w_out[sl, :])
    return out, new_pool_hist, new_conv_hist, new_h


def _mixer_kernel(*refs, nb, nt, carried, pos0, natural_in=False):
    rows = nb * nt
    x_ref, g_mix = refs[:2]
    weights = refs[2:17]
    if carried:
        k_ref, v_ref, out_ref, pool_st, conv_st, h_st, qo_scr = refs[17:24]
        step = pl.program_id(0)

        @pl.when(step == 0)
        def _():
            pool_st[...] = jnp.zeros_like(pool_st)
            conv_st[...] = jnp.zeros_like(conv_st)
            h_st[...] = jnp.zeros_like(h_st)

        x = _load_time_major(x_ref, refs[24], nb, nt) if natural_in else x_ref[...]
        first_pos = pos0 + step * nt if pos0 < POOL_BUF else None
        out, pool_hist, conv_hist, h = _mixer_tile(
            x, _rmsnorm(x, g_mix[...]).astype(_BF16), pool_st[...], conv_st[...], h_st[...], weights, nb, nt,
            first_pos, k_ref, v_ref, qo_scr, head_rows=False)
        out_ref[...] = out
        pool_st[...] = pool_hist
        conv_st[...] = conv_hist
        h_st[...] = h
    else:
        assert pos0 >= POOL_BUF
        (k_ref, v_ref, pool_in, conv_in, h_in, out_ref, pool_out, conv_out, h_out, qo_scr) = refs[17:]
        x = x_ref[...].reshape(rows, D_MODEL)
        pool_hist = jnp.concatenate(
            [jnp.zeros((nb, D_POOL), _F32), pool_in[...].reshape(POOL_BUF * nb, D_POOL)], axis=0)
        out, pool_hist, conv_hist, h = _mixer_tile(
            x, _rmsnorm(x, g_mix[...]).astype(_BF16), pool_hist,
            conv_in[...].reshape((RNN_CONV - 1) * nb, D_RNN), h_in[...], weights, nb, nt,
            None, k_ref, v_ref, qo_scr, head_rows=True)
        out_ref[...] = out.reshape(nt, nb, D_MODEL)
        pool_out[...] = pool_hist[nb:].reshape(POOL_BUF, nb, D_POOL)
        conv_out[...] = conv_hist.reshape(RNN_CONV - 1, nb, D_RNN)
        h_out[...] = h


def _ffn_tile(x, xn, read_hist, write_hist, weights, nb, nt, act_scr):
    rows = nb * nt
    w_up, conv_w, conv_b, w_down = weights
    for c in range(0, D_FF, _FF_CHUNK):
        sl = slice(c, c + _FF_CHUNK)
        g_pre = _dot(xn, w_up[:, sl])
        val = _dot(xn, w_up[:, D_FF + c:D_FF + c + _FF_CHUNK])
        ext = jnp.concatenate([read_hist(sl), g_pre], axis=0)
        g_conv = conv_b[:, sl]
        for j in range(FFN_CONV):
            g_conv = g_conv + ext[j * nb:j * nb + rows] * conv_w[j:j + 1, sl]
        write_hist(sl, ext[rows:])
        act_scr[:, sl] = (jax.nn.gelu(g_conv) * val).astype(_BF16)
    return x + _dot(act_scr[...], w_down[...])


def _ffn_kernel(*refs, nb, nt, carried, final_norm):
    rows = nb * nt
    x_ref, g_ffn = refs[:2]
    weights = refs[2:6]
    rest = list(refs[6:])
    g_final = rest.pop(0) if final_norm else None
    finish = (lambda y: _rmsnorm(y, g_final[...])) if final_norm else (lambda y: y)
    if carried:
        out_ref, st_out, act_scr = rest[:3]

        @pl.when(pl.program_id(0) == 0)
        def _():
            st_out[...] = jnp.zeros_like(st_out)

        def write_hist(sl, hist):
            st_out[:, sl] = hist

        x = x_ref[...]
        out = finish(_ffn_tile(x, _rmsnorm(x, g_ffn[...]).astype(_BF16), lambda sl: st_out[:, sl], write_hist,
                               weights, nb, nt, act_scr))
        if final_norm:
            _store_from_time_major(out, out_ref, rest[3], nb, nt)
        else:
            out_ref[...] = out
    else:
        st_in, out_ref, st_out, act_scr = rest
        x = x_ref[...].reshape(rows, D_MODEL)

        def write_hist(sl, hist):
            st_out[:, :, sl] = hist.reshape(FFN_CONV - 1, nb, _FF_CHUNK)

        out = finish(_ffn_tile(
            x, _rmsnorm(x, g_ffn[...]).astype(_BF16),
            lambda sl: st_in[:, :, sl].reshape((FFN_CONV - 1) * nb, _FF_CHUNK), write_hist,
            weights, nb, nt, act_scr))
        out_ref[...] = out.reshape(nt, nb, D_MODEL)


def _mem_kv_kernel(mem_ref, g_ref, wk_ref, wv_ref, k_ref, v_ref, kb_ref, vb_ref):
    n_mem = mem_ref.shape[1]
    for b in range(mem_ref.shape[0]):
        mn = _rmsnorm(mem_ref[b], g_ref[...]).astype(_BF16)
        for w_ref, out_ref, out_bf_ref in ((wk_ref, k_ref, kb_ref), (wv_ref, v_ref, vb_ref)):
            kv = _dot(mn, w_ref[...])
            for h in range(N_HEADS):
                out_ref[b, pl.ds(h, n_mem, stride=N_HEADS), :] = kv[:, _head(h)]
            if out_bf_ref is kb_ref:
                for h in range(N_HEADS):
                    kb_ref[b, h] = kv[:, _head(h)].T.astype(_BF16)
            else:
                out_bf_ref[b] = kv.astype(_BF16)


def _resident(shape):
    return pl.BlockSpec(shape, lambda *_: (0,) * len(shape), pipeline_mode=pl.Buffered(1))


def _params(semantics, vmem_limit_bytes=_VMEM_LIMIT_BYTES):
    return pltpu.CompilerParams(dimension_semantics=semantics, vmem_limit_bytes=vmem_limit_bytes)


def _layer_resident(a, layer):
    return pl.BlockSpec((None,) + a.shape[1:], lambda *_: (layer,) + (0,) * (a.ndim - 1),
                        pipeline_mode=pl.Buffered(1))


def _mixer_weights(pw, layer):
    names = ('g_mix', 'w_in', 'w_gate', 'b_gate', 'pool_w', 'pool_scale', 'rnn_conv_w', 'rnn_conv_b',
             'rnn_wax', 'rnn_ba', 'rnn_bx', 'rnn_lambda', 'w_br_pool', 'w_br_rnn', 'w_br_attn', 'w_out')
    arrs = [pw[n] for n in names]
    return arrs, [_layer_resident(a, layer) for a in arrs]


def _mixer_prompt(h, pw, kb, vb, layer, n_seq):
    nt, nb = _PROMPT_STEPS, n_seq
    rows = nt * nb
    natural_in = h.ndim == 3
    n_tok = h.shape[0] * h.shape[1] if natural_in else h.shape[0]
    w_arrs, w_specs = _mixer_weights(pw, layer)
    tile = pl.BlockSpec((rows, D_MODEL), lambda j: (j, 0))
    in_tile = pl.BlockSpec((nb, nt, D_MODEL), lambda j: (0, j, 0)) if natural_in else tile
    scratch = [pltpu.VMEM((2, N_HEADS, rows, HEAD_DIM), _F32)]
    if natural_in:
        scratch.append(pltpu.VMEM((D_MODEL // _LANES, rows, _LANES), _F32))
    out_shapes = (jax.ShapeDtypeStruct((n_tok, D_MODEL), _F32),
                  jax.ShapeDtypeStruct(((POOL_BUF + 1) * nb, D_POOL), _F32),
                  jax.ShapeDtypeStruct(((RNN_CONV - 1) * nb, D_RNN), _F32),
                  jax.ShapeDtypeStruct((nb, D_RNN), _F32))
    whole = lambda s: pl.BlockSpec(s.shape, lambda j: (0,) * len(s.shape))
    return pl.pallas_call(
        functools.partial(_mixer_kernel, nb=nb, nt=nt, carried=True, pos0=0, natural_in=natural_in),
        grid=(n_tok // rows,),
        in_specs=[in_tile] + w_specs + [_layer_resident(kb, layer), _layer_resident(vb, layer)],
        out_specs=(tile,) + tuple(whole(s) for s in out_shapes[1:]),
        out_shape=out_shapes,
        scratch_shapes=scratch,
        compiler_params=_params(("arbitrary",)),
        name="mixer_prompt",
    )(h, *w_arrs, kb, vb)


def _mixer_sample(h, pw, layer, k, v, pool_st, conv_st, h_st):
    nt, n_seq = h.shape[0], h.shape[1]
    nb = _SAMPLE_MIXER_SEQS
    w_arrs, w_specs = _mixer_weights(pw, layer)
    seq3 = lambda a: pl.BlockSpec((a.shape[0], nb, a.shape[2]), lambda i: (0, i, 0))
    h_spec = pl.BlockSpec((nb, D_RNN), lambda i: (i, 0))
    kv_spec = pl.BlockSpec((None, nb) + k.shape[2:], lambda i: (layer, i, 0, 0))
    outs = (h, pool_st, conv_st, h_st)
    return pl.pallas_call(
        functools.partial(_mixer_kernel, nb=nb, nt=nt, carried=False, pos0=PAST_LEN),
        grid=(n_seq // nb,),
        in_specs=[seq3(h)] + w_specs + [kv_spec, kv_spec, seq3(pool_st), seq3(conv_st), h_spec],
        out_specs=(seq3(h), seq3(pool_st), seq3(conv_st), h_spec),
        out_shape=tuple(jax.ShapeDtypeStruct(a.shape, _F32) for a in outs),
        scratch_shapes=[pltpu.VMEM((2, N_HEADS, nt * nb, HEAD_DIM), _F32)],
        compiler_params=_params(("arbitrary",), _VMEM_LIMIT_SAMPLE_MIXER_BYTES),
        name="mixer_sample",
    )(h, *w_arrs, k, v, pool_st, conv_st, h_st)


def _ffn_weights(pw, layer, g_final):
    arrs = [pw[n] for n in ('g_ffn', 'w_up', 'ffn_conv_w', 'ffn_conv_b', 'w_down')]
    specs = [_layer_resident(a, layer) for a in arrs]
    if g_final is not None:
        arrs.append(g_final)
        specs.append(_resident(g_final.shape))
    return arrs, specs


def _ffn_prompt(h, pw, layer, n_seq, g_final):
    nt, nb = _PROMPT_STEPS, n_seq
    rows = nt * nb
    w_arrs, w_specs = _ffn_weights(pw, layer, g_final)
    tile = pl.BlockSpec((rows, D_MODEL), lambda j: (j, 0))
    st_shape = ((FFN_CONV - 1) * nb, D_FF)
    scratch = [pltpu.VMEM((rows, D_FF), _BF16)]
    if g_final is None:
        out_tile, out_shape = tile, h.shape
    else:
        out_tile = pl.BlockSpec((nb, nt, D_MODEL), lambda j: (0, j, 0))
        out_shape = (nb, h.shape[0] // nb, D_MODEL)
        scratch.append(pltpu.VMEM((D_MODEL // _LANES, rows, _LANES), _F32))
    return pl.pallas_call(
        functools.partial(_ffn_kernel, nb=nb, nt=nt, carried=True, final_norm=g_final is not None),
        grid=(h.shape[0] // rows,),
        in_specs=[tile] + w_specs,
        out_specs=(out_tile, pl.BlockSpec(st_shape, lambda j: (0, 0))),
        out_shape=(jax.ShapeDtypeStruct(out_shape, _F32), jax.ShapeDtypeStruct(st_shape, _F32)),
        scratch_shapes=scratch,
        compiler_params=_params(("arbitrary",)),
        name="ffn_prompt",
    )(h, *w_arrs)


def _ffn_sample(h, pw, layer, st, g_final):
    nt, n_seq = h.shape[0], h.shape[1]
    nb = _SAMPLE_SEQS
    w_arrs, w_specs = _ffn_weights(pw, layer, g_final)
    seq3 = lambda a: pl.BlockSpec((a.shape[0], nb, a.shape[2]), lambda i: (0, i, 0))
    return pl.pallas_call(
        functools.partial(_ffn_kernel, nb=nb, nt=nt, carried=False, final_norm=g_final is not None),
        grid=(n_seq // nb,),
        in_specs=[seq3(h)] + w_specs + [seq3(st)],
        out_specs=(seq3(h), seq3(st)),
        out_shape=(jax.ShapeDtypeStruct(h.shape, _F32), jax.ShapeDtypeStruct(st.shape, _F32)),
        scratch_shapes=[pltpu.VMEM((nt * nb, D_FF), _BF16)],
        compiler_params=_params(("arbitrary",)),
        name="ffn_sample",
    )(h, *w_arrs, st)


def _mem_kv(mem, g_mem, w_k, w_v):
    n_seq, n_mem = mem.shape[0], mem.shape[1]
    per_layer = lambda a: pl.BlockSpec((None,) + a.shape[1:], lambda l: (l,) + (0,) * (a.ndim - 1))
    out_shapes = ((DEPTH, n_seq, n_mem * N_HEADS, HEAD_DIM),) * 2 + (
        (DEPTH, n_seq, N_HEADS, HEAD_DIM, n_mem), (DEPTH, n_seq, n_mem, D_Q))
    dtypes = (_F32, _F32, _BF16, _BF16)
    return pl.pallas_call(
        _mem_kv_kernel,
        grid=(DEPTH,),
        in_specs=[pl.BlockSpec(mem.shape, lambda l: (0, 0, 0)), per_layer(g_mem), per_layer(w_k), per_layer(w_v)],
        out_specs=tuple(pl.BlockSpec((None,) + s[1:], lambda l, nd=len(s): (l,) + (0,) * (nd - 1))
                        for s in out_shapes),
        out_shape=tuple(jax.ShapeDtypeStruct(s, d) for s, d in zip(out_shapes, dtypes)),
        compiler_params=_params(("arbitrary",)),
        name="mem_kv",
    )(mem, g_mem, w_k, w_v)


def kernel(x_prompt, x_sample, mem_prompt, cache_mem_k, cache_mem_v, state_pool, state_rnn_conv, state_rnn_h, state_ffn_conv, g_mix, w_in, w_gate, b_gate, pool_w, pool_scale, rnn_conv_w, rnn_conv_b, rnn_wa, rnn_ba, rnn_wx, rnn_bx, rnn_lambda, g_mem, w_k, w_v, w_br_pool, w_br_rnn, w_br_attn, w_out, g_ffn, w_up, ffn_conv_w, ffn_conv_b, w_down, g_final):
    bp = x_prompt.shape[0]
    bs, ts, _ = x_sample.shape
    n_mem = mem_prompt.shape[1]
    bf = lambda a: a.astype(_BF16)
    row = lambda a: a[:, None, :]
    pw = {
        'g_mix': row(g_mix), 'w_in': bf(w_in), 'w_gate': bf(w_gate), 'b_gate': row(b_gate),
        'pool_w': bf(pool_w), 'pool_scale': row(pool_scale),
        'rnn_conv_w': rnn_conv_w, 'rnn_conv_b': row(rnn_conv_b),
        'rnn_wax': bf(jnp.concatenate([rnn_wa, rnn_wx], axis=-1)),
        'rnn_ba': row(rnn_ba), 'rnn_bx': row(rnn_bx), 'rnn_lambda': row(rnn_lambda),
        'w_br_pool': bf(w_br_pool), 'w_br_rnn': bf(w_br_rnn), 'w_br_attn': bf(w_br_attn), 'w_out': bf(w_out),
        'g_ffn': row(g_ffn), 'w_up': bf(w_up), 'ffn_conv_w': ffn_conv_w, 'ffn_conv_b': row(ffn_conv_b),
        'w_down': bf(w_down),
    }
    g_fin = g_final[None, :]

    p_mk, p_mv, kb, vb = _mem_kv(mem_prompt, row(g_mem), bf(w_k), bf(w_v))
    cache_k = cache_mem_k.reshape(DEPTH, bs, n_mem * N_HEADS, HEAD_DIM)
    cache_v = cache_mem_v.reshape(DEPTH, bs, n_mem * N_HEADS, HEAD_DIM)

    time_major = lambda a: jnp.swapaxes(a, 0, 1)
    hp = x_prompt
    hs = time_major(x_sample)
    p_pool, p_rconv, p_rh, p_fconv = [], [], [], []
    s_pool, s_rconv, s_rh, s_fconv = [], [], [], []
    for l in range(DEPTH):
        last = l == DEPTH - 1
        hp, pool_st, conv_st, h_st = _mixer_prompt(hp, pw, kb, vb, l, bp)
        hp, ffn_st = _ffn_prompt(hp, pw, l, bp, g_fin if last else None)
        p_pool.append(time_major(pool_st.reshape(POOL_BUF + 1, bp, D_POOL)[1:]))
        p_rconv.append(time_major(conv_st.reshape(RNN_CONV - 1, bp, D_RNN)))
        p_rh.append(h_st)
        p_fconv.append(time_major(ffn_st.reshape(FFN_CONV - 1, bp, D_FF)))

        hs, pool_o, conv_o, h_o = _mixer_sample(
            hs, pw, l, cache_k, cache_v, time_major(state_pool[l]), time_major(state_rnn_conv[l]),
            state_rnn_h[l])
        hs, ffn_o = _ffn_sample(hs, pw, l, time_major(state_ffn_conv[l]), g_fin if last else None)
        s_pool.append(time_major(pool_o))
        s_rconv.append(time_major(conv_o))
        s_rh.append(h_o)
        s_fconv.append(time_major(ffn_o))

    head = (N_HEADS, HEAD_DIM)
    return (hp, time_major(hs),
            jnp.stack(p_pool), jnp.stack(p_rconv), jnp.stack(p_rh), jnp.stack(p_fconv),
            p_mk.reshape(DEPTH, bp, n_mem, *head), p_mv.reshape(DEPTH, bp, n_mem, *head),
            jnp.stack(s_pool), jnp.stack(s_rconv), jnp.stack(s_rh), jnp.stack(s_fconv))
```

```python
import functools

import jax
import jax.numpy as jnp
from jax import lax
from jax.experimental import pallas as pl
from jax.experimental.pallas import tpu as pltpu

D_MODEL = 1024
DEPTH = 2
PAST_LEN = 16384
POOL_WINDOWS = (2, 4, 8, 16)
D_POOL = D_MODEL // 2
POOL_GROUP = D_POOL // len(POOL_WINDOWS)
POOL_BUF = max(POOL_WINDOWS) - 1
D_RNN = D_MODEL
N_RNN_BLOCKS = 8
RNN_BLOCK = D_RNN // N_RNN_BLOCKS
RNN_CONV = 4
RG_C = 8.0
N_HEADS = 4
HEAD_DIM = 128
D_Q = N_HEADS * HEAD_DIM
N_BRANCH = 3
D_FF = 3 * D_MODEL
FFN_CONV = 3
EPS = 1e-6

_V7X_VMEM_BYTES = 64 * 1024 * 1024
_VMEM_LIMIT_BYTES = _V7X_VMEM_BYTES - 8 * 1024 * 1024

_PROMPT_STEPS = 64
_SAMPLE_SEQS = 32
_SAMPLE_MIXER_SEQS = 16
_FF_CHUNK = 512

_MXU_COLS = 256
_LANES = 128

_F32 = jnp.float32
_BF16 = jnp.bfloat16


def _dot(a, b):
    return jnp.dot(a, b, preferred_element_type=_F32)


def _dot_nt(a, b):
    return lax.dot_general(a, b, (((1,), (1,)), ((), ())), preferred_element_type=_F32)


def _rmsnorm(x, g):
    return x * lax.rsqrt(jnp.mean(x * x, axis=-1, keepdims=True) + EPS) * g


def _sigmoid(x):
    return 0.5 * jnp.tanh(0.5 * x) + 0.5


def _softplus(x):
    return jnp.maximum(x, 0.0) + jnp.log1p(jnp.exp(-jnp.abs(x)))


def _softmax_rows(s):
    e = jnp.exp(s - jnp.max(s, axis=-1, keepdims=True))
    return e / jnp.sum(e, axis=-1, keepdims=True)


def _head(h):
    return slice(h * HEAD_DIM, (h + 1) * HEAD_DIM)


def _pool_window_sums(ext, rows, nb):
    sums = []
    for g, w in enumerate(POOL_WINDOWS):
        s = ext[:, g * POOL_GROUP:(g + 1) * POOL_GROUP]
        off, k = 0, 1
        while k < w:
            s = s[k * nb:] + s[:-k * nb]
            off += k * nb
            k *= 2
        start = (POOL_BUF + 1) * nb - off
        sums.append(s[start:start + rows])
    return sums


def _load_time_major(x_ref, scr, nb, nt):
    for b in range(nb):
        for c in range(scr.shape[0]):
            scr[c, pl.ds(b, nt, stride=nb), :] = x_ref[b, :, c * _LANES:(c + 1) * _LANES]
    return jnp.concatenate([scr[c] for c in range(scr.shape[0])], axis=1)


def _store_from_time_major(y, y_ref, scr, nb, nt):
    for c in range(scr.shape[0]):
        scr[c] = y[:, c * _LANES:(c + 1) * _LANES]
    for b in range(nb):
        for c in range(scr.shape[0]):
            y_ref[b, :, c * _LANES:(c + 1) * _LANES] = scr[c, pl.ds(b, nt, stride=nb), :]


def _scores_per_head(q_rows, k_ref, nb):
    return jnp.concatenate(
        [_dot(q_rows(b, h).astype(_BF16), k_ref[b, h]) for b in range(nb) for h in range(N_HEADS)], axis=0)


def _values_per_head(p, v_ref, nb, nt, put):
    for b in range(nb):
        for h in range(N_HEADS):
            n = b * N_HEADS + h
            put(b, h, _dot(p[n * nt:(n + 1) * nt].astype(_BF16), v_ref[b, :, _head(h)]))


def _scores_head_rows(q_rows, k_ref, nb, nt):
    s = jnp.concatenate(
        [_dot_nt(jnp.concatenate([q_rows(b, h) for h in range(N_HEADS)], axis=0).astype(_BF16),
                 k_ref[b].astype(_BF16)) for b in range(nb)], axis=0)
    r = lax.broadcasted_iota(jnp.int32, s.shape, 0)
    c = lax.broadcasted_iota(jnp.int32, s.shape, 1)
    same_head = ((r >> (nt.bit_length() - 1)) & (N_HEADS - 1)) == (c & (N_HEADS - 1))
    return jnp.where(same_head, s, -jnp.inf)


def _values_head_rows(p, v_ref, nb, nt, put):
    hq = N_HEADS * nt
    for b in range(nb):
        o = _dot(p[b * hq:(b + 1) * hq].astype(_BF16), v_ref[b].astype(_BF16))
        for h in range(N_HEADS):
            put(b, h, o[h * nt:(h + 1) * nt])


def _mixer_tile(x, g_mix, pool_hist, conv_hist, h0, weights, nb, nt, first_pos, k_ref, v_ref, qo_scr, head_rows):
    rows = nb * nt
    (w_in, w_gate, b_gate, pool_w, pool_scale, conv_w, conv_b, w_ax,
     b_a, b_x, lam, w_brp, w_brr, w_bra, w_out) = weights

    n_gate_chunks = N_BRANCH * D_MODEL // _MXU_COLS
    gate = [None] * n_gate_chunks
    xn = _rmsnorm(x, g_mix).astype(_BF16)

    def gate_chunk(j):
        sl = slice(j * _MXU_COLS, (j + 1) * _MXU_COLS)
        return _sigmoid(_dot(xn, w_gate[:, sl]) + b_gate[:, sl])

    u_rnn = _dot(xn, w_in[:, D_POOL:D_POOL + D_RNN])
    u_pool = _dot(xn, w_in[:, 0:D_POOL])
    q = _dot(xn, w_in[:, D_POOL + D_RNN:])

    ext_r = jnp.concatenate([conv_hist, u_rnn], axis=0)
    xc = conv_b[...]
    for j in range(RNN_CONV):
        xc = xc + ext_r[j * nb:j * nb + rows] * conv_w[j:j + 1, :]
    new_conv_hist = ext_r[rows:]
    xcb = xc.astype(_BF16)
    log_a_unit = -RG_C * _softplus(-lam[...])
    cols, h_last = [], []
    for n in range(N_RNN_BLOCKS):
        sl = slice(n * RNN_BLOCK, (n + 1) * RNN_BLOCK)
        ri = _dot(xcb[:, sl], w_ax[n])
        r = _sigmoid(ri[:, :RNN_BLOCK] + b_a[:, sl])
        i = _sigmoid(ri[:, RNN_BLOCK:] + b_x[:, sl])
        log_a = r * log_a_unit[:, sl]
        a = jnp.exp(log_a)
        th = jnp.tanh(log_a)
        b = jnp.sqrt(-2.0 * th / (1.0 - th)) * (i * xc[:, sl])
        h = h0[:, sl]
        hs = []
        for t in range(nt):
            h = a[t * nb:(t + 1) * nb] * h + b[t * nb:(t + 1) * nb]
            hs.append(h)
        cols.append(jnp.concatenate(hs, axis=0))
        h_last.append(h)
        for j in range(n * n_gate_chunks // N_RNN_BLOCKS, (n + 1) * n_gate_chunks // N_RNN_BLOCKS):
            gate[j] = gate_chunk(j)
    y_rnn = jnp.concatenate(cols, axis=1).astype(_BF16)
    new_h = jnp.concatenate(h_last, axis=1)

    for h in range(N_HEADS):
        qo_scr[0, h] = q[:, _head(h)]

    def q_rows(b, h):
        return qo_scr[0, h, pl.ds(b, nt, stride=nb), :]

    def put_rows(b, h, o):
        qo_scr[1, h, pl.ds(b, nt, stride=nb), :] = o

    s = _scores_head_rows(q_rows, k_ref, nb, nt) if head_rows else _scores_per_head(q_rows, k_ref, nb)
    rnn_branch = _dot(y_rnn, w_brr[...])
    p = _softmax_rows(s * (HEAD_DIM ** -0.5))
    (_values_head_rows if head_rows else _values_per_head)(p, v_ref, nb, nt, put_rows)

    ext = jnp.concatenate([pool_hist, u_pool], axis=0)
    wins = _pool_window_sums(ext, rows, nb)
    if first_pos is not None:
        row = lax.broadcasted_iota(jnp.int32, (rows, POOL_GROUP), 0)
        pos1 = first_pos + (row >> (nb.bit_length() - 1)) + 1
    ys = []
    for g, w in enumerate(POOL_WINDOWS):
        if first_pos is not None:
            mean = wins[g] / jnp.minimum(pos1, w).astype(_F32)
        else:
            mean = wins[g] * (1.0 / w)
        d = mean - u_pool[:, g * POOL_GROUP:(g + 1) * POOL_GROUP]
        ys.append(_dot(d.astype(_BF16), pool_w[g]))
    y_pool = (jnp.concatenate(ys, axis=1) * pool_scale[...]).astype(_BF16)
    new_pool_hist = ext[rows:]
    pool_branch = _dot(y_pool, w_brp[...])

    y_attn = jnp.concatenate([qo_scr[1, h] for h in range(N_HEADS)], axis=1).astype(_BF16)

    out = x
    chunks_per_branch = D_MODEL // _MXU_COLS
    for c in range(chunks_per_branch):
        sl = slice(c * _MXU_COLS, (c + 1) * _MXU_COLS)
        branches = (pool_branch[:, sl], rnn_branch[:, sl], _dot(y_attn, w_bra[:, sl]))
        merged = None
        for k, branch in enumerate(branches):
            term = gate[k * chunks_per_branch + c] * branch
            merged = term if merged is None else merged + term
        out = out + _dot(merged.astype(_BF16), w_out[sl, :])
    return out, new_pool_hist, new_conv_hist, new_h


def _mixer_kernel(*refs, nb, nt, carried, pos0, natural_in=False):
    rows = nb * nt
    x_ref, g_mix = refs[:2]
    weights = refs[2:17]
    if carried:
        k_ref, v_ref, out_ref, pool_st, conv_st, h_st, qo_scr = refs[17:24]
        step = pl.program_id(0)

        @pl.when(step == 0)
        def _():
            pool_st[...] = jnp.zeros_like(pool_st)
            conv_st[...] = jnp.zeros_like(conv_st)
            h_st[...] = jnp.zeros_like(h_st)

        x = _load_time_major(x_ref, refs[24], nb, nt) if natural_in else x_ref[...]
        first_pos = pos0 + step * nt if pos0 < POOL_BUF else None
        out, pool_hist, conv_hist, h = _mixer_tile(
            x, g_mix[...], pool_st[...], conv_st[...], h_st[...], weights, nb, nt,
            first_pos, k_ref, v_ref, qo_scr, head_rows=False)
        out_ref[...] = out
        pool_st[...] = pool_hist
        conv_st[...] = conv_hist
        h_st[...] = h
    else:
        assert pos0 >= POOL_BUF
        (k_ref, v_ref, pool_in, conv_in, h_in, out_ref, pool_out, conv_out, h_out, qo_scr) = refs[17:]
        x = x_ref[...].reshape(rows, D_MODEL)
        pool_hist = jnp.concatenate(
            [jnp.zeros((nb, D_POOL), _F32), pool_in[...].reshape(POOL_BUF * nb, D_POOL)], axis=0)
        out, pool_hist, conv_hist, h = _mixer_tile(
            x, g_mix[...], pool_hist,
            conv_in[...].reshape((RNN_CONV - 1) * nb, D_RNN), h_in[...], weights, nb, nt,
            None, k_ref, v_ref, qo_scr, head_rows=True)
        out_ref[...] = out.reshape(nt, nb, D_MODEL)
        pool_out[...] = pool_hist[nb:].reshape(POOL_BUF, nb, D_POOL)
        conv_out[...] = conv_hist.reshape(RNN_CONV - 1, nb, D_RNN)
        h_out[...] = h


def _ffn_tile(x, g_ffn, read_hist, write_hist, weights, nb, nt, act_scr):
    rows = nb * nt
    w_up, conv_w, conv_b, w_down = weights
    xn = _rmsnorm(x, g_ffn).astype(_BF16)
    for c in range(0, D_FF, _FF_CHUNK):
        sl = slice(c, c + _FF_CHUNK)
        g_pre = _dot(xn, w_up[:, sl])
        val = _dot(xn, w_up[:, D_FF + c:D_FF + c + _FF_CHUNK])
        ext = jnp.concatenate([read_hist(sl), g_pre], axis=0)
        g_conv = conv_b[:, sl]
        for j in range(FFN_CONV):
            g_conv = g_conv + ext[j * nb:j * nb + rows] * conv_w[j:j + 1, sl]
        write_hist(sl, ext[rows:])
        act_scr[:, sl] = (jax.nn.gelu(g_conv) * val).astype(_BF16)
    return x + _dot(act_scr[...], w_down[...])


def _ffn_kernel(*refs, nb, nt, carried, final_norm):
    rows = nb * nt
    x_ref, g_ffn = refs[:2]
    weights = refs[2:6]
    rest = list(refs[6:])
    g_final = rest.pop(0) if final_norm else None
    finish = (lambda y: _rmsnorm(y, g_final[...])) if final_norm else (lambda y: y)
    if carried:
        k_in, v_in, out_ref, st_out, k_out, v_out, act_scr = rest[:7]
        k_out[...] = k_in[...].astype(_BF16)
        v_out[...] = v_in[...].astype(_BF16)

        @pl.when(pl.program_id(0) == 0)
        def _():
            st_out[...] = jnp.zeros_like(st_out)

        def write_hist(sl, hist):
            st_out[:, sl] = hist

        x = x_ref[...]
        out = finish(_ffn_tile(x, g_ffn[...], lambda sl: st_out[:, sl], write_hist,
                               weights, nb, nt, act_scr))
        if final_norm:
            _store_from_time_major(out, out_ref, rest[7], nb, nt)
        else:
            out_ref[...] = out
    else:
        st_in, out_ref, st_out, act_scr = rest
        x = x_ref[...].reshape(rows, D_MODEL)

        def write_hist(sl, hist):
            st_out[:, :, sl] = hist.reshape(FFN_CONV - 1, nb, _FF_CHUNK)

        out = finish(_ffn_tile(
            x, g_ffn[...],
            lambda sl: st_in[:, :, sl].reshape((FFN_CONV - 1) * nb, _FF_CHUNK), write_hist,
            weights, nb, nt, act_scr))
        out_ref[...] = out.reshape(nt, nb, D_MODEL)


def _mem_kv_kernel(mem_ref, g_ref, wk_ref, wv_ref, k_ref, v_ref, kb_ref, vb_ref):
    n_mem = mem_ref.shape[1]
    for b in range(mem_ref.shape[0]):
        mn = _rmsnorm(mem_ref[b], g_ref[...]).astype(_BF16)
        for w_ref, out_ref, out_bf_ref in ((wk_ref, k_ref, kb_ref), (wv_ref, v_ref, vb_ref)):
            kv = _dot(mn, w_ref[...])
            for h in range(N_HEADS):
                out_ref[b, pl.ds(h, n_mem, stride=N_HEADS), :] = kv[:, _head(h)]
            if out_bf_ref is kb_ref:
                for h in range(N_HEADS):
                    kb_ref[b, h] = kv[:, _head(h)].T.astype(_BF16)
            else:
                out_bf_ref[b] = kv.astype(_BF16)


def _resident(shape):
    return pl.BlockSpec(shape, lambda *_: (0,) * len(shape), pipeline_mode=pl.Buffered(1))


def _params(semantics):
    return pltpu.CompilerParams(dimension_semantics=semantics, vmem_limit_bytes=_VMEM_LIMIT_BYTES)


def _layer_resident(a, layer):
    return pl.BlockSpec((None,) + a.shape[1:], lambda *_: (layer,) + (0,) * (a.ndim - 1),
                        pipeline_mode=pl.Buffered(1))


def _mixer_weights(pw, layer):
    names = ('g_mix', 'w_in', 'w_gate', 'b_gate', 'pool_w', 'pool_scale', 'rnn_conv_w', 'rnn_conv_b',
             'rnn_wax', 'rnn_ba', 'rnn_bx', 'rnn_lambda', 'w_br_pool', 'w_br_rnn', 'w_br_attn', 'w_out')
    arrs = [pw[n] for n in names]
    return arrs, [_layer_resident(a, layer) for a in arrs]


def _mixer_prompt(h, pw, kb, vb, layer, n_seq):
    nt, nb = _PROMPT_STEPS, n_seq
    rows = nt * nb
    natural_in = h.ndim == 3
    n_tok = h.shape[0] * h.shape[1] if natural_in else h.shape[0]
    w_arrs, w_specs = _mixer_weights(pw, layer)
    tile = pl.BlockSpec((rows, D_MODEL), lambda j: (j, 0))
    in_tile = pl.BlockSpec((nb, nt, D_MODEL), lambda j: (0, j, 0)) if natural_in else tile
    scratch = [pltpu.VMEM((2, N_HEADS, rows, HEAD_DIM), _F32)]
    if natural_in:
        scratch.append(pltpu.VMEM((D_MODEL // _LANES, rows, _LANES), _F32))
    out_shapes = (jax.ShapeDtypeStruct((n_tok, D_MODEL), _F32),
                  jax.ShapeDtypeStruct(((POOL_BUF + 1) * nb, D_POOL), _F32),
                  jax.ShapeDtypeStruct(((RNN_CONV - 1) * nb, D_RNN), _F32),
                  jax.ShapeDtypeStruct((nb, D_RNN), _F32))
    whole = lambda s: pl.BlockSpec(s.shape, lambda j: (0,) * len(s.shape))
    return pl.pallas_call(
        functools.partial(_mixer_kernel, nb=nb, nt=nt, carried=True, pos0=0, natural_in=natural_in),
        grid=(n_tok // rows,),
        in_specs=[in_tile] + w_specs + [_layer_resident(kb, layer), _layer_resident(vb, layer)],
        out_specs=(tile,) + tuple(whole(s) for s in out_shapes[1:]),
        out_shape=out_shapes,
        scratch_shapes=scratch,
        compiler_params=_params(("arbitrary",)),
        name="mixer_prompt",
    )(h, *w_arrs, kb, vb)


def _mixer_sample(h, pw, layer, k, v, pool_st, conv_st, h_st):
    nt, n_seq = h.shape[0], h.shape[1]
    nb = _SAMPLE_MIXER_SEQS
    w_arrs, w_specs = _mixer_weights(pw, layer)
    seq3 = lambda a: pl.BlockSpec((a.shape[0], nb, a.shape[2]), lambda i: (0, i, 0))
    h_spec = pl.BlockSpec((nb, D_RNN), lambda i: (i, 0))
    kv_spec = pl.BlockSpec((nb,) + k.shape[1:], lambda i: (i, 0, 0))
    outs = (h, pool_st, conv_st, h_st)
    return pl.pallas_call(
        functools.partial(_mixer_kernel, nb=nb, nt=nt, carried=False, pos0=PAST_LEN),
        grid=(n_seq // nb,),
        in_specs=[seq3(h)] + w_specs + [kv_spec, kv_spec, seq3(pool_st), seq3(conv_st), h_spec],
        out_specs=(seq3(h), seq3(pool_st), seq3(conv_st), h_spec),
        out_shape=tuple(jax.ShapeDtypeStruct(a.shape, _F32) for a in outs),
        scratch_shapes=[pltpu.VMEM((2, N_HEADS, nt * nb, HEAD_DIM), _F32)],
        compiler_params=_params(("arbitrary",)),
        name="mixer_sample",
    )(h, *w_arrs, k, v, pool_st, conv_st, h_st)


def _ffn_weights(pw, layer, g_final):
    arrs = [pw[n] for n in ('g_ffn', 'w_up', 'ffn_conv_w', 'ffn_conv_b', 'w_down')]
    specs = [_layer_resident(a, layer) for a in arrs]
    if g_final is not None:
        arrs.append(g_final)
        specs.append(_resident(g_final.shape))
    return arrs, specs


def _ffn_prompt(h, pw, layer, n_seq, g_final, cache_k, cache_v):
    nt, nb = _PROMPT_STEPS, n_seq
    rows = nt * nb
    n_steps = h.shape[0] // rows
    cache_seqs = cache_k.shape[1] // n_steps
    cache_in = pl.BlockSpec((None, cache_seqs) + cache_k.shape[2:], lambda j: (layer, j, 0, 0))
    cache_out = pl.BlockSpec((cache_seqs,) + cache_k.shape[2:], lambda j: (j, 0, 0))
    cache_shape = jax.ShapeDtypeStruct(cache_k.shape[1:], _BF16)
    w_arrs, w_specs = _ffn_weights(pw, layer, g_final)
    tile = pl.BlockSpec((rows, D_MODEL), lambda j: (j, 0))
    st_shape = ((FFN_CONV - 1) * nb, D_FF)
    scratch = [pltpu.VMEM((rows, D_FF), _BF16)]
    if g_final is None:
        out_tile, out_shape = tile, h.shape
    else:
        out_tile = pl.BlockSpec((nb, nt, D_MODEL), lambda j: (0, j, 0))
        out_shape = (nb, h.shape[0] // nb, D_MODEL)
        scratch.append(pltpu.VMEM((D_MODEL // _LANES, rows, _LANES), _F32))
    return pl.pallas_call(
        functools.partial(_ffn_kernel, nb=nb, nt=nt, carried=True, final_norm=g_final is not None),
        grid=(n_steps,),
        in_specs=[tile] + w_specs + [cache_in, cache_in],
        out_specs=(out_tile, pl.BlockSpec(st_shape, lambda j: (0, 0)), cache_out, cache_out),
        out_shape=(jax.ShapeDtypeStruct(out_shape, _F32), jax.ShapeDtypeStruct(st_shape, _F32),
                   cache_shape, cache_shape),
        scratch_shapes=scratch,
        compiler_params=_params(("arbitrary",)),
        name="ffn_prompt",
    )(h, *w_arrs, cache_k, cache_v)


def _ffn_sample(h, pw, layer, st, g_final):
    nt, n_seq = h.shape[0], h.shape[1]
    nb = _SAMPLE_SEQS
    w_arrs, w_specs = _ffn_weights(pw, layer, g_final)
    seq3 = lambda a: pl.BlockSpec((a.shape[0], nb, a.shape[2]), lambda i: (0, i, 0))
    return pl.pallas_call(
        functools.partial(_ffn_kernel, nb=nb, nt=nt, carried=False, final_norm=g_final is not None),
        grid=(n_seq // nb,),
        in_specs=[seq3(h)] + w_specs + [seq3(st)],
        out_specs=(seq3(h), seq3(st)),
        out_shape=(jax.ShapeDtypeStruct(h.shape, _F32), jax.ShapeDtypeStruct(st.shape, _F32)),
        scratch_shapes=[pltpu.VMEM((nt * nb, D_FF), _BF16)],
        compiler_params=_params(("arbitrary",)),
        name="ffn_sample",
    )(h, *w_arrs, st)


def _mem_kv(mem, g_mem, w_k, w_v):
    n_seq, n_mem = mem.shape[0], mem.shape[1]
    per_layer = lambda a: pl.BlockSpec((None,) + a.shape[1:], lambda l: (l,) + (0,) * (a.ndim - 1))
    out_shapes = ((DEPTH, n_seq, n_mem * N_HEADS, HEAD_DIM),) * 2 + (
        (DEPTH, n_seq, N_HEADS, HEAD_DIM, n_mem), (DEPTH, n_seq, n_mem, D_Q))
    dtypes = (_F32, _F32, _BF16, _BF16)
    return pl.pallas_call(
        _mem_kv_kernel,
        grid=(DEPTH,),
        in_specs=[pl.BlockSpec(mem.shape, lambda l: (0, 0, 0)), per_layer(g_mem), per_layer(w_k), per_layer(w_v)],
        out_specs=tuple(pl.BlockSpec((None,) + s[1:], lambda l, nd=len(s): (l,) + (0,) * (nd - 1))
                        for s in out_shapes),
        out_shape=tuple(jax.ShapeDtypeStruct(s, d) for s, d in zip(out_shapes, dtypes)),
        compiler_params=_params(("arbitrary",)),
        name="mem_kv",
    )(mem, g_mem, w_k, w_v)


def kernel(x_prompt, x_sample, mem_prompt, cache_mem_k, cache_mem_v, state_pool, state_rnn_conv, state_rnn_h, state_ffn_conv, g_mix, w_in, w_gate, b_gate, pool_w, pool_scale, rnn_conv_w, rnn_conv_b, rnn_wa, rnn_ba, rnn_wx, rnn_bx, rnn_lambda, g_mem, w_k, w_v, w_br_pool, w_br_rnn, w_br_attn, w_out, g_ffn, w_up, ffn_conv_w, ffn_conv_b, w_down, g_final):
    bp = x_prompt.shape[0]
    bs, ts, _ = x_sample.shape
    n_mem = mem_prompt.shape[1]
    bf = lambda a: a.astype(_BF16)
    row = lambda a: a[:, None, :]
    pw = {
        'g_mix': row(g_mix), 'w_in': bf(w_in), 'w_gate': bf(w_gate), 'b_gate': row(b_gate),
        'pool_w': bf(pool_w), 'pool_scale': row(pool_scale),
        'rnn_conv_w': rnn_conv_w, 'rnn_conv_b': row(rnn_conv_b),
        'rnn_wax': bf(jnp.concatenate([rnn_wa, rnn_wx], axis=-1)),
        'rnn_ba': row(rnn_ba), 'rnn_bx': row(rnn_bx), 'rnn_lambda': row(rnn_lambda),
        'w_br_pool': bf(w_br_pool), 'w_br_rnn': bf(w_br_rnn), 'w_br_attn': bf(w_br_attn), 'w_out': bf(w_out),
        'g_ffn': row(g_ffn), 'w_up': bf(w_up), 'ffn_conv_w': ffn_conv_w, 'ffn_conv_b': row(ffn_conv_b),
        'w_down': bf(w_down),
    }
    g_fin = g_final[None, :]

    p_mk, p_mv, kb, vb = _mem_kv(mem_prompt, row(g_mem), bf(w_k), bf(w_v))
    cache_k = cache_mem_k.reshape(DEPTH, bs, n_mem * N_HEADS, HEAD_DIM)
    cache_v = cache_mem_v.reshape(DEPTH, bs, n_mem * N_HEADS, HEAD_DIM)

    time_major = lambda a: jnp.swapaxes(a, 0, 1)
    hp = x_prompt
    hs = time_major(x_sample)
    p_pool, p_rconv, p_rh, p_fconv = [], [], [], []
    s_pool, s_rconv, s_rh, s_fconv = [], [], [], []
    for l in range(DEPTH):
        last = l == DEPTH - 1
        hp, pool_st, conv_st, h_st = _mixer_prompt(hp, pw, kb, vb, l, bp)
        hp, ffn_st, layer_k, layer_v = _ffn_prompt(hp, pw, l, bp, g_fin if last else None, cache_k, cache_v)
        p_pool.append(time_major(pool_st.reshape(POOL_BUF + 1, bp, D_POOL)[1:]))
        p_rconv.append(time_major(conv_st.reshape(RNN_CONV - 1, bp, D_RNN)))
        p_rh.append(h_st)
        p_fconv.append(time_major(ffn_st.reshape(FFN_CONV - 1, bp, D_FF)))

        hs, pool_o, conv_o, h_o = _mixer_sample(
            hs, pw, l, layer_k, layer_v, time_major(state_pool[l]), time_major(state_rnn_conv[l]),
            state_rnn_h[l])
        hs, ffn_o = _ffn_sample(hs, pw, l, time_major(state_ffn_conv[l]), g_fin if last else None)
        s_pool.append(time_major(pool_o))
        s_rconv.append(time_major(conv_o))
        s_rh.append(h_o)
        s_fconv.append(time_major(ffn_o))

    head = (N_HEADS, HEAD_DIM)
    return (hp, time_major(hs),
            jnp.stack(p_pool), jnp.stack(p_rconv), jnp.stack(p_rh), jnp.stack(p_fconv),
            p_mk.reshape(DEPTH, bp, n_mem, *head), p_mv.reshape(DEPTH, bp, n_mem, *head),
            jnp.stack(s_pool), jnp.stack(s_rconv), jnp.stack(s_rh), jnp.stack(s_fconv))
```

```python
import functools

import jax
import jax.numpy as jnp
from jax import lax
from jax.experimental import pallas as pl
from jax.experimental.pallas import tpu as pltpu

D_MODEL = 1024
DEPTH = 2
PAST_LEN = 16384
POOL_WINDOWS = (2, 4, 8, 16)
D_POOL = D_MODEL // 2
POOL_GROUP = D_POOL // len(POOL_WINDOWS)
POOL_BUF = max(POOL_WINDOWS) - 1
D_RNN = D_MODEL
N_RNN_BLOCKS = 8
RNN_BLOCK = D_RNN // N_RNN_BLOCKS
RNN_CONV = 4
RG_C = 8.0
N_HEADS = 4
HEAD_DIM = 128
D_Q = N_HEADS * HEAD_DIM
N_BRANCH = 3
D_FF = 3 * D_MODEL
FFN_CONV = 3
EPS = 1e-6

_V7X_VMEM_BYTES = 64 * 1024 * 1024
_VMEM_LIMIT_BYTES = _V7X_VMEM_BYTES - 8 * 1024 * 1024
_VMEM_LIMIT_SAMPLE_MIXER_BYTES = _V7X_VMEM_BYTES - 3 * 1024 * 1024

_PROMPT_STEPS = 64
_SAMPLE_SEQS = 32
_SAMPLE_MIXER_SEQS = 16
_FF_CHUNK = 512

_MXU_COLS = 256
_LANES = 128

_ROW_B_GATE, _ROW_G_MIX, _ROW_POOL_SCALE, _ROW_CONV_B, _ROW_B_A, _ROW_B_X, _ROW_LAMBDA, _ROW_CONV_W = range(8)
_MIXER_VEC_ROWS = 16
_FROW_CONV_W, _FROW_CONV_B, _FROW_G_FFN = 0, FFN_CONV, FFN_CONV + 1
_FFN_VEC_ROWS = 8

_F32 = jnp.float32
_BF16 = jnp.bfloat16


def _dot(a, b):
    return jnp.dot(a, b, preferred_element_type=_F32)


def _dot_nt(a, b):
    return lax.dot_general(a, b, (((1,), (1,)), ((), ())), preferred_element_type=_F32)


def _rmsnorm(x, g):
    return x * lax.rsqrt(jnp.mean(x * x, axis=-1, keepdims=True) + EPS) * g


def _sigmoid(x):
    return 0.5 * jnp.tanh(0.5 * x) + 0.5


def _softplus(x):
    return jnp.maximum(x, 0.0) + jnp.log1p(jnp.exp(-jnp.abs(x)))


def _softmax_rows(s):
    e = jnp.exp(s - jnp.max(s, axis=-1, keepdims=True))
    return e / jnp.sum(e, axis=-1, keepdims=True)


def _head(h):
    return slice(h * HEAD_DIM, (h + 1) * HEAD_DIM)


def _pool_window_sums(ext, rows, nb):
    sums = []
    for g, w in enumerate(POOL_WINDOWS):
        s = ext[:, g * POOL_GROUP:(g + 1) * POOL_GROUP]
        off, k = 0, 1
        while k < w:
            s = s[k * nb:] + s[:-k * nb]
            off += k * nb
            k *= 2
        start = (POOL_BUF + 1) * nb - off
        sums.append(s[start:start + rows])
    return sums


def _load_time_major(x_ref, scr, nb, nt):
    for b in range(nb):
        for c in range(scr.shape[0]):
            scr[c, pl.ds(b, nt, stride=nb), :] = x_ref[b, :, c * _LANES:(c + 1) * _LANES]
    return jnp.concatenate([scr[c] for c in range(scr.shape[0])], axis=1)


def _store_from_time_major(y, y_ref, scr, nb, nt):
    for c in range(scr.shape[0]):
        scr[c] = y[:, c * _LANES:(c + 1) * _LANES]
    for b in range(nb):
        for c in range(scr.shape[0]):
            y_ref[b, :, c * _LANES:(c + 1) * _LANES] = scr[c, pl.ds(b, nt, stride=nb), :]


def _scores_per_head(q_rows, k_ref, nb):
    return jnp.concatenate(
        [_dot(q_rows(b, h).astype(_BF16), k_ref[b, h]) for b in range(nb) for h in range(N_HEADS)], axis=0)


def _values_per_head(p, v_ref, nb, nt, put):
    for b in range(nb):
        for h in range(N_HEADS):
            n = b * N_HEADS + h
            put(b, h, _dot(p[n * nt:(n + 1) * nt].astype(_BF16), v_ref[b, :, _head(h)]))


def _scores_head_rows(q_rows, k_ref, nb, nt):
    s = jnp.concatenate(
        [_dot_nt(jnp.concatenate([q_rows(b, h) for h in range(N_HEADS)], axis=0).astype(_BF16),
                 k_ref[b].astype(_BF16)) for b in range(nb)], axis=0)
    r = lax.broadcasted_iota(jnp.int32, s.shape, 0)
    c = lax.broadcasted_iota(jnp.int32, s.shape, 1)
    same_head = ((r >> (nt.bit_length() - 1)) & (N_HEADS - 1)) == (c & (N_HEADS - 1))
    return jnp.where(same_head, s, -jnp.inf)


def _values_head_rows(p, v_ref, nb, nt, put):
    hq = N_HEADS * nt
    for b in range(nb):
        o = _dot(p[b * hq:(b + 1) * hq].astype(_BF16), v_ref[b].astype(_BF16))
        for h in range(N_HEADS):
            put(b, h, o[h * nt:(h + 1) * nt])


def _mixer_tile(x, pool_hist, conv_hist, h0, weights, nb, nt, first_pos, k_ref, v_ref, qo_scr, head_rows):
    rows = nb * nt
    vecs, w_in, w_gate, pool_w, w_ax, w_br, w_out = weights

    def vec(r, cols=slice(0, D_MODEL)):
        return vecs[r:r + 1, cols]

    n_gate_chunks = N_BRANCH * D_MODEL // _MXU_COLS
    gate = [None] * n_gate_chunks
    xn = _rmsnorm(x, vec(_ROW_G_MIX)).astype(_BF16)

    def gate_chunk(j):
        sl = slice(j * _MXU_COLS, (j + 1) * _MXU_COLS)
        return _sigmoid(_dot(xn, w_gate[:, sl]) + vec(_ROW_B_GATE, sl))

    u_rnn = _dot(xn, w_in[:, D_POOL:D_POOL + D_RNN])
    u_pool = _dot(xn, w_in[:, 0:D_POOL])
    q = _dot(xn, w_in[:, D_POOL + D_RNN:])

    ext_r = jnp.concatenate([conv_hist, u_rnn], axis=0)
    xc = vec(_ROW_CONV_B)
    for j in range(RNN_CONV):
        xc = xc + ext_r[j * nb:j * nb + rows] * vec(_ROW_CONV_W + j)
    new_conv_hist = ext_r[rows:]
    xcb = xc.astype(_BF16)
    log_a_unit = -RG_C * _softplus(-vec(_ROW_LAMBDA))
    cols, h_last = [], []
    for n in range(N_RNN_BLOCKS):
        sl = slice(n * RNN_BLOCK, (n + 1) * RNN_BLOCK)
        ri = _dot(xcb[:, sl], w_ax[n])
        r = _sigmoid(ri[:, :RNN_BLOCK] + vec(_ROW_B_A, sl))
        i = _sigmoid(ri[:, RNN_BLOCK:] + vec(_ROW_B_X, sl))
        log_a = r * log_a_unit[:, sl]
        a = jnp.exp(log_a)
        th = jnp.tanh(log_a)
        b = jnp.sqrt(-2.0 * th / (1.0 - th)) * (i * xc[:, sl])
        h = h0[:, sl]
        hs = []
        for t in range(nt):
            h = a[t * nb:(t + 1) * nb] * h + b[t * nb:(t + 1) * nb]
            hs.append(h)
        cols.append(jnp.concatenate(hs, axis=0))
        h_last.append(h)
        for j in range(n * n_gate_chunks // N_RNN_BLOCKS, (n + 1) * n_gate_chunks // N_RNN_BLOCKS):
            gate[j] = gate_chunk(j)
    y_rnn = jnp.concatenate(cols, axis=1).astype(_BF16)
    new_h = jnp.concatenate(h_last, axis=1)

    for h in range(N_HEADS):
        qo_scr[0, h] = q[:, _head(h)]

    def q_rows(b, h):
        return qo_scr[0, h, pl.ds(b, nt, stride=nb), :]

    def put_rows(b, h, o):
        qo_scr[1, h, pl.ds(b, nt, stride=nb), :] = o

    s = _scores_head_rows(q_rows, k_ref, nb, nt) if head_rows else _scores_per_head(q_rows, k_ref, nb)
    rnn_branch = _dot(y_rnn, w_br[D_POOL:D_POOL + D_RNN, :])
    p = _softmax_rows(s * (HEAD_DIM ** -0.5))
    (_values_head_rows if head_rows else _values_per_head)(p, v_ref, nb, nt, put_rows)

    ext = jnp.concatenate([pool_hist, u_pool], axis=0)
    wins = _pool_window_sums(ext, rows, nb)
    if first_pos is not None:
        row = lax.broadcasted_iota(jnp.int32, (rows, POOL_GROUP), 0)
        pos1 = first_pos + (row >> (nb.bit_length() - 1)) + 1
    ys = []
    for g, w in enumerate(POOL_WINDOWS):
        if first_pos is not None:
            mean = wins[g] / jnp.minimum(pos1, w).astype(_F32)
        else:
            mean = wins[g] * (1.0 / w)
        d = mean - u_pool[:, g * POOL_GROUP:(g + 1) * POOL_GROUP]
        ys.append(_dot(d.astype(_BF16), pool_w[g]))
    y_pool = (jnp.concatenate(ys, axis=1) * vec(_ROW_POOL_SCALE, slice(0, D_POOL))).astype(_BF16)
    new_pool_hist = ext[rows:]
    pool_branch = _dot(y_pool, w_br[0:D_POOL, :])

    y_attn = jnp.concatenate([qo_scr[1, h] for h in range(N_HEADS)], axis=1).astype(_BF16)

    out = x
    chunks_per_branch = D_MODEL // _MXU_COLS
    for c in range(chunks_per_branch):
        sl = slice(c * _MXU_COLS, (c + 1) * _MXU_COLS)
        branches = (pool_branch[:, sl], rnn_branch[:, sl], _dot(y_attn, w_br[D_POOL + D_RNN:, sl]))
        merged = None
        for k, branch in enumerate(branches):
            term = gate[k * chunks_per_branch + c] * branch
            merged = term if merged is None else merged + term
        out = out + _dot(merged.astype(_BF16), w_out[sl, :])
    return out, new_pool_hist, new_conv_hist, new_h


def _mixer_kernel(*refs, nb, nt, carried, pos0, natural_in=False):
    rows = nb * nt
    x_ref = refs[0]
    weights = refs[1:8]
    if carried:
        k_ref, v_ref, out_ref, pool_st, conv_st, h_st, qo_scr = refs[8:15]
        step = pl.program_id(0)

        @pl.when(step == 0)
        def _():
            pool_st[...] = jnp.zeros_like(pool_st)
            conv_st[...] = jnp.zeros_like(conv_st)
            h_st[...] = jnp.zeros_like(h_st)

        x = _load_time_major(x_ref, refs[15], nb, nt) if natural_in else x_ref[...]
        first_pos = pos0 + step * nt if pos0 < POOL_BUF else None
        out, pool_hist, conv_hist, h = _mixer_tile(
            x, pool_st[...], conv_st[...], h_st[...], weights, nb, nt,
            first_pos, k_ref, v_ref, qo_scr, head_rows=False)
        out_ref[...] = out
        pool_st[...] = pool_hist
        conv_st[...] = conv_hist
        h_st[...] = h
    else:
        assert pos0 >= POOL_BUF
        (k_ref, v_ref, pool_in, conv_in, h_in, out_ref, pool_out, conv_out, h_out, qo_scr) = refs[8:]
        x = x_ref[...].reshape(rows, D_MODEL)
        pool_hist = jnp.concatenate(
            [jnp.zeros((nb, D_POOL), _F32), pool_in[...].reshape(POOL_BUF * nb, D_POOL)], axis=0)
        out, pool_hist, conv_hist, h = _mixer_tile(
            x, pool_hist,
            conv_in[...].reshape((RNN_CONV - 1) * nb, D_RNN), h_in[...], weights, nb, nt,
            None, k_ref, v_ref, qo_scr, head_rows=True)
        out_ref[...] = out.reshape(nt, nb, D_MODEL)
        pool_out[...] = pool_hist[nb:].reshape(POOL_BUF, nb, D_POOL)
        conv_out[...] = conv_hist.reshape(RNN_CONV - 1, nb, D_RNN)
        h_out[...] = h


def _ffn_tile(x, read_hist, write_hist, weights, nb, nt, act_scr):
    rows = nb * nt
    vecs, w_up, w_down = weights
    xn = _rmsnorm(x, vecs[_FROW_G_FFN:_FROW_G_FFN + 1, 0:D_MODEL]).astype(_BF16)
    for c in range(0, D_FF, _FF_CHUNK):
        sl = slice(c, c + _FF_CHUNK)
        g_pre = _dot(xn, w_up[:, sl])
        val = _dot(xn, w_up[:, D_FF + c:D_FF + c + _FF_CHUNK])
        ext = jnp.concatenate([read_hist(sl), g_pre], axis=0)
        g_conv = vecs[_FROW_CONV_B:_FROW_CONV_B + 1, sl]
        for j in range(FFN_CONV):
            g_conv = g_conv + ext[j * nb:j * nb + rows] * vecs[_FROW_CONV_W + j:_FROW_CONV_W + j + 1, sl]
        write_hist(sl, ext[rows:])
        act_scr[:, sl] = (jax.nn.gelu(g_conv) * val).astype(_BF16)
    return x + _dot(act_scr[...], w_down[...])


def _ffn_kernel(*refs, nb, nt, carried, final_norm):
    rows = nb * nt
    x_ref = refs[0]
    weights = refs[1:4]
    rest = list(refs[4:])
    g_final = rest.pop(0) if final_norm else None
    finish = (lambda y: _rmsnorm(y, g_final[...])) if final_norm else (lambda y: y)
    if carried:
        out_ref, st_out, act_scr = rest[:3]

        @pl.when(pl.program_id(0) == 0)
        def _():
            st_out[...] = jnp.zeros_like(st_out)

        def write_hist(sl, hist):
            st_out[:, sl] = hist

        x = x_ref[...]
        out = finish(_ffn_tile(x, lambda sl: st_out[:, sl], write_hist,
                               weights, nb, nt, act_scr))
        if final_norm:
            _store_from_time_major(out, out_ref, rest[3], nb, nt)
        else:
            out_ref[...] = out
    else:
        st_in, out_ref, st_out, act_scr = rest
        x = x_ref[...].reshape(rows, D_MODEL)

        def write_hist(sl, hist):
            st_out[:, :, sl] = hist.reshape(FFN_CONV - 1, nb, _FF_CHUNK)

        out = finish(_ffn_tile(
            x,
            lambda sl: st_in[:, :, sl].reshape((FFN_CONV - 1) * nb, _FF_CHUNK), write_hist,
            weights, nb, nt, act_scr))
        out_ref[...] = out.reshape(nt, nb, D_MODEL)


def _mem_kv_kernel(mem_ref, g_ref, wk_ref, wv_ref, k_ref, v_ref, kb_ref, vb_ref):
    n_mem = mem_ref.shape[1]
    for b in range(mem_ref.shape[0]):
        mn = _rmsnorm(mem_ref[b], g_ref[...]).astype(_BF16)
        for w_ref, out_ref, out_bf_ref in ((wk_ref, k_ref, kb_ref), (wv_ref, v_ref, vb_ref)):
            kv = _dot(mn, w_ref[...])
            for h in range(N_HEADS):
                out_ref[b, pl.ds(h, n_mem, stride=N_HEADS), :] = kv[:, _head(h)]
            if out_bf_ref is kb_ref:
                for h in range(N_HEADS):
                    kb_ref[b, h] = kv[:, _head(h)].T.astype(_BF16)
            else:
                out_bf_ref[b] = kv.astype(_BF16)


def _resident(shape):
    return pl.BlockSpec(shape, lambda *_: (0,) * len(shape), pipeline_mode=pl.Buffered(1))


def _params(semantics, vmem_limit_bytes=_VMEM_LIMIT_BYTES):
    return pltpu.CompilerParams(dimension_semantics=semantics, vmem_limit_bytes=vmem_limit_bytes)


def _layer_resident(a, layer):
    return pl.BlockSpec((None,) + a.shape[1:], lambda *_: (layer,) + (0,) * (a.ndim - 1),
                        pipeline_mode=pl.Buffered(1))


def _pack_rows(arrays, n_rows, width):
    rows = [a[:, None, :] if a.ndim == 2 else a for a in arrays]
    rows = [jnp.pad(r, ((0, 0), (0, 0), (0, width - r.shape[2]))) for r in rows]
    packed = jnp.concatenate(rows, axis=1)
    return jnp.pad(packed, ((0, 0), (0, n_rows - packed.shape[1]), (0, 0)))


def _mixer_weights(pw, layer):
    arrs = [pw[n] for n in ('mixer_vecs', 'w_in', 'w_gate', 'pool_w', 'rnn_wax', 'w_br', 'w_out')]
    return arrs, [_layer_resident(a, layer) for a in arrs]


def _mixer_prompt(h, pw, kb, vb, layer, n_seq):
    nt, nb = _PROMPT_STEPS, n_seq
    rows = nt * nb
    natural_in = h.ndim == 3
    n_tok = h.shape[0] * h.shape[1] if natural_in else h.shape[0]
    w_arrs, w_specs = _mixer_weights(pw, layer)
    tile = pl.BlockSpec((rows, D_MODEL), lambda j: (j, 0))
    in_tile = pl.BlockSpec((nb, nt, D_MODEL), lambda j: (0, j, 0)) if natural_in else tile
    scratch = [pltpu.VMEM((2, N_HEADS, rows, HEAD_DIM), _F32)]
    if natural_in:
        scratch.append(pltpu.VMEM((D_MODEL // _LANES, rows, _LANES), _F32))
    out_shapes = (jax.ShapeDtypeStruct((n_tok, D_MODEL), _F32),
                  jax.ShapeDtypeStruct(((POOL_BUF + 1) * nb, D_POOL), _F32),
                  jax.ShapeDtypeStruct(((RNN_CONV - 1) * nb, D_RNN), _F32),
                  jax.ShapeDtypeStruct((nb, D_RNN), _F32))
    whole = lambda s: pl.BlockSpec(s.shape, lambda j: (0,) * len(s.shape))
    return pl.pallas_call(
        functools.partial(_mixer_kernel, nb=nb, nt=nt, carried=True, pos0=0, natural_in=natural_in),
        grid=(n_tok // rows,),
        in_specs=[in_tile] + w_specs + [_layer_resident(kb, layer), _layer_resident(vb, layer)],
        out_specs=(tile,) + tuple(whole(s) for s in out_shapes[1:]),
        out_shape=out_shapes,
        scratch_shapes=scratch,
        compiler_params=_params(("arbitrary",)),
        name="mixer_prompt",
    )(h, *w_arrs, kb, vb)


def _mixer_sample(h, pw, layer, k, v, pool_st, conv_st, h_st):
    nt, n_seq = h.shape[0], h.shape[1]
    nb = _SAMPLE_MIXER_SEQS
    w_arrs, w_specs = _mixer_weights(pw, layer)
    seq3 = lambda a: pl.BlockSpec((a.shape[0], nb, a.shape[2]), lambda i: (0, i, 0))
    h_spec = pl.BlockSpec((nb, D_RNN), lambda i: (i, 0))
    kv_spec = pl.BlockSpec((None, nb) + k.shape[2:], lambda i: (layer, i, 0, 0))
    outs = (h, pool_st, conv_st, h_st)
    return pl.pallas_call(
        functools.partial(_mixer_kernel, nb=nb, nt=nt, carried=False, pos0=PAST_LEN),
        grid=(n_seq // nb,),
        in_specs=[seq3(h)] + w_specs + [kv_spec, kv_spec, seq3(pool_st), seq3(conv_st), h_spec],
        out_specs=(seq3(h), seq3(pool_st), seq3(conv_st), h_spec),
        out_shape=tuple(jax.ShapeDtypeStruct(a.shape, _F32) for a in outs),
        scratch_shapes=[pltpu.VMEM((2, N_HEADS, nt * nb, HEAD_DIM), _F32)],
        compiler_params=_params(("arbitrary",), _VMEM_LIMIT_SAMPLE_MIXER_BYTES),
        name="mixer_sample",
    )(h, *w_arrs, k, v, pool_st, conv_st, h_st)


def _ffn_weights(pw, layer, g_final):
    arrs = [pw[n] for n in ('ffn_vecs', 'w_up', 'w_down')]
    specs = [_layer_resident(a, layer) for a in arrs]
    if g_final is not None:
        arrs.append(g_final)
        specs.append(_resident(g_final.shape))
    return arrs, specs


def _ffn_prompt(h, pw, layer, n_seq, g_final):
    nt, nb = _PROMPT_STEPS, n_seq
    rows = nt * nb
    w_arrs, w_specs = _ffn_weights(pw, layer, g_final)
    tile = pl.BlockSpec((rows, D_MODEL), lambda j: (j, 0))
    st_shape = ((FFN_CONV - 1) * nb, D_FF)
    scratch = [pltpu.VMEM((rows, D_FF), _BF16)]
    if g_final is None:
        out_tile, out_shape = tile, h.shape
    else:
        out_tile = pl.BlockSpec((nb, nt, D_MODEL), lambda j: (0, j, 0))
        out_shape = (nb, h.shape[0] // nb, D_MODEL)
        scratch.append(pltpu.VMEM((D_MODEL // _LANES, rows, _LANES), _F32))
    return pl.pallas_call(
        functools.partial(_ffn_kernel, nb=nb, nt=nt, carried=True, final_norm=g_final is not None),
        grid=(h.shape[0] // rows,),
        in_specs=[tile] + w_specs,
        out_specs=(out_tile, pl.BlockSpec(st_shape, lambda j: (0, 0))),
        out_shape=(jax.ShapeDtypeStruct(out_shape, _F32), jax.ShapeDtypeStruct(st_shape, _F32)),
        scratch_shapes=scratch,
        compiler_params=_params(("arbitrary",)),
        name="ffn_prompt",
    )(h, *w_arrs)


def _ffn_sample(h, pw, layer, st, g_final):
    nt, n_seq = h.shape[0], h.shape[1]
    nb = _SAMPLE_SEQS
    w_arrs, w_specs = _ffn_weights(pw, layer, g_final)
    seq3 = lambda a: pl.BlockSpec((a.shape[0], nb, a.shape[2]), lambda i: (0, i, 0))
    return pl.pallas_call(
        functools.partial(_ffn_kernel, nb=nb, nt=nt, carried=False, final_norm=g_final is not None),
        grid=(n_seq // nb,),
        in_specs=[seq3(h)] + w_specs + [seq3(st)],
        out_specs=(seq3(h), seq3(st)),
        out_shape=(jax.ShapeDtypeStruct(h.shape, _F32), jax.ShapeDtypeStruct(st.shape, _F32)),
        scratch_shapes=[pltpu.VMEM((nt * nb, D_FF), _BF16)],
        compiler_params=_params(("arbitrary",)),
        name="ffn_sample",
    )(h, *w_arrs, st)


def _mem_kv(mem, g_mem, w_k, w_v):
    n_seq, n_mem = mem.shape[0], mem.shape[1]
    per_layer = lambda a: pl.BlockSpec((None,) + a.shape[1:], lambda l: (l,) + (0,) * (a.ndim - 1))
    out_shapes = ((DEPTH, n_seq, n_mem * N_HEADS, HEAD_DIM),) * 2 + (
        (DEPTH, n_seq, N_HEADS, HEAD_DIM, n_mem), (DEPTH, n_seq, n_mem, D_Q))
    dtypes = (_F32, _F32, _BF16, _BF16)
    return pl.pallas_call(
        _mem_kv_kernel,
        grid=(DEPTH,),
        in_specs=[pl.BlockSpec(mem.shape, lambda l: (0, 0, 0)), per_layer(g_mem), per_layer(w_k), per_layer(w_v)],
        out_specs=tuple(pl.BlockSpec((None,) + s[1:], lambda l, nd=len(s): (l,) + (0,) * (nd - 1))
                        for s in out_shapes),
        out_shape=tuple(jax.ShapeDtypeStruct(s, d) for s, d in zip(out_shapes, dtypes)),
        compiler_params=_params(("arbitrary",)),
        name="mem_kv",
    )(mem, g_mem, w_k, w_v)


def kernel(x_prompt, x_sample, mem_prompt, cache_mem_k, cache_mem_v, state_pool, state_rnn_conv, state_rnn_h, state_ffn_conv, g_mix, w_in, w_gate, b_gate, pool_w, pool_scale, rnn_conv_w, rnn_conv_b, rnn_wa, rnn_ba, rnn_wx, rnn_bx, rnn_lambda, g_mem, w_k, w_v, w_br_pool, w_br_rnn, w_br_attn, w_out, g_ffn, w_up, ffn_conv_w, ffn_conv_b, w_down, g_final):
    bp = x_prompt.shape[0]
    bs, ts, _ = x_sample.shape
    n_mem = mem_prompt.shape[1]
    bf = lambda a: a.astype(_BF16)
    pw = {
        'mixer_vecs': _pack_rows([b_gate, g_mix, pool_scale, rnn_conv_b, rnn_ba, rnn_bx, rnn_lambda, rnn_conv_w],
                                 _MIXER_VEC_ROWS, N_BRANCH * D_MODEL),
        'w_in': bf(w_in), 'w_gate': bf(w_gate), 'pool_w': bf(pool_w),
        'rnn_wax': bf(jnp.concatenate([rnn_wa, rnn_wx], axis=-1)),
        'w_br': bf(jnp.concatenate([w_br_pool, w_br_rnn, w_br_attn], axis=1)), 'w_out': bf(w_out),
        'ffn_vecs': _pack_rows([ffn_conv_w, ffn_conv_b, g_ffn], _FFN_VEC_ROWS, D_FF),
        'w_up': bf(w_up), 'w_down': bf(w_down),
    }
    g_fin = g_final[None, :]

    p_mk, p_mv, kb, vb = _mem_kv(mem_prompt, g_mem[:, None, :], bf(w_k), bf(w_v))
    cache_k = cache_mem_k.reshape(DEPTH, bs, n_mem * N_HEADS, HEAD_DIM)
    cache_v = cache_mem_v.reshape(DEPTH, bs, n_mem * N_HEADS, HEAD_DIM)

    time_major = lambda a: jnp.swapaxes(a, 0, 1)
    hp = x_prompt
    hs = time_major(x_sample)
    p_pool, p_rconv, p_rh, p_fconv = [], [], [], []
    s_pool, s_rconv, s_rh, s_fconv = [], [], [], []
    for l in range(DEPTH):
        last = l == DEPTH - 1
        hp, pool_st, conv_st, h_st = _mixer_prompt(hp, pw, kb, vb, l, bp)
        hp, ffn_st = _ffn_prompt(hp, pw, l, bp, g_fin if last else None)
        p_pool.append(time_major(pool_st.reshape(POOL_BUF + 1, bp, D_POOL)[1:]))
        p_rconv.append(time_major(conv_st.reshape(RNN_CONV - 1, bp, D_RNN)))
        p_rh.append(h_st)
        p_fconv.append(time_major(ffn_st.reshape(FFN_CONV - 1, bp, D_FF)))

        hs, pool_o, conv_o, h_o = _mixer_sample(
            hs, pw, l, cache_k, cache_v, time_major(state_pool[l]), time_major(state_rnn_conv[l]),
            state_rnn_h[l])
        hs, ffn_o = _ffn_sample(hs, pw, l, time_major(state_ffn_conv[l]), g_fin if last else None)
        s_pool.append(time_major(pool_o))
        s_rconv.append(time_major(conv_o))
        s_rh.append(h_o)
        s_fconv.append(time_major(ffn_o))

    head = (N_HEADS, HEAD_DIM)
    return (hp, time_major(hs),
            jnp.stack(p_pool), jnp.stack(p_rconv), jnp.stack(p_rh), jnp.stack(p_fconv),
            p_mk.reshape(DEPTH, bp, n_mem, *head), p_mv.reshape(DEPTH, bp, n_mem, *head),
            jnp.stack(s_pool), jnp.stack(s_rconv), jnp.stack(s_rh), jnp.stack(s_fconv))
```

```python
import functools

import jax
import jax.numpy as jnp
from jax import lax
from jax.experimental import pallas as pl
from jax.experimental.pallas import tpu as pltpu

D_MODEL = 1024
DEPTH = 2
PAST_LEN = 16384
POOL_WINDOWS = (2, 4, 8, 16)
D_POOL = D_MODEL // 2
POOL_GROUP = D_POOL // len(POOL_WINDOWS)
POOL_BUF = max(POOL_WINDOWS) - 1
D_RNN = D_MODEL
N_RNN_BLOCKS = 8
RNN_BLOCK = D_RNN // N_RNN_BLOCKS
RNN_CONV = 4
RG_C = 8.0
N_HEADS = 4
HEAD_DIM = 128
D_Q = N_HEADS * HEAD_DIM
N_BRANCH = 3
D_FF = 3 * D_MODEL
FFN_CONV = 3
EPS = 1e-6

_V7X_VMEM_BYTES = 64 * 1024 * 1024
_VMEM_LIMIT_BYTES = _V7X_VMEM_BYTES - 8 * 1024 * 1024
_VMEM_LIMIT_SAMPLE_MIXER_BYTES = _V7X_VMEM_BYTES - 3 * 1024 * 1024

_PROMPT_STEPS = 64
_FFN_PROMPT_STEPS = 128
_SAMPLE_SEQS = 64
_SAMPLE_MIXER_SEQS = 16
_FF_CHUNK = 512

_MXU_COLS = 256
_LANES = 128

_ROW_B_GATE, _ROW_G_MIX, _ROW_POOL_SCALE, _ROW_CONV_B, _ROW_B_A, _ROW_B_X, _ROW_LAMBDA, _ROW_CONV_W = range(8)
_MIXER_VEC_ROWS = 16
_FROW_CONV_W, _FROW_CONV_B, _FROW_G_FFN = 0, FFN_CONV, FFN_CONV + 1
_FFN_VEC_ROWS = 8

_F32 = jnp.float32
_BF16 = jnp.bfloat16


def _dot(a, b):
    return jnp.dot(a, b, preferred_element_type=_F32)


def _dot_nt(a, b):
    return lax.dot_general(a, b, (((1,), (1,)), ((), ())), preferred_element_type=_F32)


def _rmsnorm(x, g):
    return x * lax.rsqrt(jnp.mean(x * x, axis=-1, keepdims=True) + EPS) * g


def _sigmoid(x):
    return 0.5 * jnp.tanh(0.5 * x) + 0.5


def _softplus(x):
    return jnp.maximum(x, 0.0) + jnp.log1p(jnp.exp(-jnp.abs(x)))


def _softmax_rows(s):
    e = jnp.exp(s - jnp.max(s, axis=-1, keepdims=True))
    return e / jnp.sum(e, axis=-1, keepdims=True)


def _head(h):
    return slice(h * HEAD_DIM, (h + 1) * HEAD_DIM)


def _pool_window_sums(ext, rows, nb):
    sums = []
    for g, w in enumerate(POOL_WINDOWS):
        s = ext[:, g * POOL_GROUP:(g + 1) * POOL_GROUP]
        off, k = 0, 1
        while k < w:
            s = s[k * nb:] + s[:-k * nb]
            off += k * nb
            k *= 2
        start = (POOL_BUF + 1) * nb - off
        sums.append(s[start:start + rows])
    return sums


def _load_time_major(x_ref, scr, nb, nt):
    for b in range(nb):
        for c in range(scr.shape[0]):
            scr[c, pl.ds(b, nt, stride=nb), :] = x_ref[b, :, c * _LANES:(c + 1) * _LANES]
    return jnp.concatenate([scr[c] for c in range(scr.shape[0])], axis=1)


def _store_from_time_major(y, y_ref, scr, nb, nt):
    for c in range(scr.shape[0]):
        scr[c] = y[:, c * _LANES:(c + 1) * _LANES]
    for b in range(nb):
        for c in range(scr.shape[0]):
            y_ref[b, :, c * _LANES:(c + 1) * _LANES] = scr[c, pl.ds(b, nt, stride=nb), :]


def _scores_per_head(q_rows, k_ref, nb):
    return jnp.concatenate(
        [_dot(q_rows(b, h).astype(_BF16), k_ref[b, h]) for b in range(nb) for h in range(N_HEADS)], axis=0)


def _values_per_head(p, v_ref, nb, nt, put):
    for b in range(nb):
        for h in range(N_HEADS):
            n = b * N_HEADS + h
            put(b, h, _dot(p[n * nt:(n + 1) * nt].astype(_BF16), v_ref[b, :, _head(h)]))


def _scores_head_rows(q_rows, k_ref, nb, nt):
    s = jnp.concatenate(
        [_dot_nt(jnp.concatenate([q_rows(b, h) for h in range(N_HEADS)], axis=0).astype(_BF16),
                 k_ref[b].astype(_BF16)) for b in range(nb)], axis=0)
    r = lax.broadcasted_iota(jnp.int32, s.shape, 0)
    c = lax.broadcasted_iota(jnp.int32, s.shape, 1)
    same_head = ((r >> (nt.bit_length() - 1)) & (N_HEADS - 1)) == (c & (N_HEADS - 1))
    return jnp.where(same_head, s, -jnp.inf)


def _values_head_rows(p, v_ref, nb, nt, put):
    hq = N_HEADS * nt
    for b in range(nb):
        o = _dot(p[b * hq:(b + 1) * hq].astype(_BF16), v_ref[b].astype(_BF16))
        for h in range(N_HEADS):
            put(b, h, o[h * nt:(h + 1) * nt])


def _mixer_tile(x, pool_hist, conv_hist, h0, weights, nb, nt, first_pos, k_ref, v_ref, qo_scr, head_rows):
    rows = nb * nt
    vecs, w_in, w_gate, pool_w, w_ax, w_br, w_out = weights

    def vec(r, cols=slice(0, D_MODEL)):
        return vecs[r:r + 1, cols]

    n_gate_chunks = N_BRANCH * D_MODEL // _MXU_COLS
    gate = [None] * n_gate_chunks
    xn = _rmsnorm(x, vec(_ROW_G_MIX)).astype(_BF16)

    def gate_chunk(j):
        sl = slice(j * _MXU_COLS, (j + 1) * _MXU_COLS)
        return _sigmoid(_dot(xn, w_gate[:, sl]) + vec(_ROW_B_GATE, sl))

    u_rnn = _dot(xn, w_in[:, D_POOL:D_POOL + D_RNN])
    u_pool = _dot(xn, w_in[:, 0:D_POOL])
    q = _dot(xn, w_in[:, D_POOL + D_RNN:])

    ext_r = jnp.concatenate([conv_hist, u_rnn], axis=0)
    xc = vec(_ROW_CONV_B)
    for j in range(RNN_CONV):
        xc = xc + ext_r[j * nb:j * nb + rows] * vec(_ROW_CONV_W + j)
    new_conv_hist = ext_r[rows:]
    xcb = xc.astype(_BF16)
    log_a_unit = -RG_C * _softplus(-vec(_ROW_LAMBDA))
    cols, h_last = [], []
    for n in range(N_RNN_BLOCKS):
        sl = slice(n * RNN_BLOCK, (n + 1) * RNN_BLOCK)
        ri = _dot(xcb[:, sl], w_ax[n])
        r = _sigmoid(ri[:, :RNN_BLOCK] + vec(_ROW_B_A, sl))
        i = _sigmoid(ri[:, RNN_BLOCK:] + vec(_ROW_B_X, sl))
        log_a = r * log_a_unit[:, sl]
        a = jnp.exp(log_a)
        th = jnp.tanh(log_a)
        b = jnp.sqrt(-2.0 * th / (1.0 - th)) * (i * xc[:, sl])
        h = h0[:, sl]
        hs = []
        for t in range(nt):
            h = a[t * nb:(t + 1) * nb] * h + b[t * nb:(t + 1) * nb]
            hs.append(h)
        cols.append(jnp.concatenate(hs, axis=0))
        h_last.append(h)
        for j in range(n * n_gate_chunks // N_RNN_BLOCKS, (n + 1) * n_gate_chunks // N_RNN_BLOCKS):
            gate[j] = gate_chunk(j)
    y_rnn = jnp.concatenate(cols, axis=1).astype(_BF16)
    new_h = jnp.concatenate(h_last, axis=1)

    for h in range(N_HEADS):
        qo_scr[0, h] = q[:, _head(h)]

    def q_rows(b, h):
        return qo_scr[0, h, pl.ds(b, nt, stride=nb), :]

    def put_rows(b, h, o):
        qo_scr[1, h, pl.ds(b, nt, stride=nb), :] = o

    s = _scores_head_rows(q_rows, k_ref, nb, nt) if head_rows else _scores_per_head(q_rows, k_ref, nb)
    rnn_branch = _dot(y_rnn, w_br[D_POOL:D_POOL + D_RNN, :])
    p = _softmax_rows(s * (HEAD_DIM ** -0.5))
    (_values_head_rows if head_rows else _values_per_head)(p, v_ref, nb, nt, put_rows)

    ext = jnp.concatenate([pool_hist, u_pool], axis=0)
    wins = _pool_window_sums(ext, rows, nb)
    if first_pos is not None:
        row = lax.broadcasted_iota(jnp.int32, (rows, POOL_GROUP), 0)
        pos1 = first_pos + (row >> (nb.bit_length() - 1)) + 1
    ys = []
    for g, w in enumerate(POOL_WINDOWS):
        if first_pos is not None:
            mean = wins[g] / jnp.minimum(pos1, w).astype(_F32)
        else:
            mean = wins[g] * (1.0 / w)
        d = mean - u_pool[:, g * POOL_GROUP:(g + 1) * POOL_GROUP]
        ys.append(_dot(d.astype(_BF16), pool_w[g]))
    y_pool = (jnp.concatenate(ys, axis=1) * vec(_ROW_POOL_SCALE, slice(0, D_POOL))).astype(_BF16)
    new_pool_hist = ext[rows:]
    pool_branch = _dot(y_pool, w_br[0:D_POOL, :])

    y_attn = jnp.concatenate([qo_scr[1, h] for h in range(N_HEADS)], axis=1).astype(_BF16)

    out = x
    chunks_per_branch = D_MODEL // _MXU_COLS
    for c in range(chunks_per_branch):
        sl = slice(c * _MXU_COLS, (c + 1) * _MXU_COLS)
        branches = (pool_branch[:, sl], rnn_branch[:, sl], _dot(y_attn, w_br[D_POOL + D_RNN:, sl]))
        merged = None
        for k, branch in enumerate(branches):
            term = gate[k * chunks_per_branch + c] * branch
            merged = term if merged is None else merged + term
        out = out + _dot(merged.astype(_BF16), w_out[sl, :])
    return out, new_pool_hist, new_conv_hist, new_h


def _mixer_kernel(*refs, nb, nt, carried, pos0, natural_in=False):
    rows = nb * nt
    x_ref = refs[0]
    weights = refs[1:8]
    if carried:
        k_ref, v_ref, out_ref, pool_st, conv_st, h_st, qo_scr = refs[8:15]
        step = pl.program_id(0)

        @pl.when(step == 0)
        def _():
            pool_st[...] = jnp.zeros_like(pool_st)
            conv_st[...] = jnp.zeros_like(conv_st)
            h_st[...] = jnp.zeros_like(h_st)

        x = _load_time_major(x_ref, refs[15], nb, nt) if natural_in else x_ref[...]
        first_pos = pos0 + step * nt if pos0 < POOL_BUF else None
        out, pool_hist, conv_hist, h = _mixer_tile(
            x, pool_st[...], conv_st[...], h_st[...], weights, nb, nt,
            first_pos, k_ref, v_ref, qo_scr, head_rows=False)
        out_ref[...] = out
        pool_st[...] = pool_hist
        conv_st[...] = conv_hist
        h_st[...] = h
    else:
        assert pos0 >= POOL_BUF
        (k_ref, v_ref, pool_in, conv_in, h_in, out_ref, pool_out, conv_out, h_out, qo_scr) = refs[8:]
        x = x_ref[...].reshape(rows, D_MODEL)
        pool_hist = jnp.concatenate(
            [jnp.zeros((nb, D_POOL), _F32), pool_in[...].reshape(POOL_BUF * nb, D_POOL)], axis=0)
        out, pool_hist, conv_hist, h = _mixer_tile(
            x, pool_hist,
            conv_in[...].reshape((RNN_CONV - 1) * nb, D_RNN), h_in[...], weights, nb, nt,
            None, k_ref, v_ref, qo_scr, head_rows=True)
        out_ref[...] = out.reshape(nt, nb, D_MODEL)
        pool_out[...] = pool_hist[nb:].reshape(POOL_BUF, nb, D_POOL)
        conv_out[...] = conv_hist.reshape(RNN_CONV - 1, nb, D_RNN)
        h_out[...] = h


def _ffn_tile(x, read_hist, write_hist, weights, nb, nt, act_scr):
    rows = nb * nt
    vecs, w_up, w_down = weights
    xn = _rmsnorm(x, vecs[_FROW_G_FFN:_FROW_G_FFN + 1, 0:D_MODEL]).astype(_BF16)
    for c in range(0, D_FF, _FF_CHUNK):
        sl = slice(c, c + _FF_CHUNK)
        g_pre = _dot(xn, w_up[:, sl])
        val = _dot(xn, w_up[:, D_FF + c:D_FF + c + _FF_CHUNK])
        ext = jnp.concatenate([read_hist(sl), g_pre], axis=0)
        g_conv = vecs[_FROW_CONV_B:_FROW_CONV_B + 1, sl]
        for j in range(FFN_CONV):
            g_conv = g_conv + ext[j * nb:j * nb + rows] * vecs[_FROW_CONV_W + j:_FROW_CONV_W + j + 1, sl]
        write_hist(sl, ext[rows:])
        act_scr[:, sl] = (jax.nn.gelu(g_conv) * val).astype(_BF16)
    return x + _dot(act_scr[...], w_down[...])


def _ffn_kernel(*refs, nb, nt, carried, final_norm):
    rows = nb * nt
    x_ref = refs[0]
    weights = refs[1:4]
    rest = list(refs[4:])
    g_final = rest.pop(0) if final_norm else None
    finish = (lambda y: _rmsnorm(y, g_final[...])) if final_norm else (lambda y: y)
    if carried:
        out_ref, st_out, act_scr = rest[:3]

        @pl.when(pl.program_id(0) == 0)
        def _():
            st_out[...] = jnp.zeros_like(st_out)

        def write_hist(sl, hist):
            st_out[:, sl] = hist

        x = x_ref[...]
        out = finish(_ffn_tile(x, lambda sl: st_out[:, sl], write_hist,
                               weights, nb, nt, act_scr))
        if final_norm:
            _store_from_time_major(out, out_ref, rest[3], nb, nt)
        else:
            out_ref[...] = out
    else:
        st_in, out_ref, st_out, act_scr = rest
        x = x_ref[...].reshape(rows, D_MODEL)

        def write_hist(sl, hist):
            st_out[:, :, sl] = hist.reshape(FFN_CONV - 1, nb, _FF_CHUNK)

        out = finish(_ffn_tile(
            x,
            lambda sl: st_in[:, :, sl].reshape((FFN_CONV - 1) * nb, _FF_CHUNK), write_hist,
            weights, nb, nt, act_scr))
        out_ref[...] = out.reshape(nt, nb, D_MODEL)


def _mem_kv_kernel(mem_ref, g_ref, wk_ref, wv_ref, k_ref, v_ref, kb_ref, vb_ref):
    n_mem = mem_ref.shape[1]
    for b in range(mem_ref.shape[0]):
        mn = _rmsnorm(mem_ref[b], g_ref[...]).astype(_BF16)
        for w_ref, out_ref, out_bf_ref in ((wk_ref, k_ref, kb_ref), (wv_ref, v_ref, vb_ref)):
            kv = _dot(mn, w_ref[...])
            for h in range(N_HEADS):
                out_ref[b, pl.ds(h, n_mem, stride=N_HEADS), :] = kv[:, _head(h)]
            if out_bf_ref is kb_ref:
                for h in range(N_HEADS):
                    kb_ref[b, h] = kv[:, _head(h)].T.astype(_BF16)
            else:
                out_bf_ref[b] = kv.astype(_BF16)


def _resident(shape):
    return pl.BlockSpec(shape, lambda *_: (0,) * len(shape), pipeline_mode=pl.Buffered(1))


def _params(semantics, vmem_limit_bytes=_VMEM_LIMIT_BYTES):
    return pltpu.CompilerParams(dimension_semantics=semantics, vmem_limit_bytes=vmem_limit_bytes)


def _layer_resident(a, layer):
    return pl.BlockSpec((None,) + a.shape[1:], lambda *_: (layer,) + (0,) * (a.ndim - 1),
                        pipeline_mode=pl.Buffered(1))


def _pack_rows(arrays, n_rows, width):
    rows = [a[:, None, :] if a.ndim == 2 else a for a in arrays]
    rows = [jnp.pad(r, ((0, 0), (0, 0), (0, width - r.shape[2]))) for r in rows]
    packed = jnp.concatenate(rows, axis=1)
    return jnp.pad(packed, ((0, 0), (0, n_rows - packed.shape[1]), (0, 0)))


def _mixer_weights(pw, layer):
    arrs = [pw[n] for n in ('mixer_vecs', 'w_in', 'w_gate', 'pool_w', 'rnn_wax', 'w_br', 'w_out')]
    return arrs, [_layer_resident(a, layer) for a in arrs]


def _mixer_prompt(h, pw, kb, vb, layer, n_seq):
    nt, nb = _PROMPT_STEPS, n_seq
    rows = nt * nb
    natural_in = h.ndim == 3
    n_tok = h.shape[0] * h.shape[1] if natural_in else h.shape[0]
    w_arrs, w_specs = _mixer_weights(pw, layer)
    tile = pl.BlockSpec((rows, D_MODEL), lambda j: (j, 0))
    in_tile = pl.BlockSpec((nb, nt, D_MODEL), lambda j: (0, j, 0)) if natural_in else tile
    scratch = [pltpu.VMEM((2, N_HEADS, rows, HEAD_DIM), _F32)]
    if natural_in:
        scratch.append(pltpu.VMEM((D_MODEL // _LANES, rows, _LANES), _F32))
    out_shapes = (jax.ShapeDtypeStruct((n_tok, D_MODEL), _F32),
                  jax.ShapeDtypeStruct(((POOL_BUF + 1) * nb, D_POOL), _F32),
                  jax.ShapeDtypeStruct(((RNN_CONV - 1) * nb, D_RNN), _F32),
                  jax.ShapeDtypeStruct((nb, D_RNN), _F32))
    whole = lambda s: pl.BlockSpec(s.shape, lambda j: (0,) * len(s.shape))
    return pl.pallas_call(
        functools.partial(_mixer_kernel, nb=nb, nt=nt, carried=True, pos0=0, natural_in=natural_in),
        grid=(n_tok // rows,),
        in_specs=[in_tile] + w_specs + [_layer_resident(kb, layer), _layer_resident(vb, layer)],
        out_specs=(tile,) + tuple(whole(s) for s in out_shapes[1:]),
        out_shape=out_shapes,
        scratch_shapes=scratch,
        compiler_params=_params(("arbitrary",)),
        name="mixer_prompt",
    )(h, *w_arrs, kb, vb)


def _mixer_sample(h, pw, layer, k, v, pool_st, conv_st, h_st):
    nt, n_seq = h.shape[0], h.shape[1]
    nb = _SAMPLE_MIXER_SEQS
    w_arrs, w_specs = _mixer_weights(pw, layer)
    seq3 = lambda a: pl.BlockSpec((a.shape[0], nb, a.shape[2]), lambda i: (0, i, 0))
    h_spec = pl.BlockSpec((nb, D_RNN), lambda i: (i, 0))
    kv_spec = pl.BlockSpec((None, nb) + k.shape[2:], lambda i: (layer, i, 0, 0))
    outs = (h, pool_st, conv_st, h_st)
    return pl.pallas_call(
        functools.partial(_mixer_kernel, nb=nb, nt=nt, carried=False, pos0=PAST_LEN),
        grid=(n_seq // nb,),
        in_specs=[seq3(h)] + w_specs + [kv_spec, kv_spec, seq3(pool_st), seq3(conv_st), h_spec],
        out_specs=(seq3(h), seq3(pool_st), seq3(conv_st), h_spec),
        out_shape=tuple(jax.ShapeDtypeStruct(a.shape, _F32) for a in outs),
        scratch_shapes=[pltpu.VMEM((2, N_HEADS, nt * nb, HEAD_DIM), _F32)],
        compiler_params=_params(("arbitrary",), _VMEM_LIMIT_SAMPLE_MIXER_BYTES),
        name="mixer_sample",
    )(h, *w_arrs, k, v, pool_st, conv_st, h_st)


def _ffn_weights(pw, layer, g_final):
    arrs = [pw[n] for n in ('ffn_vecs', 'w_up', 'w_down')]
    specs = [_layer_resident(a, layer) for a in arrs]
    if g_final is not None:
        arrs.append(g_final)
        specs.append(_resident(g_final.shape))
    return arrs, specs


def _ffn_prompt(h, pw, layer, n_seq, g_final):
    nt, nb = _FFN_PROMPT_STEPS, n_seq
    rows = nt * nb
    w_arrs, w_specs = _ffn_weights(pw, layer, g_final)
    tile = pl.BlockSpec((rows, D_MODEL), lambda j: (j, 0))
    st_shape = ((FFN_CONV - 1) * nb, D_FF)
    scratch = [pltpu.VMEM((rows, D_FF), _BF16)]
    if g_final is None:
        out_tile, out_shape = tile, h.shape
    else:
        out_tile = pl.BlockSpec((nb, nt, D_MODEL), lambda j: (0, j, 0))
        out_shape = (nb, h.shape[0] // nb, D_MODEL)
        scratch.append(pltpu.VMEM((D_MODEL // _LANES, rows, _LANES), _F32))
    return pl.pallas_call(
        functools.partial(_ffn_kernel, nb=nb, nt=nt, carried=True, final_norm=g_final is not None),
        grid=(h.shape[0] // rows,),
        in_specs=[tile] + w_specs,
        out_specs=(out_tile, pl.BlockSpec(st_shape, lambda j: (0, 0))),
        out_shape=(jax.ShapeDtypeStruct(out_shape, _F32), jax.ShapeDtypeStruct(st_shape, _F32)),
        scratch_shapes=scratch,
        compiler_params=_params(("arbitrary",)),
        name="ffn_prompt",
    )(h, *w_arrs)


def _ffn_sample(h, pw, layer, st, g_final):
    nt, n_seq = h.shape[0], h.shape[1]
    nb = _SAMPLE_SEQS
    w_arrs, w_specs = _ffn_weights(pw, layer, g_final)
    seq3 = lambda a: pl.BlockSpec((a.shape[0], nb, a.shape[2]), lambda i: (0, i, 0))
    return pl.pallas_call(
        functools.partial(_ffn_kernel, nb=nb, nt=nt, carried=False, final_norm=g_final is not None),
        grid=(n_seq // nb,),
        in_specs=[seq3(h)] + w_specs + [seq3(st)],
        out_specs=(seq3(h), seq3(st)),
        out_shape=(jax.ShapeDtypeStruct(h.shape, _F32), jax.ShapeDtypeStruct(st.shape, _F32)),
        scratch_shapes=[pltpu.VMEM((nt * nb, D_FF), _BF16)],
        compiler_params=_params(("arbitrary",)),
        name="ffn_sample",
    )(h, *w_arrs, st)


def _mem_kv(mem, g_mem, w_k, w_v):
    n_seq, n_mem = mem.shape[0], mem.shape[1]
    per_layer = lambda a: pl.BlockSpec((None,) + a.shape[1:], lambda l: (l,) + (0,) * (a.ndim - 1))
    out_shapes = ((DEPTH, n_seq, n_mem * N_HEADS, HEAD_DIM),) * 2 + (
        (DEPTH, n_seq, N_HEADS, HEAD_DIM, n_mem), (DEPTH, n_seq, n_mem, D_Q))
    dtypes = (_F32, _F32, _BF16, _BF16)
    return pl.pallas_call(
        _mem_kv_kernel,
        grid=(DEPTH,),
        in_specs=[pl.BlockSpec(mem.shape, lambda l: (0, 0, 0)), per_layer(g_mem), per_layer(w_k), per_layer(w_v)],
        out_specs=tuple(pl.BlockSpec((None,) + s[1:], lambda l, nd=len(s): (l,) + (0,) * (nd - 1))
                        for s in out_shapes),
        out_shape=tuple(jax.ShapeDtypeStruct(s, d) for s, d in zip(out_shapes, dtypes)),
        compiler_params=_params(("arbitrary",)),
        name="mem_kv",
    )(mem, g_mem, w_k, w_v)


def kernel(x_prompt, x_sample, mem_prompt, cache_mem_k, cache_mem_v, state_pool, state_rnn_conv, state_rnn_h, state_ffn_conv, g_mix, w_in, w_gate, b_gate, pool_w, pool_scale, rnn_conv_w, rnn_conv_b, rnn_wa, rnn_ba, rnn_wx, rnn_bx, rnn_lambda, g_mem, w_k, w_v, w_br_pool, w_br_rnn, w_br_attn, w_out, g_ffn, w_up, ffn_conv_w, ffn_conv_b, w_down, g_final):
    bp = x_prompt.shape[0]
    bs, ts, _ = x_sample.shape
    n_mem = mem_prompt.shape[1]
    bf = lambda a: a.astype(_BF16)
    pw = {
        'mixer_vecs': _pack_rows([b_gate, g_mix, pool_scale, rnn_conv_b, rnn_ba, rnn_bx, rnn_lambda, rnn_conv_w],
                                 _MIXER_VEC_ROWS, N_BRANCH * D_MODEL),
        'w_in': bf(w_in), 'w_gate': bf(w_gate), 'pool_w': bf(pool_w),
        'rnn_wax': bf(jnp.concatenate([rnn_wa, rnn_wx], axis=-1)),
        'w_br': bf(jnp.concatenate([w_br_pool, w_br_rnn, w_br_attn], axis=1)), 'w_out': bf(w_out),
        'ffn_vecs': _pack_rows([ffn_conv_w, ffn_conv_b, g_ffn], _FFN_VEC_ROWS, D_FF),
        'w_up': bf(w_up), 'w_down': bf(w_down),
    }
    g_fin = g_final[None, :]

    p_mk, p_mv, kb, vb = _mem_kv(mem_prompt, g_mem[:, None, :], bf(w_k), bf(w_v))
    cache_k = cache_mem_k.reshape(DEPTH, bs, n_mem * N_HEADS, HEAD_DIM)
    cache_v = cache_mem_v.reshape(DEPTH, bs, n_mem * N_HEADS, HEAD_DIM)

    time_major = lambda a: jnp.swapaxes(a, 0, 1)
    hp = x_prompt
    hs = time_major(x_sample)
    p_pool, p_rconv, p_rh, p_fconv = [], [], [], []
    s_pool, s_rconv, s_rh, s_fconv = [], [], [], []
    for l in range(DEPTH):
        last = l == DEPTH - 1
        hp, pool_st, conv_st, h_st = _mixer_prompt(hp, pw, kb, vb, l, bp)
        hp, ffn_st = _ffn_prompt(hp, pw, l, bp, g_fin if last else None)
        p_pool.append(time_major(pool_st.reshape(POOL_BUF + 1, bp, D_POOL)[1:]))
        p_rconv.append(time_major(conv_st.reshape(RNN_CONV - 1, bp, D_RNN)))
        p_rh.append(h_st)
        p_fconv.append(time_major(ffn_st.reshape(FFN_CONV - 1, bp, D_FF)))

        hs, pool_o, conv_o, h_o = _mixer_sample(
            hs, pw, l, cache_k, cache_v, time_major(state_pool[l]), time_major(state_rnn_conv[l]),
            state_rnn_h[l])
        hs, ffn_o = _ffn_sample(hs, pw, l, time_major(state_ffn_conv[l]), g_fin if last else None)
        s_pool.append(time_major(pool_o))
        s_rconv.append(time_major(conv_o))
        s_rh.append(h_o)
        s_fconv.append(time_major(ffn_o))

    head = (N_HEADS, HEAD_DIM)
    return (hp, time_major(hs),
            jnp.stack(p_pool), jnp.stack(p_rconv), jnp.stack(p_rh), jnp.stack(p_fconv),
            p_mk.reshape(DEPTH, bp, n_mem, *head), p_mv.reshape(DEPTH, bp, n_mem, *head),
            jnp.stack(s_pool), jnp.stack(s_rconv), jnp.stack(s_rh), jnp.stack(s_fconv))
```

```python
import functools

import jax
import jax.numpy as jnp
from jax import lax
from jax.experimental import pallas as pl
from jax.experimental.pallas import tpu as pltpu

D_MODEL = 1024
DEPTH = 2
PAST_LEN = 16384
POOL_WINDOWS = (2, 4, 8, 16)
D_POOL = D_MODEL // 2
POOL_GROUP = D_POOL // len(POOL_WINDOWS)
POOL_BUF = max(POOL_WINDOWS) - 1
D_RNN = D_MODEL
N_RNN_BLOCKS = 8
RNN_BLOCK = D_RNN // N_RNN_BLOCKS
RNN_CONV = 4
RG_C = 8.0
N_HEADS = 4
HEAD_DIM = 128
D_Q = N_HEADS * HEAD_DIM
N_BRANCH = 3
D_FF = 3 * D_MODEL
FFN_CONV = 3
EPS = 1e-6

_V7X_VMEM_BYTES = 64 * 1024 * 1024
_VMEM_LIMIT_BYTES = _V7X_VMEM_BYTES - 8 * 1024 * 1024
_VMEM_LIMIT_SAMPLE_MIXER_BYTES = _V7X_VMEM_BYTES - 3 * 1024 * 1024

_PROMPT_STEPS = 64
_FFN_PROMPT_STEPS = 128
_SAMPLE_SEQS = 64
_SAMPLE_MIXER_SEQS = 16
_FF_CHUNK = 512

_MXU_COLS = 256
_LANES = 128

_ROW_B_GATE, _ROW_G_MIX, _ROW_POOL_SCALE, _ROW_CONV_B, _ROW_B_A, _ROW_B_X, _ROW_LAMBDA, _ROW_CONV_W = range(8)
_MIXER_VEC_ROWS = 16
_FROW_CONV_W, _FROW_CONV_B, _FROW_G_FFN = 0, FFN_CONV, FFN_CONV + 1
_FFN_VEC_ROWS = 8

_F32 = jnp.float32
_BF16 = jnp.bfloat16


def _dot(a, b):
    return jnp.dot(a, b, preferred_element_type=_F32)


def _dot_nt(a, b):
    return lax.dot_general(a, b, (((1,), (1,)), ((), ())), preferred_element_type=_F32)


def _rmsnorm(x, g):
    return x * lax.rsqrt(jnp.mean(x * x, axis=-1, keepdims=True) + EPS) * g


def _sigmoid(x):
    return 0.5 * jnp.tanh(0.5 * x) + 0.5


def _softplus(x):
    return jnp.maximum(x, 0.0) + jnp.log1p(jnp.exp(-jnp.abs(x)))


def _softmax_rows(s):
    e = jnp.exp(s - jnp.max(s, axis=-1, keepdims=True))
    return e / jnp.sum(e, axis=-1, keepdims=True)


def _head(h):
    return slice(h * HEAD_DIM, (h + 1) * HEAD_DIM)


def _pool_window_sums(ext, rows, nb):
    sums = []
    for g, w in enumerate(POOL_WINDOWS):
        s = ext[:, g * POOL_GROUP:(g + 1) * POOL_GROUP]
        off, k = 0, 1
        while k < w:
            s = s[k * nb:] + s[:-k * nb]
            off += k * nb
            k *= 2
        start = (POOL_BUF + 1) * nb - off
        sums.append(s[start:start + rows])
    return sums


def _load_time_major(x_ref, scr, nb, nt):
    for b in range(nb):
        for c in range(scr.shape[0]):
            scr[c, pl.ds(b, nt, stride=nb), :] = x_ref[b, :, c * _LANES:(c + 1) * _LANES]
    return jnp.concatenate([scr[c] for c in range(scr.shape[0])], axis=1)


def _store_from_time_major(y, y_ref, scr, nb, nt):
    for c in range(scr.shape[0]):
        scr[c] = y[:, c * _LANES:(c + 1) * _LANES]
    for b in range(nb):
        for c in range(scr.shape[0]):
            y_ref[b, :, c * _LANES:(c + 1) * _LANES] = scr[c, pl.ds(b, nt, stride=nb), :]


def _scores_per_head(q_rows, k_ref, nb):
    return jnp.concatenate(
        [_dot(q_rows(b, h).astype(_BF16), k_ref[b, h]) for b in range(nb) for h in range(N_HEADS)], axis=0)


def _values_per_head(p, v_ref, nb, nt, put):
    for b in range(nb):
        for h in range(N_HEADS):
            n = b * N_HEADS + h
            put(b, h, _dot(p[n * nt:(n + 1) * nt].astype(_BF16), v_ref[b, :, _head(h)]))


def _scores_head_rows(q_rows, k_ref, nb, nt):
    s = jnp.concatenate(
        [_dot_nt(jnp.concatenate([q_rows(b, h) for h in range(N_HEADS)], axis=0).astype(_BF16),
                 k_ref[b].astype(_BF16)) for b in range(nb)], axis=0)
    r = lax.broadcasted_iota(jnp.int32, s.shape, 0)
    c = lax.broadcasted_iota(jnp.int32, s.shape, 1)
    same_head = ((r >> (nt.bit_length() - 1)) & (N_HEADS - 1)) == (c & (N_HEADS - 1))
    return jnp.where(same_head, s, -jnp.inf)


def _values_head_rows(p, v_ref, nb, nt, put):
    hq = N_HEADS * nt
    for b in range(nb):
        o = _dot(p[b * hq:(b + 1) * hq].astype(_BF16), v_ref[b].astype(_BF16))
        for h in range(N_HEADS):
            put(b, h, o[h * nt:(h + 1) * nt])


def _mixer_tile(x, pool_hist, conv_hist, h0, weights, nb, nt, first_pos, k_ref, v_ref, qo_scr, head_rows):
    rows = nb * nt
    vecs, w_in, w_gate, pool_w, w_ax, w_br, w_out = weights

    def vec(r, cols=slice(0, D_MODEL)):
        return vecs[r:r + 1, cols]

    n_gate_chunks = N_BRANCH * D_MODEL // _MXU_COLS
    gate = [None] * n_gate_chunks
    xn = _rmsnorm(x, vec(_ROW_G_MIX)).astype(_BF16)

    def gate_chunk(j):
        sl = slice(j * _MXU_COLS, (j + 1) * _MXU_COLS)
        return _sigmoid(_dot(xn, w_gate[:, sl]) + vec(_ROW_B_GATE, sl))

    u_rnn = _dot(xn, w_in[:, D_POOL:D_POOL + D_RNN])
    u_pool = _dot(xn, w_in[:, 0:D_POOL])
    q = _dot(xn, w_in[:, D_POOL + D_RNN:])

    ext_r = jnp.concatenate([conv_hist, u_rnn], axis=0)
    xc = vec(_ROW_CONV_B)
    for j in range(RNN_CONV):
        xc = xc + ext_r[j * nb:j * nb + rows] * vec(_ROW_CONV_W + j)
    new_conv_hist = ext_r[rows:]
    xcb = xc.astype(_BF16)
    log_a_unit = -RG_C * _softplus(-vec(_ROW_LAMBDA))
    cols, h_last = [], []
    for n in range(N_RNN_BLOCKS):
        sl = slice(n * RNN_BLOCK, (n + 1) * RNN_BLOCK)
        ri = _dot(xcb[:, sl], w_ax[n])
        r = _sigmoid(ri[:, :RNN_BLOCK] + vec(_ROW_B_A, sl))
        i = _sigmoid(ri[:, RNN_BLOCK:] + vec(_ROW_B_X, sl))
        log_a = r * log_a_unit[:, sl]
        a = jnp.exp(log_a)
        th = jnp.tanh(log_a)
        b = jnp.sqrt(-2.0 * th / (1.0 - th)) * (i * xc[:, sl])
        h = h0[:, sl]
        hs = []
        for t in range(nt):
            h = a[t * nb:(t + 1) * nb] * h + b[t * nb:(t + 1) * nb]
            hs.append(h)
        cols.append(jnp.concatenate(hs, axis=0))
        h_last.append(h)
        for j in range(n * n_gate_chunks // N_RNN_BLOCKS, (n + 1) * n_gate_chunks // N_RNN_BLOCKS):
            gate[j] = gate_chunk(j)
    y_rnn = jnp.concatenate(cols, axis=1).astype(_BF16)
    new_h = jnp.concatenate(h_last, axis=1)

    for h in range(N_HEADS):
        qo_scr[0, h] = q[:, _head(h)]

    def q_rows(b, h):
        return qo_scr[0, h, pl.ds(b, nt, stride=nb), :]

    def put_rows(b, h, o):
        qo_scr[1, h, pl.ds(b, nt, stride=nb), :] = o

    s = _scores_head_rows(q_rows, k_ref, nb, nt) if head_rows else _scores_per_head(q_rows, k_ref, nb)
    rnn_branch = _dot(y_rnn, w_br[D_POOL:D_POOL + D_RNN, :])
    p = _softmax_rows(s * (HEAD_DIM ** -0.5))
    (_values_head_rows if head_rows else _values_per_head)(p, v_ref, nb, nt, put_rows)

    ext = jnp.concatenate([pool_hist, u_pool], axis=0)
    wins = _pool_window_sums(ext, rows, nb)
    if first_pos is not None:
        row = lax.broadcasted_iota(jnp.int32, (rows, POOL_GROUP), 0)
        pos1 = first_pos + (row >> (nb.bit_length() - 1)) + 1
        inv_pos1 = 1.0 / pos1.astype(_F32)
    ys = []
    for g, w in enumerate(POOL_WINDOWS):
        if first_pos is not None:
            mean = wins[g] * jnp.maximum(inv_pos1, 1.0 / w)
        else:
            mean = wins[g] * (1.0 / w)
        d = mean - u_pool[:, g * POOL_GROUP:(g + 1) * POOL_GROUP]
        ys.append(_dot(d.astype(_BF16), pool_w[g]))
    y_pool = (jnp.concatenate(ys, axis=1) * vec(_ROW_POOL_SCALE, slice(0, D_POOL))).astype(_BF16)
    new_pool_hist = ext[rows:]
    pool_branch = _dot(y_pool, w_br[0:D_POOL, :])

    y_attn = jnp.concatenate([qo_scr[1, h] for h in range(N_HEADS)], axis=1).astype(_BF16)

    out = x
    chunks_per_branch = D_MODEL // _MXU_COLS
    for c in range(chunks_per_branch):
        sl = slice(c * _MXU_COLS, (c + 1) * _MXU_COLS)
        branches = (pool_branch[:, sl], rnn_branch[:, sl], _dot(y_attn, w_br[D_POOL + D_RNN:, sl]))
        merged = None
        for k, branch in enumerate(branches):
            term = gate[k * chunks_per_branch + c] * branch
            merged = term if merged is None else merged + term
        out = out + _dot(merged.astype(_BF16), w_out[sl, :])
    return out, new_pool_hist, new_conv_hist, new_h


def _round_slices(src_refs, dst_refs):
    for src, dst in zip(src_refs, dst_refs):
        dst[...] = src[...].astype(_BF16)


def _mixer_kernel(*refs, nb, nt, carried, pos0, natural_in=False, n_round=0):
    rows = nb * nt
    x_ref = refs[0]
    weights = refs[1:8]
    if carried:
        k_ref, v_ref = refs[8:10]
        out_ref, pool_st, conv_st, h_st = refs[10 + n_round:14 + n_round]
        _round_slices(refs[10:10 + n_round], refs[14 + n_round:14 + 2 * n_round])
        refs = refs[14 + 2 * n_round:]
        qo_scr = refs[0]
        step = pl.program_id(0)

        @pl.when(step == 0)
        def _():
            pool_st[...] = jnp.zeros_like(pool_st)
            conv_st[...] = jnp.zeros_like(conv_st)
            h_st[...] = jnp.zeros_like(h_st)

        x = _load_time_major(x_ref, refs[1], nb, nt) if natural_in else x_ref[...]
        first_pos = pos0 + step * nt if pos0 < POOL_BUF else None
        out, pool_hist, conv_hist, h = _mixer_tile(
            x, pool_st[...], conv_st[...], h_st[...], weights, nb, nt,
            first_pos, k_ref, v_ref, qo_scr, head_rows=False)
        out_ref[...] = out
        pool_st[...] = pool_hist
        conv_st[...] = conv_hist
        h_st[...] = h
    else:
        assert pos0 >= POOL_BUF
        (k_ref, v_ref, pool_in, conv_in, h_in, out_ref, pool_out, conv_out, h_out, qo_scr) = refs[8:]
        x = x_ref[...].reshape(rows, D_MODEL)
        pool_hist = jnp.concatenate(
            [jnp.zeros((nb, D_POOL), _F32), pool_in[...].reshape(POOL_BUF * nb, D_POOL)], axis=0)
        out, pool_hist, conv_hist, h = _mixer_tile(
            x, pool_hist,
            conv_in[...].reshape((RNN_CONV - 1) * nb, D_RNN), h_in[...], weights, nb, nt,
            None, k_ref, v_ref, qo_scr, head_rows=True)
        out_ref[...] = out.reshape(nt, nb, D_MODEL)
        pool_out[...] = pool_hist[nb:].reshape(POOL_BUF, nb, D_POOL)
        conv_out[...] = conv_hist.reshape(RNN_CONV - 1, nb, D_RNN)
        h_out[...] = h


def _ffn_tile(x, read_hist, write_hist, weights, nb, nt, act_scr):
    rows = nb * nt
    vecs, w_up, w_down = weights
    xn = _rmsnorm(x, vecs[_FROW_G_FFN:_FROW_G_FFN + 1, 0:D_MODEL]).astype(_BF16)
    for c in range(0, D_FF, _FF_CHUNK):
        sl = slice(c, c + _FF_CHUNK)
        g_pre = _dot(xn, w_up[:, sl])
        val = _dot(xn, w_up[:, D_FF + c:D_FF + c + _FF_CHUNK])
        ext = jnp.concatenate([read_hist(sl), g_pre], axis=0)
        g_conv = vecs[_FROW_CONV_B:_FROW_CONV_B + 1, sl]
        for j in range(FFN_CONV):
            g_conv = g_conv + ext[j * nb:j * nb + rows] * vecs[_FROW_CONV_W + j:_FROW_CONV_W + j + 1, sl]
        write_hist(sl, ext[rows:])
        act_scr[:, sl] = (jax.nn.gelu(g_conv) * val).astype(_BF16)
    return x + _dot(act_scr[...], w_down[...])


def _ffn_kernel(*refs, nb, nt, carried, final_norm, n_round=0):
    rows = nb * nt
    x_ref = refs[0]
    weights = refs[1:4]
    rest = list(refs[4:])
    g_final = rest.pop(0) if final_norm else None
    finish = (lambda y: _rmsnorm(y, g_final[...])) if final_norm else (lambda y: y)
    if carried:
        out_ref, st_out = rest[n_round:n_round + 2]
        _round_slices(rest[:n_round], rest[n_round + 2:2 * n_round + 2])
        rest = rest[2 * n_round + 2:]
        act_scr = rest[0]

        @pl.when(pl.program_id(0) == 0)
        def _():
            st_out[...] = jnp.zeros_like(st_out)

        def write_hist(sl, hist):
            st_out[:, sl] = hist

        x = x_ref[...]
        out = finish(_ffn_tile(x, lambda sl: st_out[:, sl], write_hist,
                               weights, nb, nt, act_scr))
        if final_norm:
            _store_from_time_major(out, out_ref, rest[1], nb, nt)
        else:
            out_ref[...] = out
    else:
        st_in, out_ref, st_out, act_scr = rest
        x = x_ref[...].reshape(rows, D_MODEL)

        def write_hist(sl, hist):
            st_out[:, :, sl] = hist.reshape(FFN_CONV - 1, nb, _FF_CHUNK)

        out = finish(_ffn_tile(
            x,
            lambda sl: st_in[:, :, sl].reshape((FFN_CONV - 1) * nb, _FF_CHUNK), write_hist,
            weights, nb, nt, act_scr))
        out_ref[...] = out.reshape(nt, nb, D_MODEL)


def _mem_kv_kernel(mem_ref, g_ref, wk_ref, wv_ref, k_ref, v_ref, kb_ref, vb_ref):
    n_mem = mem_ref.shape[1]
    for b in range(mem_ref.shape[0]):
        mn = _rmsnorm(mem_ref[b], g_ref[...]).astype(_BF16)
        for w_ref, out_ref, out_bf_ref in ((wk_ref, k_ref, kb_ref), (wv_ref, v_ref, vb_ref)):
            kv = _dot(mn, w_ref[...])
            for h in range(N_HEADS):
                out_ref[b, pl.ds(h, n_mem, stride=N_HEADS), :] = kv[:, _head(h)]
            if out_bf_ref is kb_ref:
                for h in range(N_HEADS):
                    kb_ref[b, h] = kv[:, _head(h)].T.astype(_BF16)
            else:
                out_bf_ref[b] = kv.astype(_BF16)


def _resident(shape):
    return pl.BlockSpec(shape, lambda *_: (0,) * len(shape), pipeline_mode=pl.Buffered(1))


def _params(semantics, vmem_limit_bytes=_VMEM_LIMIT_BYTES):
    return pltpu.CompilerParams(dimension_semantics=semantics, vmem_limit_bytes=vmem_limit_bytes)


def _layer_resident(a, layer):
    index = layer if a.shape[0] > 1 else 0
    return pl.BlockSpec((None,) + a.shape[1:], lambda *_: (index,) + (0,) * (a.ndim - 1),
                        pipeline_mode=pl.Buffered(1))


def _round_specs(mats, layer, n_steps):
    in_specs, out_specs, out_shapes = [], [], []
    for m in mats:
        rows = m.shape[1] // n_steps
        in_specs.append(pl.BlockSpec((None, rows, m.shape[2]), lambda j: (layer, j, 0)))
        out_specs.append(pl.BlockSpec((None, rows, m.shape[2]), lambda j: (0, j, 0)))
        out_shapes.append(jax.ShapeDtypeStruct((1,) + m.shape[1:], _BF16))
    return in_specs, out_specs, out_shapes


def _pack_rows(arrays, n_rows, width):
    rows = [a[:, None, :] if a.ndim == 2 else a for a in arrays]
    rows = [jnp.pad(r, ((0, 0), (0, 0), (0, width - r.shape[2]))) for r in rows]
    packed = jnp.concatenate(rows, axis=1)
    return jnp.pad(packed, ((0, 0), (0, n_rows - packed.shape[1]), (0, 0)))


def _mixer_weights(pw, layer):
    arrs = [pw[n] for n in ('mixer_vecs', 'w_in', 'w_gate', 'pool_w', 'rnn_wax', 'w_br', 'w_out')]
    return arrs, [_layer_resident(a, layer) for a in arrs]


def _mixer_prompt(h, pw, kb, vb, layer, n_seq, round_next=()):
    nt, nb = _PROMPT_STEPS, n_seq
    rows = nt * nb
    natural_in = h.ndim == 3
    n_tok = h.shape[0] * h.shape[1] if natural_in else h.shape[0]
    r_in, r_out, r_shapes = _round_specs(round_next, layer + 1, n_tok // rows)
    w_arrs, w_specs = _mixer_weights(pw, layer)
    tile = pl.BlockSpec((rows, D_MODEL), lambda j: (j, 0))
    in_tile = pl.BlockSpec((nb, nt, D_MODEL), lambda j: (0, j, 0)) if natural_in else tile
    scratch = [pltpu.VMEM((2, N_HEADS, rows, HEAD_DIM), _F32)]
    if natural_in:
        scratch.append(pltpu.VMEM((D_MODEL // _LANES, rows, _LANES), _F32))
    out_shapes = (jax.ShapeDtypeStruct((n_tok, D_MODEL), _F32),
                  jax.ShapeDtypeStruct(((POOL_BUF + 1) * nb, D_POOL), _F32),
                  jax.ShapeDtypeStruct(((RNN_CONV - 1) * nb, D_RNN), _F32),
                  jax.ShapeDtypeStruct((nb, D_RNN), _F32))
    whole = lambda s: pl.BlockSpec(s.shape, lambda j: (0,) * len(s.shape))
    return pl.pallas_call(
        functools.partial(_mixer_kernel, nb=nb, nt=nt, carried=True, pos0=0, natural_in=natural_in,
                          n_round=len(round_next)),
        grid=(n_tok // rows,),
        in_specs=[in_tile] + w_specs + [_layer_resident(kb, layer), _layer_resident(vb, layer)] + r_in,
        out_specs=(tile,) + tuple(whole(s) for s in out_shapes[1:]) + tuple(r_out),
        out_shape=out_shapes + tuple(r_shapes),
        scratch_shapes=scratch,
        compiler_params=_params(("arbitrary",)),
        name="mixer_prompt",
    )(h, *w_arrs, kb, vb, *round_next)


def _mixer_sample(h, pw, layer, k, v, pool_st, conv_st, h_st):
    nt, n_seq = h.shape[0], h.shape[1]
    nb = _SAMPLE_MIXER_SEQS
    w_arrs, w_specs = _mixer_weights(pw, layer)
    seq3 = lambda a: pl.BlockSpec((a.shape[0], nb, a.shape[2]), lambda i: (0, i, 0))
    h_spec = pl.BlockSpec((nb, D_RNN), lambda i: (i, 0))
    kv_spec = pl.BlockSpec((None, nb) + k.shape[2:], lambda i: (layer, i, 0, 0))
    outs = (h, pool_st, conv_st, h_st)
    return pl.pallas_call(
        functools.partial(_mixer_kernel, nb=nb, nt=nt, carried=False, pos0=PAST_LEN),
        grid=(n_seq // nb,),
        in_specs=[seq3(h)] + w_specs + [kv_spec, kv_spec, seq3(pool_st), seq3(conv_st), h_spec],
        out_specs=(seq3(h), seq3(pool_st), seq3(conv_st), h_spec),
        out_shape=tuple(jax.ShapeDtypeStruct(a.shape, _F32) for a in outs),
        scratch_shapes=[pltpu.VMEM((2, N_HEADS, nt * nb, HEAD_DIM), _F32)],
        compiler_params=_params(("arbitrary",), _VMEM_LIMIT_SAMPLE_MIXER_BYTES),
        name="mixer_sample",
    )(h, *w_arrs, k, v, pool_st, conv_st, h_st)


def _ffn_weights(pw, layer, g_final):
    arrs = [pw[n] for n in ('ffn_vecs', 'w_up', 'w_down')]
    specs = [_layer_resident(a, layer) for a in arrs]
    if g_final is not None:
        arrs.append(g_final)
        specs.append(_resident(g_final.shape))
    return arrs, specs


def _ffn_prompt(h, pw, layer, n_seq, g_final, round_next=()):
    nt, nb = _FFN_PROMPT_STEPS, n_seq
    rows = nt * nb
    r_in, r_out, r_shapes = _round_specs(round_next, layer + 1, h.shape[0] // rows)
    w_arrs, w_specs = _ffn_weights(pw, layer, g_final)
    tile = pl.BlockSpec((rows, D_MODEL), lambda j: (j, 0))
    st_shape = ((FFN_CONV - 1) * nb, D_FF)
    scratch = [pltpu.VMEM((rows, D_FF), _BF16)]
    if g_final is None:
        out_tile, out_shape = tile, h.shape
    else:
        out_tile = pl.BlockSpec((nb, nt, D_MODEL), lambda j: (0, j, 0))
        out_shape = (nb, h.shape[0] // nb, D_MODEL)
        scratch.append(pltpu.VMEM((D_MODEL // _LANES, rows, _LANES), _F32))
    return pl.pallas_call(
        functools.partial(_ffn_kernel, nb=nb, nt=nt, carried=True, final_norm=g_final is not None,
                          n_round=len(round_next)),
        grid=(h.shape[0] // rows,),
        in_specs=[tile] + w_specs + r_in,
        out_specs=(out_tile, pl.BlockSpec(st_shape, lambda j: (0, 0))) + tuple(r_out),
        out_shape=(jax.ShapeDtypeStruct(out_shape, _F32), jax.ShapeDtypeStruct(st_shape, _F32))
        + tuple(r_shapes),
        scratch_shapes=scratch,
        compiler_params=_params(("arbitrary",)),
        name="ffn_prompt",
    )(h, *w_arrs, *round_next)


def _ffn_sample(h, pw, layer, st, g_final):
    nt, n_seq = h.shape[0], h.shape[1]
    nb = _SAMPLE_SEQS
    w_arrs, w_specs = _ffn_weights(pw, layer, g_final)
    seq3 = lambda a: pl.BlockSpec((a.shape[0], nb, a.shape[2]), lambda i: (0, i, 0))
    return pl.pallas_call(
        functools.partial(_ffn_kernel, nb=nb, nt=nt, carried=False, final_norm=g_final is not None),
        grid=(n_seq // nb,),
        in_specs=[seq3(h)] + w_specs + [seq3(st)],
        out_specs=(seq3(h), seq3(st)),
        out_shape=(jax.ShapeDtypeStruct(h.shape, _F32), jax.ShapeDtypeStruct(st.shape, _F32)),
        scratch_shapes=[pltpu.VMEM((nt * nb, D_FF), _BF16)],
        compiler_params=_params(("arbitrary",)),
        name="ffn_sample",
    )(h, *w_arrs, st)


def _mem_kv(mem, g_mem, w_k, w_v):
    n_seq, n_mem = mem.shape[0], mem.shape[1]
    per_layer = lambda a: pl.BlockSpec((None,) + a.shape[1:], lambda l: (l,) + (0,) * (a.ndim - 1))
    out_shapes = ((DEPTH, n_seq, n_mem * N_HEADS, HEAD_DIM),) * 2 + (
        (DEPTH, n_seq, N_HEADS, HEAD_DIM, n_mem), (DEPTH, n_seq, n_mem, D_Q))
    dtypes = (_F32, _F32, _BF16, _BF16)
    return pl.pallas_call(
        _mem_kv_kernel,
        grid=(DEPTH,),
        in_specs=[pl.BlockSpec(mem.shape, lambda l: (0, 0, 0)), per_layer(g_mem), per_layer(w_k), per_layer(w_v)],
        out_specs=tuple(pl.BlockSpec((None,) + s[1:], lambda l, nd=len(s): (l,) + (0,) * (nd - 1))
                        for s in out_shapes),
        out_shape=tuple(jax.ShapeDtypeStruct(s, d) for s, d in zip(out_shapes, dtypes)),
        compiler_params=_params(("arbitrary",)),
        name="mem_kv",
    )(mem, g_mem, w_k, w_v)


def kernel(x_prompt, x_sample, mem_prompt, cache_mem_k, cache_mem_v, state_pool, state_rnn_conv, state_rnn_h, state_ffn_conv, g_mix, w_in, w_gate, b_gate, pool_w, pool_scale, rnn_conv_w, rnn_conv_b, rnn_wa, rnn_ba, rnn_wx, rnn_bx, rnn_lambda, g_mem, w_k, w_v, w_br_pool, w_br_rnn, w_br_attn, w_out, g_ffn, w_up, ffn_conv_w, ffn_conv_b, w_down, g_final):
    bp = x_prompt.shape[0]
    bs, ts, _ = x_sample.shape
    n_mem = mem_prompt.shape[1]
    bf = lambda a: a.astype(_BF16)
    pw = {
        'mixer_vecs': _pack_rows([b_gate, g_mix, pool_scale, rnn_conv_b, rnn_ba, rnn_bx, rnn_lambda, rnn_conv_w],
                                 _MIXER_VEC_ROWS, N_BRANCH * D_MODEL),
        'pool_w': bf(pool_w), 'rnn_wax': bf(jnp.concatenate([rnn_wa, rnn_wx], axis=-1)),
        'w_br': bf(jnp.concatenate([w_br_pool, w_br_rnn, w_br_attn], axis=1)), 'w_out': bf(w_out),
        'ffn_vecs': _pack_rows([ffn_conv_w, ffn_conv_b, g_ffn], _FFN_VEC_ROWS, D_FF),
        'w_in': bf(w_in[:1]), 'w_gate': bf(w_gate[:1]), 'w_up': bf(w_up[:1]), 'w_down': bf(w_down[:1]),
    }
    g_fin = g_final[None, :]

    p_mk, p_mv, kb, vb = _mem_kv(mem_prompt, g_mem[:, None, :], bf(w_k), bf(w_v))
    cache_k = cache_mem_k.reshape(DEPTH, bs, n_mem * N_HEADS, HEAD_DIM)
    cache_v = cache_mem_v.reshape(DEPTH, bs, n_mem * N_HEADS, HEAD_DIM)

    time_major = lambda a: jnp.swapaxes(a, 0, 1)
    hp = x_prompt
    hs = time_major(x_sample)
    p_pool, p_rconv, p_rh, p_fconv = [], [], [], []
    s_pool, s_rconv, s_rh, s_fconv = [], [], [], []
    for l in range(DEPTH):
        last = l == DEPTH - 1
        hp, pool_st, conv_st, h_st, *next_ffn = _mixer_prompt(
            hp, pw, kb, vb, l, bp, round_next=() if last else (w_up, w_down))
        hp, ffn_st, *next_mixer = _ffn_prompt(
            hp, pw, l, bp, g_fin if last else None, round_next=() if last else (w_in, w_gate))
        p_pool.append(time_major(pool_st.reshape(POOL_BUF + 1, bp, D_POOL)[1:]))
        p_rconv.append(time_major(conv_st.reshape(RNN_CONV - 1, bp, D_RNN)))
        p_rh.append(h_st)
        p_fconv.append(time_major(ffn_st.reshape(FFN_CONV - 1, bp, D_FF)))

        hs, pool_o, conv_o, h_o = _mixer_sample(
            hs, pw, l, cache_k, cache_v, time_major(state_pool[l]), time_major(state_rnn_conv[l]),
            state_rnn_h[l])
        hs, ffn_o = _ffn_sample(hs, pw, l, time_major(state_ffn_conv[l]), g_fin if last else None)
        s_pool.append(time_major(pool_o))
        s_rconv.append(time_major(conv_o))
        s_rh.append(h_o)
        s_fconv.append(time_major(ffn_o))
        if not last:
            pw = dict(pw, w_up=next_ffn[0], w_down=next_ffn[1], w_in=next_mixer[0], w_gate=next_mixer[1])

    head = (N_HEADS, HEAD_DIM)
    return (hp, time_major(hs),
            jnp.stack(p_pool), jnp.stack(p_rconv), jnp.stack(p_rh), jnp.stack(p_fconv),
            p_mk.reshape(DEPTH, bp, n_mem, *head), p_mv.reshape(DEPTH, bp, n_mem, *head),
            jnp.stack(s_pool), jnp.stack(s_rconv), jnp.stack(s_rh), jnp.stack(s_fconv))
```

```python
import functools

import jax
import jax.numpy as jnp
from jax import lax
from jax.experimental import pallas as pl
from jax.experimental.pallas import tpu as pltpu

D_MODEL = 1024
DEPTH = 2
PAST_LEN = 16384
POOL_WINDOWS = (2, 4, 8, 16)
D_POOL = D_MODEL // 2
POOL_GROUP = D_POOL // len(POOL_WINDOWS)
POOL_BUF = max(POOL_WINDOWS) - 1
D_RNN = D_MODEL
N_RNN_BLOCKS = 8
RNN_BLOCK = D_RNN // N_RNN_BLOCKS
RNN_CONV = 4
RG_C = 8.0
N_HEADS = 4
HEAD_DIM = 128
D_Q = N_HEADS * HEAD_DIM
N_BRANCH = 3
D_FF = 3 * D_MODEL
FFN_CONV = 3
EPS = 1e-6

_V7X_VMEM_BYTES = 64 * 1024 * 1024
_VMEM_LIMIT_BYTES = _V7X_VMEM_BYTES - 8 * 1024 * 1024
_VMEM_LIMIT_SAMPLE_MIXER_BYTES = _V7X_VMEM_BYTES - 3 * 1024 * 1024

_PROMPT_STEPS = 64
_FFN_PROMPT_STEPS = 128
_SAMPLE_SEQS = 64
_SAMPLE_MIXER_SEQS = 16
_FF_CHUNK = 512

_MXU_COLS = 256
_LANES = 128

_ROW_B_GATE, _ROW_G_MIX, _ROW_POOL_SCALE, _ROW_CONV_B, _ROW_B_A, _ROW_B_X, _ROW_LAMBDA, _ROW_CONV_W = range(8)
_MIXER_VEC_ROWS = 16
_FROW_CONV_W, _FROW_CONV_B, _FROW_G_FFN = 0, FFN_CONV, FFN_CONV + 1
_FFN_VEC_ROWS = 8

_F32 = jnp.float32
_BF16 = jnp.bfloat16


def _dot(a, b):
    return jnp.dot(a, b, preferred_element_type=_F32)


def _dot_nt(a, b):
    return lax.dot_general(a, b, (((1,), (1,)), ((), ())), preferred_element_type=_F32)


def _rmsnorm(x, g):
    return x * lax.rsqrt(jnp.mean(x * x, axis=-1, keepdims=True) + EPS) * g


def _sigmoid(x):
    return 0.5 * jnp.tanh(0.5 * x) + 0.5


def _softplus(x):
    return jnp.maximum(x, 0.0) + jnp.log1p(jnp.exp(-jnp.abs(x)))


def _softmax_rows(s):
    e = jnp.exp(s - jnp.max(s, axis=-1, keepdims=True))
    return e / jnp.sum(e, axis=-1, keepdims=True)


def _head(h):
    return slice(h * HEAD_DIM, (h + 1) * HEAD_DIM)


def _pool_window_sums(ext, rows, nb):
    sums = []
    for g, w in enumerate(POOL_WINDOWS):
        s = ext[:, g * POOL_GROUP:(g + 1) * POOL_GROUP]
        off, k = 0, 1
        while k < w:
            s = s[k * nb:] + s[:-k * nb]
            off += k * nb
            k *= 2
        start = (POOL_BUF + 1) * nb - off
        sums.append(s[start:start + rows])
    return sums


def _load_time_major(x_ref, scr, nb, nt):
    for b in range(nb):
        for c in range(scr.shape[0]):
            scr[c, pl.ds(b, nt, stride=nb), :] = x_ref[b, :, c * _LANES:(c + 1) * _LANES]
    return jnp.concatenate([scr[c] for c in range(scr.shape[0])], axis=1)


def _store_from_time_major(y, y_ref, scr, nb, nt):
    for c in range(scr.shape[0]):
        scr[c] = y[:, c * _LANES:(c + 1) * _LANES]
    for b in range(nb):
        for c in range(scr.shape[0]):
            y_ref[b, :, c * _LANES:(c + 1) * _LANES] = scr[c, pl.ds(b, nt, stride=nb), :]


def _scores_per_head(q_rows, k_ref, nb):
    return jnp.concatenate(
        [_dot(q_rows(b, h).astype(_BF16), k_ref[b, h]) for b in range(nb) for h in range(N_HEADS)], axis=0)


def _values_per_head(p, v_ref, nb, nt, put):
    for b in range(nb):
        for h in range(N_HEADS):
            n = b * N_HEADS + h
            put(b, h, _dot(p[n * nt:(n + 1) * nt].astype(_BF16), v_ref[b, :, _head(h)]))


def _scores_head_rows(q_rows, k_ref, nb, nt):
    s = jnp.concatenate(
        [_dot_nt(jnp.concatenate([q_rows(b, h) for h in range(N_HEADS)], axis=0).astype(_BF16),
                 k_ref[b].astype(_BF16)) for b in range(nb)], axis=0)
    r = lax.broadcasted_iota(jnp.int32, s.shape, 0)
    c = lax.broadcasted_iota(jnp.int32, s.shape, 1)
    same_head = ((r >> (nt.bit_length() - 1)) & (N_HEADS - 1)) == (c & (N_HEADS - 1))
    return jnp.where(same_head, s, -jnp.inf)


def _values_head_rows(p, v_ref, nb, nt, put):
    hq = N_HEADS * nt
    for b in range(nb):
        o = _dot(p[b * hq:(b + 1) * hq].astype(_BF16), v_ref[b].astype(_BF16))
        for h in range(N_HEADS):
            put(b, h, o[h * nt:(h + 1) * nt])


def _mixer_tile(x, pool_hist, conv_hist, h0, weights, nb, nt, first_pos, k_ref, v_ref, qo_scr, head_rows):
    rows = nb * nt
    vecs, w_in, w_gate, pool_w, w_ax, w_br, w_out = weights

    def vec(r, cols=slice(0, D_MODEL)):
        return vecs[r:r + 1, cols]

    n_gate_chunks = N_BRANCH * D_MODEL // _MXU_COLS
    gate = [None] * n_gate_chunks
    xn = _rmsnorm(x, vec(_ROW_G_MIX)).astype(_BF16)

    def gate_chunk(j):
        sl = slice(j * _MXU_COLS, (j + 1) * _MXU_COLS)
        return _sigmoid(_dot(xn, w_gate[:, sl]) + vec(_ROW_B_GATE, sl))

    u_rnn = _dot(xn, w_in[:, D_POOL:D_POOL + D_RNN])
    u_pool = _dot(xn, w_in[:, 0:D_POOL])
    q = _dot(xn, w_in[:, D_POOL + D_RNN:])

    ext_r = jnp.concatenate([conv_hist, u_rnn], axis=0)
    xc = vec(_ROW_CONV_B)
    for j in range(RNN_CONV):
        xc = xc + ext_r[j * nb:j * nb + rows] * vec(_ROW_CONV_W + j)
    new_conv_hist = ext_r[rows:]
    xcb = xc.astype(_BF16)
    log_a_unit = -RG_C * _softplus(-vec(_ROW_LAMBDA))
    cols, h_last = [], []
    for n in range(N_RNN_BLOCKS):
        sl = slice(n * RNN_BLOCK, (n + 1) * RNN_BLOCK)
        ri = _dot(xcb[:, sl], w_ax[n])
        r = _sigmoid(ri[:, :RNN_BLOCK] + vec(_ROW_B_A, sl))
        i = _sigmoid(ri[:, RNN_BLOCK:] + vec(_ROW_B_X, sl))
        log_a = r * log_a_unit[:, sl]
        a = jnp.exp(log_a)
        th = jnp.tanh(log_a)
        b = jnp.sqrt(-2.0 * th / (1.0 - th)) * (i * xc[:, sl])
        h = h0[:, sl]
        hs = []
        for t in range(nt):
            h = a[t * nb:(t + 1) * nb] * h + b[t * nb:(t + 1) * nb]
            hs.append(h)
        cols.append(jnp.concatenate(hs, axis=0))
        h_last.append(h)
        for j in range(n * n_gate_chunks // N_RNN_BLOCKS, (n + 1) * n_gate_chunks // N_RNN_BLOCKS):
            gate[j] = gate_chunk(j)
    y_rnn = jnp.concatenate(cols, axis=1).astype(_BF16)
    new_h = jnp.concatenate(h_last, axis=1)

    for h in range(N_HEADS):
        qo_scr[0, h] = q[:, _head(h)]

    def q_rows(b, h):
        return qo_scr[0, h, pl.ds(b, nt, stride=nb), :]

    def put_rows(b, h, o):
        qo_scr[1, h, pl.ds(b, nt, stride=nb), :] = o

    s = _scores_head_rows(q_rows, k_ref, nb, nt) if head_rows else _scores_per_head(q_rows, k_ref, nb)
    rnn_branch = _dot(y_rnn, w_br[D_POOL:D_POOL + D_RNN, :])
    p = _softmax_rows(s * (HEAD_DIM ** -0.5))
    (_values_head_rows if head_rows else _values_per_head)(p, v_ref, nb, nt, put_rows)

    ext = jnp.concatenate([pool_hist, u_pool], axis=0)
    wins = _pool_window_sums(ext, rows, nb)
    if first_pos is not None:
        row = lax.broadcasted_iota(jnp.int32, (rows, POOL_GROUP), 0)
        pos1 = first_pos + (row >> (nb.bit_length() - 1)) + 1
        inv_pos1 = 1.0 / pos1.astype(_F32)
    ys = []
    for g, w in enumerate(POOL_WINDOWS):
        if first_pos is not None:
            mean = wins[g] * jnp.maximum(inv_pos1, 1.0 / w)
        else:
            mean = wins[g] * (1.0 / w)
        d = mean - u_pool[:, g * POOL_GROUP:(g + 1) * POOL_GROUP]
        ys.append(_dot(d.astype(_BF16), pool_w[g]))
    y_pool = (jnp.concatenate(ys, axis=1) * vec(_ROW_POOL_SCALE, slice(0, D_POOL))).astype(_BF16)
    new_pool_hist = ext[rows:]
    pool_branch = _dot(y_pool, w_br[0:D_POOL, :])

    y_attn = jnp.concatenate([qo_scr[1, h] for h in range(N_HEADS)], axis=1).astype(_BF16)

    out = x
    chunks_per_branch = D_MODEL // _MXU_COLS
    for c in range(chunks_per_branch):
        sl = slice(c * _MXU_COLS, (c + 1) * _MXU_COLS)
        branches = (pool_branch[:, sl], rnn_branch[:, sl], _dot(y_attn, w_br[D_POOL + D_RNN:, sl]))
        merged = None
        for k, branch in enumerate(branches):
            term = gate[k * chunks_per_branch + c] * branch
            merged = term if merged is None else merged + term
        out = out + _dot(merged.astype(_BF16), w_out[sl, :])
    return out, new_pool_hist, new_conv_hist, new_h


def _round_slices(src_refs, dst_refs):
    for src, dst in zip(src_refs, dst_refs):
        dst[...] = src[...].astype(_BF16)


def _mixer_kernel(*refs, nb, nt, carried, pos0, natural_in=False, n_round=0):
    rows = nb * nt
    x_ref = refs[0]
    weights = refs[1:8]
    if carried:
        k_ref, v_ref = refs[8:10]
        out_ref, pool_st, conv_st, h_st = refs[10 + n_round:14 + n_round]
        _round_slices(refs[10:10 + n_round], refs[14 + n_round:14 + 2 * n_round])
        refs = refs[14 + 2 * n_round:]
        qo_scr = refs[0]
        step = pl.program_id(0)

        @pl.when(step == 0)
        def _():
            pool_st[...] = jnp.zeros_like(pool_st)
            conv_st[...] = jnp.zeros_like(conv_st)
            h_st[...] = jnp.zeros_like(h_st)

        x = _load_time_major(x_ref, refs[1], nb, nt) if natural_in else x_ref[...]
        first_pos = pos0 + step * nt if pos0 < POOL_BUF else None
        out, pool_hist, conv_hist, h = _mixer_tile(
            x, pool_st[...], conv_st[...], h_st[...], weights, nb, nt,
            first_pos, k_ref, v_ref, qo_scr, head_rows=False)
        out_ref[...] = out
        pool_st[...] = pool_hist
        conv_st[...] = conv_hist
        h_st[...] = h
    else:
        assert pos0 >= POOL_BUF
        (k_ref, v_ref, pool_in, conv_in, h_in, out_ref, pool_out, conv_out, h_out, qo_scr) = refs[8:]
        x = x_ref[...].reshape(rows, D_MODEL)
        pool_hist = jnp.concatenate(
            [jnp.zeros((nb, D_POOL), _F32), pool_in[...].reshape(POOL_BUF * nb, D_POOL)], axis=0)
        out, pool_hist, conv_hist, h = _mixer_tile(
            x, pool_hist,
            conv_in[...].reshape((RNN_CONV - 1) * nb, D_RNN), h_in[...], weights, nb, nt,
            None, k_ref, v_ref, qo_scr, head_rows=True)
        out_ref[...] = out.reshape(nt, nb, D_MODEL)
        pool_out[...] = pool_hist[nb:].reshape(POOL_BUF, nb, D_POOL)
        conv_out[...] = conv_hist.reshape(RNN_CONV - 1, nb, D_RNN)
        h_out[...] = h


def _ffn_tile(x, read_hist, write_hist, weights, nb, nt, act_scr):
    rows = nb * nt
    vecs, w_up, w_down = weights
    xn = _rmsnorm(x, vecs[_FROW_G_FFN:_FROW_G_FFN + 1, 0:D_MODEL]).astype(_BF16)
    for c in range(0, D_FF, _FF_CHUNK):
        sl = slice(c, c + _FF_CHUNK)
        g_pre = _dot(xn, w_up[:, sl])
        val = _dot(xn, w_up[:, D_FF + c:D_FF + c + _FF_CHUNK])
        ext = jnp.concatenate([read_hist(sl), g_pre], axis=0)
        g_conv = vecs[_FROW_CONV_B:_FROW_CONV_B + 1, sl]
        for j in range(FFN_CONV):
            g_conv = g_conv + ext[j * nb:j * nb + rows] * vecs[_FROW_CONV_W + j:_FROW_CONV_W + j + 1, sl]
        write_hist(sl, ext[rows:])
        act_scr[:, sl] = (jax.nn.gelu(g_conv) * val).astype(_BF16)
    return x + _dot(act_scr[...], w_down[...])


def _ffn_kernel(*refs, nb, nt, carried, final_norm, n_round=0):
    rows = nb * nt
    x_ref = refs[0]
    weights = refs[1:4]
    rest = list(refs[4:])
    g_final = rest.pop(0) if final_norm else None
    finish = (lambda y: _rmsnorm(y, g_final[...])) if final_norm else (lambda y: y)
    if carried:
        out_ref, st_out = rest[n_round:n_round + 2]
        _round_slices(rest[:n_round], rest[n_round + 2:2 * n_round + 2])
        rest = rest[2 * n_round + 2:]
        act_scr = rest[0]

        @pl.when(pl.program_id(0) == 0)
        def _():
            st_out[...] = jnp.zeros_like(st_out)

        def write_hist(sl, hist):
            st_out[:, sl] = hist

        x = x_ref[...]
        out = finish(_ffn_tile(x, lambda sl: st_out[:, sl], write_hist,
                               weights, nb, nt, act_scr))
        if final_norm:
            _store_from_time_major(out, out_ref, rest[1], nb, nt)
        else:
            out_ref[...] = out
    else:
        st_in, out_ref, st_out, act_scr = rest
        x = x_ref[...].reshape(rows, D_MODEL)

        def write_hist(sl, hist):
            st_out[:, :, sl] = hist.reshape(FFN_CONV - 1, nb, _FF_CHUNK)

        out = finish(_ffn_tile(
            x,
            lambda sl: st_in[:, :, sl].reshape((FFN_CONV - 1) * nb, _FF_CHUNK), write_hist,
            weights, nb, nt, act_scr))
        out_ref[...] = out.reshape(nt, nb, D_MODEL)


def _mem_kv_kernel(mem_ref, g_ref, wk_ref, wv_ref, k_ref, v_ref, kb_ref, vb_ref):
    n_mem = mem_ref.shape[1]
    for b in range(mem_ref.shape[0]):
        mn = _rmsnorm(mem_ref[b], g_ref[...]).astype(_BF16)
        for w_ref, out_ref, out_bf_ref in ((wk_ref, k_ref, kb_ref), (wv_ref, v_ref, vb_ref)):
            kv = _dot(mn, w_ref[...])
            for h in range(N_HEADS):
                out_ref[b, pl.ds(h, n_mem, stride=N_HEADS), :] = kv[:, _head(h)]
            if out_bf_ref is kb_ref:
                for h in range(N_HEADS):
                    kb_ref[b, h] = kv[:, _head(h)].T.astype(_BF16)
            else:
                out_bf_ref[b] = kv.astype(_BF16)


def _resident(shape):
    return pl.BlockSpec(shape, lambda *_: (0,) * len(shape), pipeline_mode=pl.Buffered(1))


def _params(semantics, vmem_limit_bytes=_VMEM_LIMIT_BYTES):
    return pltpu.CompilerParams(dimension_semantics=semantics, vmem_limit_bytes=vmem_limit_bytes)


def _layer_resident(a, layer):
    index = layer if a.shape[0] > 1 else 0
    return pl.BlockSpec((None,) + a.shape[1:], lambda *_: (index,) + (0,) * (a.ndim - 1),
                        pipeline_mode=pl.Buffered(1))


def _round_specs(mats, layer, n_steps):
    in_specs, out_specs, out_shapes = [], [], []
    for m in mats:
        rows = m.shape[1] // n_steps
        in_specs.append(pl.BlockSpec((None, rows, m.shape[2]), lambda j: (layer, j, 0)))
        out_specs.append(pl.BlockSpec((None, rows, m.shape[2]), lambda j: (0, j, 0)))
        out_shapes.append(jax.ShapeDtypeStruct((1,) + m.shape[1:], _BF16))
    return in_specs, out_specs, out_shapes


def _pack_rows(arrays, n_rows, width):
    rows = [a[:, None, :] if a.ndim == 2 else a for a in arrays]
    rows = [jnp.pad(r, ((0, 0), (0, 0), (0, width - r.shape[2]))) for r in rows]
    packed = jnp.concatenate(rows, axis=1)
    return jnp.pad(packed, ((0, 0), (0, n_rows - packed.shape[1]), (0, 0)))


def _mixer_weights(pw, layer):
    arrs = [pw[n] for n in ('mixer_vecs', 'w_in', 'w_gate', 'pool_w', 'rnn_wax', 'w_br', 'w_out')]
    return arrs, [_layer_resident(a, layer) for a in arrs]


def _mixer_prompt(h, pw, kb, vb, layer, n_seq, round_mats=(), round_layer=0):
    nt, nb = _PROMPT_STEPS, n_seq
    rows = nt * nb
    natural_in = h.ndim == 3
    n_tok = h.shape[0] * h.shape[1] if natural_in else h.shape[0]
    r_in, r_out, r_shapes = _round_specs(round_mats, round_layer, n_tok // rows)
    w_arrs, w_specs = _mixer_weights(pw, layer)
    tile = pl.BlockSpec((rows, D_MODEL), lambda j: (j, 0))
    in_tile = pl.BlockSpec((nb, nt, D_MODEL), lambda j: (0, j, 0)) if natural_in else tile
    scratch = [pltpu.VMEM((2, N_HEADS, rows, HEAD_DIM), _F32)]
    if natural_in:
        scratch.append(pltpu.VMEM((D_MODEL // _LANES, rows, _LANES), _F32))
    out_shapes = (jax.ShapeDtypeStruct((n_tok, D_MODEL), _F32),
                  jax.ShapeDtypeStruct(((POOL_BUF + 1) * nb, D_POOL), _F32),
                  jax.ShapeDtypeStruct(((RNN_CONV - 1) * nb, D_RNN), _F32),
                  jax.ShapeDtypeStruct((nb, D_RNN), _F32))
    whole = lambda s: pl.BlockSpec(s.shape, lambda j: (0,) * len(s.shape))
    return pl.pallas_call(
        functools.partial(_mixer_kernel, nb=nb, nt=nt, carried=True, pos0=0, natural_in=natural_in,
                          n_round=len(round_mats)),
        grid=(n_tok // rows,),
        in_specs=[in_tile] + w_specs + [_layer_resident(kb, layer), _layer_resident(vb, layer)] + r_in,
        out_specs=(tile,) + tuple(whole(s) for s in out_shapes[1:]) + tuple(r_out),
        out_shape=out_shapes + tuple(r_shapes),
        scratch_shapes=scratch,
        compiler_params=_params(("arbitrary",)),
        name="mixer_prompt",
    )(h, *w_arrs, kb, vb, *round_mats)


def _mixer_sample(h, pw, layer, k, v, pool_st, conv_st, h_st):
    nt, n_seq = h.shape[0], h.shape[1]
    nb = _SAMPLE_MIXER_SEQS
    w_arrs, w_specs = _mixer_weights(pw, layer)
    seq3 = lambda a: pl.BlockSpec((a.shape[0], nb, a.shape[2]), lambda i: (0, i, 0))
    h_spec = pl.BlockSpec((nb, D_RNN), lambda i: (i, 0))
    kv_spec = pl.BlockSpec((None, nb) + k.shape[2:], lambda i: (layer, i, 0, 0))
    outs = (h, pool_st, conv_st, h_st)
    return pl.pallas_call(
        functools.partial(_mixer_kernel, nb=nb, nt=nt, carried=False, pos0=PAST_LEN),
        grid=(n_seq // nb,),
        in_specs=[seq3(h)] + w_specs + [kv_spec, kv_spec, seq3(pool_st), seq3(conv_st), h_spec],
        out_specs=(seq3(h), seq3(pool_st), seq3(conv_st), h_spec),
        out_shape=tuple(jax.ShapeDtypeStruct(a.shape, _F32) for a in outs),
        scratch_shapes=[pltpu.VMEM((2, N_HEADS, nt * nb, HEAD_DIM), _F32)],
        compiler_params=_params(("arbitrary",), _VMEM_LIMIT_SAMPLE_MIXER_BYTES),
        name="mixer_sample",
    )(h, *w_arrs, k, v, pool_st, conv_st, h_st)


def _ffn_weights(pw, layer, g_final):
    arrs = [pw[n] for n in ('ffn_vecs', 'w_up', 'w_down')]
    specs = [_layer_resident(a, layer) for a in arrs]
    if g_final is not None:
        arrs.append(g_final)
        specs.append(_resident(g_final.shape))
    return arrs, specs


def _ffn_prompt(h, pw, layer, n_seq, g_final, round_mats=(), round_layer=0):
    nt, nb = _FFN_PROMPT_STEPS, n_seq
    rows = nt * nb
    r_in, r_out, r_shapes = _round_specs(round_mats, round_layer, h.shape[0] // rows)
    w_arrs, w_specs = _ffn_weights(pw, layer, g_final)
    tile = pl.BlockSpec((rows, D_MODEL), lambda j: (j, 0))
    st_shape = ((FFN_CONV - 1) * nb, D_FF)
    scratch = [pltpu.VMEM((rows, D_FF), _BF16)]
    if g_final is None:
        out_tile, out_shape = tile, h.shape
    else:
        out_tile = pl.BlockSpec((nb, nt, D_MODEL), lambda j: (0, j, 0))
        out_shape = (nb, h.shape[0] // nb, D_MODEL)
        scratch.append(pltpu.VMEM((D_MODEL // _LANES, rows, _LANES), _F32))
    return pl.pallas_call(
        functools.partial(_ffn_kernel, nb=nb, nt=nt, carried=True, final_norm=g_final is not None,
                          n_round=len(round_mats)),
        grid=(h.shape[0] // rows,),
        in_specs=[tile] + w_specs + r_in,
        out_specs=(out_tile, pl.BlockSpec(st_shape, lambda j: (0, 0))) + tuple(r_out),
        out_shape=(jax.ShapeDtypeStruct(out_shape, _F32), jax.ShapeDtypeStruct(st_shape, _F32))
        + tuple(r_shapes),
        scratch_shapes=scratch,
        compiler_params=_params(("arbitrary",)),
        name="ffn_prompt",
    )(h, *w_arrs, *round_mats)


def _ffn_sample(h, pw, layer, st, g_final):
    nt, n_seq = h.shape[0], h.shape[1]
    nb = _SAMPLE_SEQS
    w_arrs, w_specs = _ffn_weights(pw, layer, g_final)
    seq3 = lambda a: pl.BlockSpec((a.shape[0], nb, a.shape[2]), lambda i: (0, i, 0))
    return pl.pallas_call(
        functools.partial(_ffn_kernel, nb=nb, nt=nt, carried=False, final_norm=g_final is not None),
        grid=(n_seq // nb,),
        in_specs=[seq3(h)] + w_specs + [seq3(st)],
        out_specs=(seq3(h), seq3(st)),
        out_shape=(jax.ShapeDtypeStruct(h.shape, _F32), jax.ShapeDtypeStruct(st.shape, _F32)),
        scratch_shapes=[pltpu.VMEM((nt * nb, D_FF), _BF16)],
        compiler_params=_params(("arbitrary",)),
        name="ffn_sample",
    )(h, *w_arrs, st)


def _mem_kv(mem, g_mem, w_k, w_v):
    n_seq, n_mem = mem.shape[0], mem.shape[1]
    per_layer = lambda a: pl.BlockSpec((None,) + a.shape[1:], lambda l: (l,) + (0,) * (a.ndim - 1))
    out_shapes = ((DEPTH, n_seq, n_mem * N_HEADS, HEAD_DIM),) * 2 + (
        (DEPTH, n_seq, N_HEADS, HEAD_DIM, n_mem), (DEPTH, n_seq, n_mem, D_Q))
    dtypes = (_F32, _F32, _BF16, _BF16)
    return pl.pallas_call(
        _mem_kv_kernel,
        grid=(DEPTH,),
        in_specs=[pl.BlockSpec(mem.shape, lambda l: (0, 0, 0)), per_layer(g_mem), per_layer(w_k), per_layer(w_v)],
        out_specs=tuple(pl.BlockSpec((None,) + s[1:], lambda l, nd=len(s): (l,) + (0,) * (nd - 1))
                        for s in out_shapes),
        out_shape=tuple(jax.ShapeDtypeStruct(s, d) for s, d in zip(out_shapes, dtypes)),
        compiler_params=_params(("arbitrary",)),
        name="mem_kv",
    )(mem, g_mem, w_k, w_v)


def kernel(x_prompt, x_sample, mem_prompt, cache_mem_k, cache_mem_v, state_pool, state_rnn_conv, state_rnn_h, state_ffn_conv, g_mix, w_in, w_gate, b_gate, pool_w, pool_scale, rnn_conv_w, rnn_conv_b, rnn_wa, rnn_ba, rnn_wx, rnn_bx, rnn_lambda, g_mem, w_k, w_v, w_br_pool, w_br_rnn, w_br_attn, w_out, g_ffn, w_up, ffn_conv_w, ffn_conv_b, w_down, g_final):
    bp = x_prompt.shape[0]
    bs, ts, _ = x_sample.shape
    n_mem = mem_prompt.shape[1]
    bf = lambda a: a.astype(_BF16)
    pw = {
        'mixer_vecs': _pack_rows([b_gate, g_mix, pool_scale, rnn_conv_b, rnn_ba, rnn_bx, rnn_lambda, rnn_conv_w],
                                 _MIXER_VEC_ROWS, N_BRANCH * D_MODEL),
        'pool_w': bf(pool_w), 'rnn_wax': bf(jnp.concatenate([rnn_wa, rnn_wx], axis=-1)),
        'w_br': bf(jnp.concatenate([w_br_pool, w_br_rnn, w_br_attn], axis=1)), 'w_out': bf(w_out),
        'ffn_vecs': _pack_rows([ffn_conv_w, ffn_conv_b, g_ffn], _FFN_VEC_ROWS, D_FF),
        'w_in': bf(w_in[:1]), 'w_gate': bf(w_gate[:1]),
    }
    g_fin = g_final[None, :]

    p_mk, p_mv, kb, vb = _mem_kv(mem_prompt, g_mem[:, None, :], bf(w_k), bf(w_v))
    cache_k = cache_mem_k.reshape(DEPTH, bs, n_mem * N_HEADS, HEAD_DIM)
    cache_v = cache_mem_v.reshape(DEPTH, bs, n_mem * N_HEADS, HEAD_DIM)

    time_major = lambda a: jnp.swapaxes(a, 0, 1)
    hp = x_prompt
    hs = time_major(x_sample)
    p_pool, p_rconv, p_rh, p_fconv = [], [], [], []
    s_pool, s_rconv, s_rh, s_fconv = [], [], [], []
    for l in range(DEPTH):
        last = l == DEPTH - 1
        hp, pool_st, conv_st, h_st, *ffn_w = _mixer_prompt(hp, pw, kb, vb, l, bp, (w_up, w_down), l)
        pw = dict(pw, w_up=ffn_w[0], w_down=ffn_w[1])
        hp, ffn_st, *mixer_w = _ffn_prompt(
            hp, pw, l, bp, g_fin if last else None, () if last else (w_in, w_gate), l + 1)
        p_pool.append(time_major(pool_st.reshape(POOL_BUF + 1, bp, D_POOL)[1:]))
        p_rconv.append(time_major(conv_st.reshape(RNN_CONV - 1, bp, D_RNN)))
        p_rh.append(h_st)
        p_fconv.append(time_major(ffn_st.reshape(FFN_CONV - 1, bp, D_FF)))

        hs, pool_o, conv_o, h_o = _mixer_sample(
            hs, pw, l, cache_k, cache_v, time_major(state_pool[l]), time_major(state_rnn_conv[l]),
            state_rnn_h[l])
        hs, ffn_o = _ffn_sample(hs, pw, l, time_major(state_ffn_conv[l]), g_fin if last else None)
        s_pool.append(time_major(pool_o))
        s_rconv.append(time_major(conv_o))
        s_rh.append(h_o)
        s_fconv.append(time_major(ffn_o))
        if not last:
            pw = dict(pw, w_in=mixer_w[0], w_gate=mixer_w[1])

    head = (N_HEADS, HEAD_DIM)
    return (hp, time_major(hs),
            jnp.stack(p_pool), jnp.stack(p_rconv), jnp.stack(p_rh), jnp.stack(p_fconv),
            p_mk.reshape(DEPTH, bp, n_mem, *head), p_mv.reshape(DEPTH, bp, n_mem, *head),
            jnp.stack(s_pool), jnp.stack(s_rconv), jnp.stack(s_rh), jnp.stack(s_fconv))
```

```python
import functools

import jax
import jax.numpy as jnp
from jax import lax
from jax.experimental import pallas as pl
from jax.experimental.pallas import tpu as pltpu

D_MODEL = 1024
DEPTH = 2
PAST_LEN = 16384
POOL_WINDOWS = (2, 4, 8, 16)
D_POOL = D_MODEL // 2
POOL_GROUP = D_POOL // len(POOL_WINDOWS)
POOL_BUF = max(POOL_WINDOWS) - 1
D_RNN = D_MODEL
N_RNN_BLOCKS = 8
RNN_BLOCK = D_RNN // N_RNN_BLOCKS
RNN_CONV = 4
RG_C = 8.0
N_HEADS = 4
HEAD_DIM = 128
D_Q = N_HEADS * HEAD_DIM
N_BRANCH = 3
D_FF = 3 * D_MODEL
FFN_CONV = 3
EPS = 1e-6

_V7X_VMEM_BYTES = 64 * 1024 * 1024
_VMEM_LIMIT_BYTES = _V7X_VMEM_BYTES - 8 * 1024 * 1024
_VMEM_LIMIT_SAMPLE_MIXER_BYTES = _V7X_VMEM_BYTES - 3 * 1024 * 1024

_PROMPT_STEPS = 64
_FFN_PROMPT_STEPS = 128
_SAMPLE_SEQS = 64
_SAMPLE_MIXER_SEQS = 16
_FF_CHUNK = 512

_MXU_COLS = 256
_LANES = 128

_ROW_B_GATE, _ROW_G_MIX, _ROW_POOL_SCALE, _ROW_CONV_B, _ROW_B_A, _ROW_B_X, _ROW_LAMBDA, _ROW_CONV_W = range(8)
_MIXER_VEC_ROWS = 16
_FROW_CONV_W, _FROW_CONV_B, _FROW_G_FFN = 0, FFN_CONV, FFN_CONV + 1
_FFN_VEC_ROWS = 8

_F32 = jnp.float32
_BF16 = jnp.bfloat16


def _dot(a, b):
    return jnp.dot(a, b, preferred_element_type=_F32)


def _dot_nt(a, b):
    return lax.dot_general(a, b, (((1,), (1,)), ((), ())), preferred_element_type=_F32)


def _rmsnorm(x, g):
    return x * lax.rsqrt(jnp.mean(x * x, axis=-1, keepdims=True) + EPS) * g


def _sigmoid(x):
    return 0.5 * jnp.tanh(0.5 * x) + 0.5


def _softplus(x):
    return jnp.maximum(x, 0.0) + jnp.log1p(jnp.exp(-jnp.abs(x)))


def _softmax_rows(s):
    e = jnp.exp(s - jnp.max(s, axis=-1, keepdims=True))
    return e / jnp.sum(e, axis=-1, keepdims=True)


def _head(h):
    return slice(h * HEAD_DIM, (h + 1) * HEAD_DIM)


def _pool_window_sums(ext, rows, nb):
    sums = []
    for g, w in enumerate(POOL_WINDOWS):
        s = ext[:, g * POOL_GROUP:(g + 1) * POOL_GROUP]
        off, k = 0, 1
        while k < w:
            s = s[k * nb:] + s[:-k * nb]
            off += k * nb
            k *= 2
        start = (POOL_BUF + 1) * nb - off
        sums.append(s[start:start + rows])
    return sums


def _load_time_major(x_ref, scr, nb, nt):
    for b in range(nb):
        for c in range(scr.shape[0]):
            scr[c, pl.ds(b, nt, stride=nb), :] = x_ref[b, :, c * _LANES:(c + 1) * _LANES]
    return jnp.concatenate([scr[c] for c in range(scr.shape[0])], axis=1)


def _store_from_time_major(y, y_ref, scr, nb, nt):
    for c in range(scr.shape[0]):
        scr[c] = y[:, c * _LANES:(c + 1) * _LANES]
    for b in range(nb):
        for c in range(scr.shape[0]):
            y_ref[b, :, c * _LANES:(c + 1) * _LANES] = scr[c, pl.ds(b, nt, stride=nb), :]


def _scores_per_head(q_rows, k_ref, nb):
    return jnp.concatenate(
        [_dot(q_rows(b, h).astype(_BF16), k_ref[b, h]) for b in range(nb) for h in range(N_HEADS)], axis=0)


def _values_per_head(p, v_ref, nb, nt, put):
    for b in range(nb):
        for h in range(N_HEADS):
            n = b * N_HEADS + h
            put(b, h, _dot(p[n * nt:(n + 1) * nt].astype(_BF16), v_ref[b, :, _head(h)]))


def _scores_head_rows(q_rows, k_ref, nb, nt):
    s = jnp.concatenate(
        [_dot_nt(jnp.concatenate([q_rows(b, h) for h in range(N_HEADS)], axis=0).astype(_BF16),
                 k_ref[b].astype(_BF16)) for b in range(nb)], axis=0)
    r = lax.broadcasted_iota(jnp.int32, s.shape, 0)
    c = lax.broadcasted_iota(jnp.int32, s.shape, 1)
    same_head = ((r >> (nt.bit_length() - 1)) & (N_HEADS - 1)) == (c & (N_HEADS - 1))
    return jnp.where(same_head, s, -jnp.inf)


def _values_head_rows(p, v_ref, nb, nt, put):
    hq = N_HEADS * nt
    for b in range(nb):
        o = _dot(p[b * hq:(b + 1) * hq].astype(_BF16), v_ref[b].astype(_BF16))
        for h in range(N_HEADS):
            put(b, h, o[h * nt:(h + 1) * nt])


def _mixer_tile(x, pool_hist, conv_hist, h0, weights, nb, nt, first_pos, k_ref, v_ref, qo_scr, head_rows):
    rows = nb * nt
    vecs, w_in, w_gate, pool_w, w_ax, w_br, w_out = weights

    def vec(r, cols=slice(0, D_MODEL)):
        return vecs[r:r + 1, cols]

    n_gate_chunks = N_BRANCH * D_MODEL // _MXU_COLS
    gate = [None] * n_gate_chunks
    xn = _rmsnorm(x, vec(_ROW_G_MIX)).astype(_BF16)

    def gate_chunk(j):
        sl = slice(j * _MXU_COLS, (j + 1) * _MXU_COLS)
        return _sigmoid(_dot(xn, w_gate[:, sl]) + vec(_ROW_B_GATE, sl))

    u_rnn = _dot(xn, w_in[:, D_POOL:D_POOL + D_RNN])
    u_pool = _dot(xn, w_in[:, 0:D_POOL])
    q = _dot(xn, w_in[:, D_POOL + D_RNN:])

    ext_r = jnp.concatenate([conv_hist, u_rnn], axis=0)
    xc = vec(_ROW_CONV_B)
    for j in range(RNN_CONV):
        xc = xc + ext_r[j * nb:j * nb + rows] * vec(_ROW_CONV_W + j)
    new_conv_hist = ext_r[rows:]
    xcb = xc.astype(_BF16)
    log_a_unit = -RG_C * _softplus(-vec(_ROW_LAMBDA))
    cols, h_last = [], []
    for n in range(N_RNN_BLOCKS):
        sl = slice(n * RNN_BLOCK, (n + 1) * RNN_BLOCK)
        ri = _dot(xcb[:, sl], w_ax[n])
        r = _sigmoid(ri[:, :RNN_BLOCK] + vec(_ROW_B_A, sl))
        i = _sigmoid(ri[:, RNN_BLOCK:] + vec(_ROW_B_X, sl))
        log_a = r * log_a_unit[:, sl]
        a = jnp.exp(log_a)
        th = jnp.tanh(log_a)
        b = jnp.sqrt(-2.0 * th / (1.0 - th)) * (i * xc[:, sl])
        h = h0[:, sl]
        hs = []
        for t in range(nt):
            h = a[t * nb:(t + 1) * nb] * h + b[t * nb:(t + 1) * nb]
            hs.append(h)
        cols.append(jnp.concatenate(hs, axis=0))
        h_last.append(h)
        for j in range(n * n_gate_chunks // N_RNN_BLOCKS, (n + 1) * n_gate_chunks // N_RNN_BLOCKS):
            gate[j] = gate_chunk(j)
    y_rnn = jnp.concatenate(cols, axis=1).astype(_BF16)
    new_h = jnp.concatenate(h_last, axis=1)

    for h in range(N_HEADS):
        qo_scr[0, h] = q[:, _head(h)]

    def q_rows(b, h):
        return qo_scr[0, h, pl.ds(b, nt, stride=nb), :]

    def put_rows(b, h, o):
        qo_scr[1, h, pl.ds(b, nt, stride=nb), :] = o

    s = _scores_head_rows(q_rows, k_ref, nb, nt) if head_rows else _scores_per_head(q_rows, k_ref, nb)
    rnn_branch = _dot(y_rnn, w_br[D_POOL:D_POOL + D_RNN, :])
    p = _softmax_rows(s * (HEAD_DIM ** -0.5))
    (_values_head_rows if head_rows else _values_per_head)(p, v_ref, nb, nt, put_rows)

    ext = jnp.concatenate([pool_hist, u_pool], axis=0)
    wins = _pool_window_sums(ext, rows, nb)
    if first_pos is not None:
        row = lax.broadcasted_iota(jnp.int32, (rows, POOL_GROUP), 0)
        pos1 = first_pos + (row >> (nb.bit_length() - 1)) + 1
        inv_pos1 = 1.0 / pos1.astype(_F32)
    ys = []
    for g, w in enumerate(POOL_WINDOWS):
        if first_pos is not None:
            mean = wins[g] * jnp.maximum(inv_pos1, 1.0 / w)
        else:
            mean = wins[g] * (1.0 / w)
        d = mean - u_pool[:, g * POOL_GROUP:(g + 1) * POOL_GROUP]
        ys.append(_dot(d.astype(_BF16), pool_w[g]))
    y_pool = (jnp.concatenate(ys, axis=1) * vec(_ROW_POOL_SCALE, slice(0, D_POOL))).astype(_BF16)
    new_pool_hist = ext[rows:]
    pool_branch = _dot(y_pool, w_br[0:D_POOL, :])

    y_attn = jnp.concatenate([qo_scr[1, h] for h in range(N_HEADS)], axis=1).astype(_BF16)

    out = x
    chunks_per_branch = D_MODEL // _MXU_COLS
    for c in range(chunks_per_branch):
        sl = slice(c * _MXU_COLS, (c + 1) * _MXU_COLS)
        branches = (pool_branch[:, sl], rnn_branch[:, sl], _dot(y_attn, w_br[D_POOL + D_RNN:, sl]))
        merged = None
        for k, branch in enumerate(branches):
            term = gate[k * chunks_per_branch + c] * branch
            merged = term if merged is None else merged + term
        out = out + _dot(merged.astype(_BF16), w_out[sl, :])
    return out, new_pool_hist, new_conv_hist, new_h


def _round_slices(src_refs, dst_refs):
    for src, dst in zip(src_refs, dst_refs):
        dst[...] = src[...].astype(_BF16)


def _mixer_kernel(*refs, nb, nt, carried, pos0, natural_in=False, n_round=0):
    rows = nb * nt
    x_ref = refs[0]
    weights = refs[1:8]
    if carried:
        k_ref, v_ref = refs[8:10]
        out_ref, pool_st, conv_st, h_st = refs[10 + n_round:14 + n_round]
        _round_slices(refs[10:10 + n_round], refs[14 + n_round:14 + 2 * n_round])
        refs = refs[14 + 2 * n_round:]
        qo_scr = refs[0]
        step = pl.program_id(0)

        @pl.when(step == 0)
        def _():
            pool_st[...] = jnp.zeros_like(pool_st)
            conv_st[...] = jnp.zeros_like(conv_st)
            h_st[...] = jnp.zeros_like(h_st)

        x = _load_time_major(x_ref, refs[1], nb, nt) if natural_in else x_ref[...]
        first_pos = pos0 + step * nt if pos0 < POOL_BUF else None
        out, pool_hist, conv_hist, h = _mixer_tile(
            x, pool_st[...], conv_st[...], h_st[...], weights, nb, nt,
            first_pos, k_ref, v_ref, qo_scr, head_rows=False)
        out_ref[...] = out
        pool_st[...] = pool_hist
        conv_st[...] = conv_hist
        h_st[...] = h
    else:
        assert pos0 >= POOL_BUF
        (k_ref, v_ref, pool_in, conv_in, h_in, out_ref, pool_out, conv_out, h_out, qo_scr) = refs[8:]
        x = x_ref[...].reshape(rows, D_MODEL)
        pool_hist = jnp.concatenate(
            [jnp.zeros((nb, D_POOL), _F32), pool_in[...].reshape(POOL_BUF * nb, D_POOL)], axis=0)
        out, pool_hist, conv_hist, h = _mixer_tile(
            x, pool_hist,
            conv_in[...].reshape((RNN_CONV - 1) * nb, D_RNN), h_in[...], weights, nb, nt,
            None, k_ref, v_ref, qo_scr, head_rows=True)
        out_ref[...] = out.reshape(nt, nb, D_MODEL)
        pool_out[...] = pool_hist[nb:].reshape(POOL_BUF, nb, D_POOL)
        conv_out[...] = conv_hist.reshape(RNN_CONV - 1, nb, D_RNN)
        h_out[...] = h


def _ffn_tile(x, read_hist, write_hist, weights, nb, nt, act_scr):
    rows = nb * nt
    vecs, w_up, w_down = weights
    xn = _rmsnorm(x, vecs[_FROW_G_FFN:_FROW_G_FFN + 1, 0:D_MODEL]).astype(_BF16)
    for c in range(0, D_FF, _FF_CHUNK):
        sl = slice(c, c + _FF_CHUNK)
        g_pre = _dot(xn, w_up[:, sl])
        val = _dot(xn, w_up[:, D_FF + c:D_FF + c + _FF_CHUNK])
        ext = jnp.concatenate([read_hist(sl), g_pre], axis=0)
        g_conv = vecs[_FROW_CONV_B:_FROW_CONV_B + 1, sl]
        for j in range(FFN_CONV):
            g_conv = g_conv + ext[j * nb:j * nb + rows] * vecs[_FROW_CONV_W + j:_FROW_CONV_W + j + 1, sl]
        write_hist(sl, ext[rows:])
        act_scr[:, sl] = (jax.nn.gelu(g_conv) * val).astype(_BF16)
    return x + _dot(act_scr[...], w_down[...])


def _ffn_kernel(*refs, nb, nt, carried, final_norm, n_round=0):
    rows = nb * nt
    x_ref = refs[0]
    weights = refs[1:4]
    rest = list(refs[4:])
    g_final = rest.pop(0) if final_norm else None
    finish = (lambda y: _rmsnorm(y, g_final[...])) if final_norm else (lambda y: y)
    if carried:
        out_ref, st_out = rest[n_round:n_round + 2]
        _round_slices(rest[:n_round], rest[n_round + 2:2 * n_round + 2])
        rest = rest[2 * n_round + 2:]
        act_scr = rest[0]

        @pl.when(pl.program_id(0) == 0)
        def _():
            st_out[...] = jnp.zeros_like(st_out)

        def write_hist(sl, hist):
            st_out[:, sl] = hist

        x = x_ref[...]
        out = finish(_ffn_tile(x, lambda sl: st_out[:, sl], write_hist,
                               weights, nb, nt, act_scr))
        if final_norm:
            _store_from_time_major(out, out_ref, rest[1], nb, nt)
        else:
            out_ref[...] = out
    else:
        st_in, out_ref, st_out, act_scr = rest
        x = x_ref[...].reshape(rows, D_MODEL)

        def write_hist(sl, hist):
            st_out[:, :, sl] = hist.reshape(FFN_CONV - 1, nb, _FF_CHUNK)

        out = finish(_ffn_tile(
            x,
            lambda sl: st_in[:, :, sl].reshape((FFN_CONV - 1) * nb, _FF_CHUNK), write_hist,
            weights, nb, nt, act_scr))
        out_ref[...] = out.reshape(nt, nb, D_MODEL)


def _mem_kv_kernel(mem_ref, g_ref, wk_ref, wv_ref, k_ref, v_ref, kb_ref, vb_ref):
    n_mem = mem_ref.shape[1]
    for b in range(mem_ref.shape[0]):
        mn = _rmsnorm(mem_ref[b], g_ref[...]).astype(_BF16)
        for w_ref, out_ref, out_bf_ref in ((wk_ref, k_ref, kb_ref), (wv_ref, v_ref, vb_ref)):
            kv = _dot(mn, w_ref[...])
            for h in range(N_HEADS):
                out_ref[b, pl.ds(h, n_mem, stride=N_HEADS), :] = kv[:, _head(h)]
            if out_bf_ref is kb_ref:
                for h in range(N_HEADS):
                    kb_ref[b, h] = kv[:, _head(h)].T.astype(_BF16)
            else:
                out_bf_ref[b] = kv.astype(_BF16)


def _resident(shape):
    return pl.BlockSpec(shape, lambda *_: (0,) * len(shape), pipeline_mode=pl.Buffered(1))


def _params(semantics, vmem_limit_bytes=_VMEM_LIMIT_BYTES):
    return pltpu.CompilerParams(dimension_semantics=semantics, vmem_limit_bytes=vmem_limit_bytes)


def _layer_resident(a, layer):
    index = layer if a.shape[0] > 1 else 0
    return pl.BlockSpec((None,) + a.shape[1:], lambda *_: (index,) + (0,) * (a.ndim - 1),
                        pipeline_mode=pl.Buffered(1))


def _round_specs(mats, layer, n_steps):
    in_specs, out_specs, out_shapes = [], [], []
    for m in mats:
        rows = m.shape[1] // n_steps
        in_specs.append(pl.BlockSpec((None, rows, m.shape[2]), lambda j: (layer, j, 0)))
        out_specs.append(pl.BlockSpec((None, rows, m.shape[2]), lambda j: (0, j, 0)))
        out_shapes.append(jax.ShapeDtypeStruct((1,) + m.shape[1:], _BF16))
    return in_specs, out_specs, out_shapes


def _pack_rows(arrays, n_rows, width):
    rows = [a[:, None, :] if a.ndim == 2 else a for a in arrays]
    rows = [jnp.pad(r, ((0, 0), (0, 0), (0, width - r.shape[2]))) for r in rows]
    packed = jnp.concatenate(rows, axis=1)
    return jnp.pad(packed, ((0, 0), (0, n_rows - packed.shape[1]), (0, 0)))


def _mixer_weights(pw, layer):
    arrs = [pw[n] for n in ('mixer_vecs', 'w_in', 'w_gate', 'pool_w', 'rnn_wax', 'w_br', 'w_out')]
    return arrs, [_layer_resident(a, layer) for a in arrs]


def _mixer_prompt(h, pw, kb, vb, layer, n_seq, round_next=()):
    nt, nb = _PROMPT_STEPS, n_seq
    rows = nt * nb
    natural_in = h.ndim == 3
    n_tok = h.shape[0] * h.shape[1] if natural_in else h.shape[0]
    r_in, r_out, r_shapes = _round_specs(round_next, layer + 1, n_tok // rows)
    w_arrs, w_specs = _mixer_weights(pw, layer)
    tile = pl.BlockSpec((rows, D_MODEL), lambda j: (j, 0))
    in_tile = pl.BlockSpec((nb, nt, D_MODEL), lambda j: (0, j, 0)) if natural_in else tile
    scratch = [pltpu.VMEM((2, N_HEADS, rows, HEAD_DIM), _F32)]
    if natural_in:
        scratch.append(pltpu.VMEM((D_MODEL // _LANES, rows, _LANES), _F32))
    out_shapes = (jax.ShapeDtypeStruct((n_tok, D_MODEL), _F32),
                  jax.ShapeDtypeStruct(((POOL_BUF + 1) * nb, D_POOL), _F32),
                  jax.ShapeDtypeStruct(((RNN_CONV - 1) * nb, D_RNN), _F32),
                  jax.ShapeDtypeStruct((nb, D_RNN), _F32))
    whole = lambda s: pl.BlockSpec(s.shape, lambda j: (0,) * len(s.shape))
    return pl.pallas_call(
        functools.partial(_mixer_kernel, nb=nb, nt=nt, carried=True, pos0=0, natural_in=natural_in,
                          n_round=len(round_next)),
        grid=(n_tok // rows,),
        in_specs=[in_tile] + w_specs + [_layer_resident(kb, layer), _layer_resident(vb, layer)] + r_in,
        out_specs=(tile,) + tuple(whole(s) for s in out_shapes[1:]) + tuple(r_out),
        out_shape=out_shapes + tuple(r_shapes),
        scratch_shapes=scratch,
        compiler_params=_params(("arbitrary",)),
        name="mixer_prompt",
    )(h, *w_arrs, kb, vb, *round_next)


def _mixer_sample(h, pw, layer, k, v, pool_st, conv_st, h_st):
    nt, n_seq = h.shape[0], h.shape[1]
    nb = _SAMPLE_MIXER_SEQS
    w_arrs, w_specs = _mixer_weights(pw, layer)
    seq3 = lambda a: pl.BlockSpec((a.shape[0], nb, a.shape[2]), lambda i: (0, i, 0))
    h_spec = pl.BlockSpec((nb, D_RNN), lambda i: (i, 0))
    kv_spec = pl.BlockSpec((None, nb) + k.shape[2:], lambda i: (layer, i, 0, 0))
    outs = (h, pool_st, conv_st, h_st)
    return pl.pallas_call(
        functools.partial(_mixer_kernel, nb=nb, nt=nt, carried=False, pos0=PAST_LEN),
        grid=(n_seq // nb,),
        in_specs=[seq3(h)] + w_specs + [kv_spec, kv_spec, seq3(pool_st), seq3(conv_st), h_spec],
        out_specs=(seq3(h), seq3(pool_st), seq3(conv_st), h_spec),
        out_shape=tuple(jax.ShapeDtypeStruct(a.shape, _F32) for a in outs),
        scratch_shapes=[pltpu.VMEM((2, N_HEADS, nt * nb, HEAD_DIM), _F32)],
        compiler_params=_params(("arbitrary",), _VMEM_LIMIT_SAMPLE_MIXER_BYTES),
        name="mixer_sample",
    )(h, *w_arrs, k, v, pool_st, conv_st, h_st)


def _ffn_weights(pw, layer, g_final):
    arrs = [pw[n] for n in ('ffn_vecs', 'w_up', 'w_down')]
    specs = [_layer_resident(a, layer) for a in arrs]
    if g_final is not None:
        arrs.append(g_final)
        specs.append(_resident(g_final.shape))
    return arrs, specs


def _ffn_prompt(h, pw, layer, n_seq, g_final, round_next=()):
    nt, nb = _FFN_PROMPT_STEPS, n_seq
    rows = nt * nb
    r_in, r_out, r_shapes = _round_specs(round_next, layer + 1, h.shape[0] // rows)
    w_arrs, w_specs = _ffn_weights(pw, layer, g_final)
    tile = pl.BlockSpec((rows, D_MODEL), lambda j: (j, 0))
    st_shape = ((FFN_CONV - 1) * nb, D_FF)
    scratch = [pltpu.VMEM((rows, D_FF), _BF16)]
    if g_final is None:
        out_tile, out_shape = tile, h.shape
    else:
        out_tile = pl.BlockSpec((nb, nt, D_MODEL), lambda j: (0, j, 0))
        out_shape = (nb, h.shape[0] // nb, D_MODEL)
        scratch.append(pltpu.VMEM((D_MODEL // _LANES, rows, _LANES), _F32))
    return pl.pallas_call(
        functools.partial(_ffn_kernel, nb=nb, nt=nt, carried=True, final_norm=g_final is not None,
                          n_round=len(round_next)),
        grid=(h.shape[0] // rows,),
        in_specs=[tile] + w_specs + r_in,
        out_specs=(out_tile, pl.BlockSpec(st_shape, lambda j: (0, 0))) + tuple(r_out),
        out_shape=(jax.ShapeDtypeStruct(out_shape, _F32), jax.ShapeDtypeStruct(st_shape, _F32))
        + tuple(r_shapes),
        scratch_shapes=scratch,
        compiler_params=_params(("arbitrary",)),
        name="ffn_prompt",
    )(h, *w_arrs, *round_next)


def _ffn_sample(h, pw, layer, st, g_final):
    nt, n_seq = h.shape[0], h.shape[1]
    nb = _SAMPLE_SEQS
    w_arrs, w_specs = _ffn_weights(pw, layer, g_final)
    seq3 = lambda a: pl.BlockSpec((a.shape[0], nb, a.shape[2]), lambda i: (0, i, 0))
    return pl.pallas_call(
        functools.partial(_ffn_kernel, nb=nb, nt=nt, carried=False, final_norm=g_final is not None),
        grid=(n_seq // nb,),
        in_specs=[seq3(h)] + w_specs + [seq3(st)],
        out_specs=(seq3(h), seq3(st)),
        out_shape=(jax.ShapeDtypeStruct(h.shape, _F32), jax.ShapeDtypeStruct(st.shape, _F32)),
        scratch_shapes=[pltpu.VMEM((nt * nb, D_FF), _BF16)],
        compiler_params=_params(("arbitrary",)),
        name="ffn_sample",
    )(h, *w_arrs, st)


def _mem_kv(mem, g_mem, w_k, w_v):
    n_seq, n_mem = mem.shape[0], mem.shape[1]
    per_layer = lambda a: pl.BlockSpec((None,) + a.shape[1:], lambda l: (l,) + (0,) * (a.ndim - 1))
    out_shapes = ((DEPTH, n_seq, n_mem * N_HEADS, HEAD_DIM),) * 2 + (
        (DEPTH, n_seq, N_HEADS, HEAD_DIM, n_mem), (DEPTH, n_seq, n_mem, D_Q))
    dtypes = (_F32, _F32, _BF16, _BF16)
    return pl.pallas_call(
        _mem_kv_kernel,
        grid=(DEPTH,),
        in_specs=[pl.BlockSpec(mem.shape, lambda l: (0, 0, 0)), per_layer(g_mem), per_layer(w_k), per_layer(w_v)],
        out_specs=tuple(pl.BlockSpec((None,) + s[1:], lambda l, nd=len(s): (l,) + (0,) * (nd - 1))
                        for s in out_shapes),
        out_shape=tuple(jax.ShapeDtypeStruct(s, d) for s, d in zip(out_shapes, dtypes)),
        compiler_params=_params(("arbitrary",)),
        name="mem_kv",
    )(mem, g_mem, w_k, w_v)


def kernel(x_prompt, x_sample, mem_prompt, cache_mem_k, cache_mem_v, state_pool, state_rnn_conv, state_rnn_h, state_ffn_conv, g_mix, w_in, w_gate, b_gate, pool_w, pool_scale, rnn_conv_w, rnn_conv_b, rnn_wa, rnn_ba, rnn_wx, rnn_bx, rnn_lambda, g_mem, w_k, w_v, w_br_pool, w_br_rnn, w_br_attn, w_out, g_ffn, w_up, ffn_conv_w, ffn_conv_b, w_down, g_final):
    bp = x_prompt.shape[0]
    bs, ts, _ = x_sample.shape
    n_mem = mem_prompt.shape[1]
    bf = lambda a: a.astype(_BF16)
    pw = {
        'mixer_vecs': _pack_rows([b_gate, g_mix, pool_scale, rnn_conv_b, rnn_ba, rnn_bx, rnn_lambda, rnn_conv_w],
                                 _MIXER_VEC_ROWS, N_BRANCH * D_MODEL),
        'pool_w': bf(pool_w), 'rnn_wax': bf(jnp.concatenate([rnn_wa, rnn_wx], axis=-1)),
        'w_br': bf(jnp.concatenate([w_br_pool, w_br_rnn, w_br_attn], axis=1)), 'w_out': bf(w_out),
        'ffn_vecs': _pack_rows([ffn_conv_w, ffn_conv_b, g_ffn], _FFN_VEC_ROWS, D_FF),
        'w_in': bf(w_in[:1]), 'w_gate': bf(w_gate[:1]), 'w_up': bf(w_up[:1]), 'w_down': bf(w_down[:1]),
    }
    g_fin = g_final[None, :]

    p_mk, p_mv, kb, vb = _mem_kv(mem_prompt, g_mem[:, None, :], bf(w_k), bf(w_v))
    cache_k = cache_mem_k.reshape(DEPTH, bs, n_mem * N_HEADS, HEAD_DIM)
    cache_v = cache_mem_v.reshape(DEPTH, bs, n_mem * N_HEADS, HEAD_DIM)

    time_major = lambda a: jnp.swapaxes(a, 0, 1)
    hp = x_prompt
    hs = time_major(x_sample)
    p_pool, p_rconv, p_rh, p_fconv = [], [], [], []
    s_pool, s_rconv, s_rh, s_fconv = [], [], [], []
    for l in range(DEPTH):
        last = l == DEPTH - 1
        hp, pool_st, conv_st, h_st, *next_ffn = _mixer_prompt(
            hp, pw, kb, vb, l, bp, round_next=() if last else (w_up, w_down))
        hp, ffn_st, *next_mixer = _ffn_prompt(
            hp, pw, l, bp, g_fin if last else None, round_next=() if last else (w_in, w_gate))
        p_pool.append(time_major(pool_st.reshape(POOL_BUF + 1, bp, D_POOL)[1:]))
        p_rconv.append(time_major(conv_st.reshape(RNN_CONV - 1, bp, D_RNN)))
        p_rh.append(h_st)
        p_fconv.append(time_major(ffn_st.reshape(FFN_CONV - 1, bp, D_FF)))

        hs, pool_o, conv_o, h_o = _mixer_sample(
            hs, pw, l, cache_k, cache_v, time_major(state_pool[l]), time_major(state_rnn_conv[l]),
            state_rnn_h[l])
        hs, ffn_o = _ffn_sample(hs, pw, l, time_major(state_ffn_conv[l]), g_fin if last else None)
        s_pool.append(time_major(pool_o))
        s_rconv.append(time_major(conv_o))
        s_rh.append(h_o)
        s_fconv.append(time_major(ffn_o))
        if not last:
            pw = dict(pw, w_up=next_ffn[0], w_down=next_ffn[1], w_in=next_mixer[0], w_gate=next_mixer[1])

    head = (N_HEADS, HEAD_DIM)
    return (hp, time_major(hs),
            jnp.stack(p_pool), jnp.stack(p_rconv), jnp.stack(p_rh), jnp.stack(p_fconv),
            p_mk.reshape(DEPTH, bp, n_mem, *head), p_mv.reshape(DEPTH, bp, n_mem, *head),
            jnp.stack(s_pool), jnp.stack(s_rconv), jnp.stack(s_rh), jnp.stack(s_fconv))
```

```python
import functools

import jax
import jax.numpy as jnp
from jax import lax
from jax.experimental import pallas as pl
from jax.experimental.pallas import tpu as pltpu

D_MODEL = 1024
DEPTH = 2
PAST_LEN = 16384
POOL_WINDOWS = (2, 4, 8, 16)
D_POOL = D_MODEL // 2
POOL_GROUP = D_POOL // len(POOL_WINDOWS)
POOL_BUF = max(POOL_WINDOWS) - 1
D_RNN = D_MODEL
N_RNN_BLOCKS = 8
RNN_BLOCK = D_RNN // N_RNN_BLOCKS
RNN_CONV = 4
RG_C = 8.0
N_HEADS = 4
HEAD_DIM = 128
D_Q = N_HEADS * HEAD_DIM
N_BRANCH = 3
D_FF = 3 * D_MODEL
FFN_CONV = 3
EPS = 1e-6

_V7X_VMEM_BYTES = 64 * 1024 * 1024
_VMEM_LIMIT_BYTES = _V7X_VMEM_BYTES - 8 * 1024 * 1024
_VMEM_LIMIT_SAMPLE_MIXER_BYTES = _V7X_VMEM_BYTES - 3 * 1024 * 1024

_PROMPT_STEPS = 64
_FFN_PROMPT_STEPS = 128
_SAMPLE_SEQS = 64
_SAMPLE_MIXER_SEQS = 16
_FF_CHUNK = 512

_MXU_COLS = 256
_LANES = 128

_ROW_B_GATE, _ROW_G_MIX, _ROW_POOL_SCALE, _ROW_CONV_B, _ROW_B_A, _ROW_B_X, _ROW_LAMBDA, _ROW_CONV_W = range(8)
_MIXER_VEC_ROWS = 16
_FROW_CONV_W, _FROW_CONV_B, _FROW_G_FFN = 0, FFN_CONV, FFN_CONV + 1
_FFN_VEC_ROWS = 8

_F32 = jnp.float32
_BF16 = jnp.bfloat16


def _dot(a, b):
    return jnp.dot(a, b, preferred_element_type=_F32)


def _dot_nt(a, b):
    return lax.dot_general(a, b, (((1,), (1,)), ((), ())), preferred_element_type=_F32)


def _rmsnorm(x, g):
    return x * lax.rsqrt(jnp.mean(x * x, axis=-1, keepdims=True) + EPS) * g


def _sigmoid(x):
    return 0.5 * jnp.tanh(0.5 * x) + 0.5


def _softplus(x):
    return jnp.maximum(x, 0.0) + jnp.log1p(jnp.exp(-jnp.abs(x)))


def _softmax_rows(s):
    e = jnp.exp(s - jnp.max(s, axis=-1, keepdims=True))
    return e / jnp.sum(e, axis=-1, keepdims=True)


def _head(h):
    return slice(h * HEAD_DIM, (h + 1) * HEAD_DIM)


def _pool_window_sums(ext, rows, nb):
    sums = []
    for g, w in enumerate(POOL_WINDOWS):
        s = ext[:, g * POOL_GROUP:(g + 1) * POOL_GROUP]
        off, k = 0, 1
        while k < w:
            s = s[k * nb:] + s[:-k * nb]
            off += k * nb
            k *= 2
        start = (POOL_BUF + 1) * nb - off
        sums.append(s[start:start + rows])
    return sums


def _load_time_major(x_ref, scr, nb, nt):
    for b in range(nb):
        for c in range(scr.shape[0]):
            scr[c, pl.ds(b, nt, stride=nb), :] = x_ref[b, :, c * _LANES:(c + 1) * _LANES]
    return jnp.concatenate([scr[c] for c in range(scr.shape[0])], axis=1)


def _store_from_time_major(y, y_ref, scr, nb, nt):
    for c in range(scr.shape[0]):
        scr[c] = y[:, c * _LANES:(c + 1) * _LANES]
    for b in range(nb):
        for c in range(scr.shape[0]):
            y_ref[b, :, c * _LANES:(c + 1) * _LANES] = scr[c, pl.ds(b, nt, stride=nb), :]


def _scores_per_head(q_rows, k_ref, seqs):
    return jnp.concatenate(
        [_dot(q_rows(b, h).astype(_BF16), k_ref[b, h]) for b in seqs for h in range(N_HEADS)], axis=0)


def _values_per_head(p, v_ref, seqs, nt, put):
    for i, b in enumerate(seqs):
        for h in range(N_HEADS):
            n = i * N_HEADS + h
            put(b, h, _dot(p[n * nt:(n + 1) * nt].astype(_BF16), v_ref[b, :, _head(h)]))


def _scores_head_rows(q_rows, k_ref, nb, nt):
    s = jnp.concatenate(
        [_dot_nt(jnp.concatenate([q_rows(b, h) for h in range(N_HEADS)], axis=0).astype(_BF16),
                 k_ref[b].astype(_BF16)) for b in range(nb)], axis=0)
    r = lax.broadcasted_iota(jnp.int32, s.shape, 0)
    c = lax.broadcasted_iota(jnp.int32, s.shape, 1)
    same_head = ((r >> (nt.bit_length() - 1)) & (N_HEADS - 1)) == (c & (N_HEADS - 1))
    return jnp.where(same_head, s, -jnp.inf)


def _values_head_rows(p, v_ref, nb, nt, put):
    hq = N_HEADS * nt
    for b in range(nb):
        o = _dot(p[b * hq:(b + 1) * hq].astype(_BF16), v_ref[b].astype(_BF16))
        for h in range(N_HEADS):
            put(b, h, o[h * nt:(h + 1) * nt])


def _mixer_tile(x, pool_hist, conv_hist, h0, weights, nb, nt, first_pos, k_ref, v_ref, qo_scr, head_rows):
    rows = nb * nt
    vecs, w_in, w_gate, pool_w, w_ax, w_br, w_out = weights

    def vec(r, cols=slice(0, D_MODEL)):
        return vecs[r:r + 1, cols]

    n_gate_chunks = N_BRANCH * D_MODEL // _MXU_COLS
    gate = [None] * n_gate_chunks
    xn = _rmsnorm(x, vec(_ROW_G_MIX)).astype(_BF16)

    def gate_chunk(j):
        sl = slice(j * _MXU_COLS, (j + 1) * _MXU_COLS)
        return _sigmoid(_dot(xn, w_gate[:, sl]) + vec(_ROW_B_GATE, sl))

    u_rnn = _dot(xn, w_in[:, D_POOL:D_POOL + D_RNN])
    u_pool = _dot(xn, w_in[:, 0:D_POOL])
    q = _dot(xn, w_in[:, D_POOL + D_RNN:])

    ext_r = jnp.concatenate([conv_hist, u_rnn], axis=0)
    xc = vec(_ROW_CONV_B)
    for j in range(RNN_CONV):
        xc = xc + ext_r[j * nb:j * nb + rows] * vec(_ROW_CONV_W + j)
    new_conv_hist = ext_r[rows:]
    xcb = xc.astype(_BF16)
    log_a_unit = -RG_C * _softplus(-vec(_ROW_LAMBDA))
    cols, h_last = [], []
    for n in range(N_RNN_BLOCKS):
        sl = slice(n * RNN_BLOCK, (n + 1) * RNN_BLOCK)
        ri = _dot(xcb[:, sl], w_ax[n])
        r = _sigmoid(ri[:, :RNN_BLOCK] + vec(_ROW_B_A, sl))
        i = _sigmoid(ri[:, RNN_BLOCK:] + vec(_ROW_B_X, sl))
        log_a = r * log_a_unit[:, sl]
        a = jnp.exp(log_a)
        th = jnp.tanh(log_a)
        b = jnp.sqrt(-2.0 * th / (1.0 - th)) * (i * xc[:, sl])
        h = h0[:, sl]
        hs = []
        for t in range(nt):
            h = a[t * nb:(t + 1) * nb] * h + b[t * nb:(t + 1) * nb]
            hs.append(h)
        cols.append(jnp.concatenate(hs, axis=0))
        h_last.append(h)
        for j in range(n * n_gate_chunks // N_RNN_BLOCKS, (n + 1) * n_gate_chunks // N_RNN_BLOCKS):
            gate[j] = gate_chunk(j)
    y_rnn = jnp.concatenate(cols, axis=1).astype(_BF16)
    new_h = jnp.concatenate(h_last, axis=1)

    for h in range(N_HEADS):
        qo_scr[0, h] = q[:, _head(h)]

    def q_rows(b, h):
        return qo_scr[0, h, pl.ds(b, nt, stride=nb), :]

    def put_rows(b, h, o):
        qo_scr[1, h, pl.ds(b, nt, stride=nb), :] = o

    if head_rows:
        s = _scores_head_rows(q_rows, k_ref, nb, nt)
        rnn_branch = _dot(y_rnn, w_br[D_POOL:D_POOL + D_RNN, :])
        _values_head_rows(_softmax_rows(s * (HEAD_DIM ** -0.5)), v_ref, nb, nt, put_rows)
    else:
        halves = (range(0, nb // 2), range(nb // 2, nb))
        scores = [_scores_per_head(q_rows, k_ref, seqs) for seqs in halves]
        rnn_branch = _dot(y_rnn, w_br[D_POOL:D_POOL + D_RNN, :])
        for seqs, s in zip(halves, scores):
            _values_per_head(_softmax_rows(s * (HEAD_DIM ** -0.5)), v_ref, seqs, nt, put_rows)

    ext = jnp.concatenate([pool_hist, u_pool], axis=0)
    wins = _pool_window_sums(ext, rows, nb)
    if first_pos is not None:
        row = lax.broadcasted_iota(jnp.int32, (rows, POOL_GROUP), 0)
        pos1 = first_pos + (row >> (nb.bit_length() - 1)) + 1
        inv_pos1 = 1.0 / pos1.astype(_F32)
    ys = []
    for g, w in enumerate(POOL_WINDOWS):
        if first_pos is not None:
            mean = wins[g] * jnp.maximum(inv_pos1, 1.0 / w)
        else:
            mean = wins[g] * (1.0 / w)
        d = mean - u_pool[:, g * POOL_GROUP:(g + 1) * POOL_GROUP]
        ys.append(_dot(d.astype(_BF16), pool_w[g]))
    y_pool = (jnp.concatenate(ys, axis=1) * vec(_ROW_POOL_SCALE, slice(0, D_POOL))).astype(_BF16)
    new_pool_hist = ext[rows:]
    pool_branch = _dot(y_pool, w_br[0:D_POOL, :])

    y_attn = jnp.concatenate([qo_scr[1, h] for h in range(N_HEADS)], axis=1).astype(_BF16)

    out = x
    chunks_per_branch = D_MODEL // _MXU_COLS
    for c in range(chunks_per_branch):
        sl = slice(c * _MXU_COLS, (c + 1) * _MXU_COLS)
        branches = (pool_branch[:, sl], rnn_branch[:, sl], _dot(y_attn, w_br[D_POOL + D_RNN:, sl]))
        merged = None
        for k, branch in enumerate(branches):
            term = gate[k * chunks_per_branch + c] * branch
            merged = term if merged is None else merged + term
        out = out + _dot(merged.astype(_BF16), w_out[sl, :])
    return out, new_pool_hist, new_conv_hist, new_h


def _round_slices(src_refs, dst_refs):
    for src, dst in zip(src_refs, dst_refs):
        dst[...] = src[...].astype(_BF16)


def _mixer_kernel(*refs, nb, nt, carried, pos0, natural_in=False, n_round=0):
    rows = nb * nt
    x_ref = refs[0]
    weights = refs[1:8]
    if carried:
        k_ref, v_ref = refs[8:10]
        out_ref, pool_st, conv_st, h_st = refs[10 + n_round:14 + n_round]
        _round_slices(refs[10:10 + n_round], refs[14 + n_round:14 + 2 * n_round])
        refs = refs[14 + 2 * n_round:]
        qo_scr = refs[0]
        step = pl.program_id(0)

        @pl.when(step == 0)
        def _():
            pool_st[...] = jnp.zeros_like(pool_st)
            conv_st[...] = jnp.zeros_like(conv_st)
            h_st[...] = jnp.zeros_like(h_st)

        x = _load_time_major(x_ref, refs[1], nb, nt) if natural_in else x_ref[...]
        first_pos = pos0 + step * nt if pos0 < POOL_BUF else None
        out, pool_hist, conv_hist, h = _mixer_tile(
            x, pool_st[...], conv_st[...], h_st[...], weights, nb, nt,
            first_pos, k_ref, v_ref, qo_scr, head_rows=False)
        out_ref[...] = out
        pool_st[...] = pool_hist
        conv_st[...] = conv_hist
        h_st[...] = h
    else:
        assert pos0 >= POOL_BUF
        (k_ref, v_ref, pool_in, conv_in, h_in, out_ref, pool_out, conv_out, h_out, qo_scr) = refs[8:]
        x = x_ref[...].reshape(rows, D_MODEL)
        pool_hist = jnp.concatenate(
            [jnp.zeros((nb, D_POOL), _F32), pool_in[...].reshape(POOL_BUF * nb, D_POOL)], axis=0)
        out, pool_hist, conv_hist, h = _mixer_tile(
            x, pool_hist,
            conv_in[...].reshape((RNN_CONV - 1) * nb, D_RNN), h_in[...], weights, nb, nt,
            None, k_ref, v_ref, qo_scr, head_rows=True)
        out_ref[...] = out.reshape(nt, nb, D_MODEL)
        pool_out[...] = pool_hist[nb:].reshape(POOL_BUF, nb, D_POOL)
        conv_out[...] = conv_hist.reshape(RNN_CONV - 1, nb, D_RNN)
        h_out[...] = h


def _ffn_tile(x, read_hist, write_hist, weights, nb, nt, act_scr):
    rows = nb * nt
    vecs, w_up, w_down = weights
    xn = _rmsnorm(x, vecs[_FROW_G_FFN:_FROW_G_FFN + 1, 0:D_MODEL]).astype(_BF16)
    for c in range(0, D_FF, _FF_CHUNK):
        sl = slice(c, c + _FF_CHUNK)
        g_pre = _dot(xn, w_up[:, sl])
        val = _dot(xn, w_up[:, D_FF + c:D_FF + c + _FF_CHUNK])
        ext = jnp.concatenate([read_hist(sl), g_pre], axis=0)
        g_conv = vecs[_FROW_CONV_B:_FROW_CONV_B + 1, sl]
        for j in range(FFN_CONV):
            g_conv = g_conv + ext[j * nb:j * nb + rows] * vecs[_FROW_CONV_W + j:_FROW_CONV_W + j + 1, sl]
        write_hist(sl, ext[rows:])
        act_scr[:, sl] = (jax.nn.gelu(g_conv) * val).astype(_BF16)
    return x + _dot(act_scr[...], w_down[...])


def _ffn_kernel(*refs, nb, nt, carried, final_norm, n_round=0):
    rows = nb * nt
    x_ref = refs[0]
    weights = refs[1:4]
    rest = list(refs[4:])
    g_final = rest.pop(0) if final_norm else None
    finish = (lambda y: _rmsnorm(y, g_final[...])) if final_norm else (lambda y: y)
    if carried:
        out_ref, st_out = rest[n_round:n_round + 2]
        _round_slices(rest[:n_round], rest[n_round + 2:2 * n_round + 2])
        rest = rest[2 * n_round + 2:]
        act_scr = rest[0]

        @pl.when(pl.program_id(0) == 0)
        def _():
            st_out[...] = jnp.zeros_like(st_out)

        def write_hist(sl, hist):
            st_out[:, sl] = hist

        x = x_ref[...]
        out = finish(_ffn_tile(x, lambda sl: st_out[:, sl], write_hist,
                               weights, nb, nt, act_scr))
        if final_norm:
            _store_from_time_major(out, out_ref, rest[1], nb, nt)
        else:
            out_ref[...] = out
    else:
        st_in, out_ref, st_out, act_scr = rest
        x = x_ref[...].reshape(rows, D_MODEL)

        def write_hist(sl, hist):
            st_out[:, :, sl] = hist.reshape(FFN_CONV - 1, nb, _FF_CHUNK)

        out = finish(_ffn_tile(
            x,
            lambda sl: st_in[:, :, sl].reshape((FFN_CONV - 1) * nb, _FF_CHUNK), write_hist,
            weights, nb, nt, act_scr))
        out_ref[...] = out.reshape(nt, nb, D_MODEL)


def _mem_kv_kernel(mem_ref, g_ref, wk_ref, wv_ref, k_ref, v_ref, kb_ref, vb_ref):
    n_mem = mem_ref.shape[1]
    for b in range(mem_ref.shape[0]):
        mn = _rmsnorm(mem_ref[b], g_ref[...]).astype(_BF16)
        for w_ref, out_ref, out_bf_ref in ((wk_ref, k_ref, kb_ref), (wv_ref, v_ref, vb_ref)):
            kv = _dot(mn, w_ref[...])
            for h in range(N_HEADS):
                out_ref[b, pl.ds(h, n_mem, stride=N_HEADS), :] = kv[:, _head(h)]
            if out_bf_ref is kb_ref:
                for h in range(N_HEADS):
                    kb_ref[b, h] = kv[:, _head(h)].T.astype(_BF16)
            else:
                out_bf_ref[b] = kv.astype(_BF16)


def _resident(shape):
    return pl.BlockSpec(shape, lambda *_: (0,) * len(shape), pipeline_mode=pl.Buffered(1))


def _params(semantics, vmem_limit_bytes=_VMEM_LIMIT_BYTES):
    return pltpu.CompilerParams(dimension_semantics=semantics, vmem_limit_bytes=vmem_limit_bytes)


def _layer_resident(a, layer):
    index = layer if a.shape[0] > 1 else 0
    return pl.BlockSpec((None,) + a.shape[1:], lambda *_: (index,) + (0,) * (a.ndim - 1),
                        pipeline_mode=pl.Buffered(1))


def _round_specs(mats, layer, n_steps):
    in_specs, out_specs, out_shapes = [], [], []
    for m in mats:
        rows = m.shape[1] // n_steps
        in_specs.append(pl.BlockSpec((None, rows, m.shape[2]), lambda j: (layer, j, 0)))
        out_specs.append(pl.BlockSpec((None, rows, m.shape[2]), lambda j: (0, j, 0)))
        out_shapes.append(jax.ShapeDtypeStruct((1,) + m.shape[1:], _BF16))
    return in_specs, out_specs, out_shapes


def _pack_rows(arrays, n_rows, width):
    rows = [a[:, None, :] if a.ndim == 2 else a for a in arrays]
    rows = [jnp.pad(r, ((0, 0), (0, 0), (0, width - r.shape[2]))) for r in rows]
    packed = jnp.concatenate(rows, axis=1)
    return jnp.pad(packed, ((0, 0), (0, n_rows - packed.shape[1]), (0, 0)))


def _mixer_weights(pw, layer):
    arrs = [pw[n] for n in ('mixer_vecs', 'w_in', 'w_gate', 'pool_w', 'rnn_wax', 'w_br', 'w_out')]
    return arrs, [_layer_resident(a, layer) for a in arrs]


def _mixer_prompt(h, pw, kb, vb, layer, n_seq, round_next=()):
    nt, nb = _PROMPT_STEPS, n_seq
    rows = nt * nb
    natural_in = h.ndim == 3
    n_tok = h.shape[0] * h.shape[1] if natural_in else h.shape[0]
    r_in, r_out, r_shapes = _round_specs(round_next, layer + 1, n_tok // rows)
    w_arrs, w_specs = _mixer_weights(pw, layer)
    tile = pl.BlockSpec((rows, D_MODEL), lambda j: (j, 0))
    in_tile = pl.BlockSpec((nb, nt, D_MODEL), lambda j: (0, j, 0)) if natural_in else tile
    scratch = [pltpu.VMEM((2, N_HEADS, rows, HEAD_DIM), _F32)]
    if natural_in:
        scratch.append(pltpu.VMEM((D_MODEL // _LANES, rows, _LANES), _F32))
    out_shapes = (jax.ShapeDtypeStruct((n_tok, D_MODEL), _F32),
                  jax.ShapeDtypeStruct(((POOL_BUF + 1) * nb, D_POOL), _F32),
                  jax.ShapeDtypeStruct(((RNN_CONV - 1) * nb, D_RNN), _F32),
                  jax.ShapeDtypeStruct((nb, D_RNN), _F32))
    whole = lambda s: pl.BlockSpec(s.shape, lambda j: (0,) * len(s.shape))
    return pl.pallas_call(
        functools.partial(_mixer_kernel, nb=nb, nt=nt, carried=True, pos0=0, natural_in=natural_in,
                          n_round=len(round_next)),
        grid=(n_tok // rows,),
        in_specs=[in_tile] + w_specs + [_layer_resident(kb, layer), _layer_resident(vb, layer)] + r_in,
        out_specs=(tile,) + tuple(whole(s) for s in out_shapes[1:]) + tuple(r_out),
        out_shape=out_shapes + tuple(r_shapes),
        scratch_shapes=scratch,
        compiler_params=_params(("arbitrary",)),
        name="mixer_prompt",
    )(h, *w_arrs, kb, vb, *round_next)


def _mixer_sample(h, pw, layer, k, v, pool_st, conv_st, h_st):
    nt, n_seq = h.shape[0], h.shape[1]
    nb = _SAMPLE_MIXER_SEQS
    w_arrs, w_specs = _mixer_weights(pw, layer)
    seq3 = lambda a: pl.BlockSpec((a.shape[0], nb, a.shape[2]), lambda i: (0, i, 0))
    h_spec = pl.BlockSpec((nb, D_RNN), lambda i: (i, 0))
    kv_spec = pl.BlockSpec((None, nb) + k.shape[2:], lambda i: (layer, i, 0, 0))
    outs = (h, pool_st, conv_st, h_st)
    return pl.pallas_call(
        functools.partial(_mixer_kernel, nb=nb, nt=nt, carried=False, pos0=PAST_LEN),
        grid=(n_seq // nb,),
        in_specs=[seq3(h)] + w_specs + [kv_spec, kv_spec, seq3(pool_st), seq3(conv_st), h_spec],
        out_specs=(seq3(h), seq3(pool_st), seq3(conv_st), h_spec),
        out_shape=tuple(jax.ShapeDtypeStruct(a.shape, _F32) for a in outs),
        scratch_shapes=[pltpu.VMEM((2, N_HEADS, nt * nb, HEAD_DIM), _F32)],
        compiler_params=_params(("arbitrary",), _VMEM_LIMIT_SAMPLE_MIXER_BYTES),
        name="mixer_sample",
    )(h, *w_arrs, k, v, pool_st, conv_st, h_st)


def _ffn_weights(pw, layer, g_final):
    arrs = [pw[n] for n in ('ffn_vecs', 'w_up', 'w_down')]
    specs = [_layer_resident(a, layer) for a in arrs]
    if g_final is not None:
        arrs.append(g_final)
        specs.append(_resident(g_final.shape))
    return arrs, specs


def _ffn_prompt(h, pw, layer, n_seq, g_final, round_next=()):
    nt, nb = _FFN_PROMPT_STEPS, n_seq
    rows = nt * nb
    r_in, r_out, r_shapes = _round_specs(round_next, layer + 1, h.shape[0] // rows)
    w_arrs, w_specs = _ffn_weights(pw, layer, g_final)
    tile = pl.BlockSpec((rows, D_MODEL), lambda j: (j, 0))
    st_shape = ((FFN_CONV - 1) * nb, D_FF)
    scratch = [pltpu.VMEM((rows, D_FF), _BF16)]
    if g_final is None:
        out_tile, out_shape = tile, h.shape
    else:
        out_tile = pl.BlockSpec((nb, nt, D_MODEL), lambda j: (0, j, 0))
        out_shape = (nb, h.shape[0] // nb, D_MODEL)
        scratch.append(pltpu.VMEM((D_MODEL // _LANES, rows, _LANES), _F32))
    return pl.pallas_call(
        functools.partial(_ffn_kernel, nb=nb, nt=nt, carried=True, final_norm=g_final is not None,
                          n_round=len(round_next)),
        grid=(h.shape[0] // rows,),
        in_specs=[tile] + w_specs + r_in,
        out_specs=(out_tile, pl.BlockSpec(st_shape, lambda j: (0, 0))) + tuple(r_out),
        out_shape=(jax.ShapeDtypeStruct(out_shape, _F32), jax.ShapeDtypeStruct(st_shape, _F32))
        + tuple(r_shapes),
        scratch_shapes=scratch,
        compiler_params=_params(("arbitrary",)),
        name="ffn_prompt",
    )(h, *w_arrs, *round_next)


def _ffn_sample(h, pw, layer, st, g_final):
    nt, n_seq = h.shape[0], h.shape[1]
    nb = _SAMPLE_SEQS
    w_arrs, w_specs = _ffn_weights(pw, layer, g_final)
    seq3 = lambda a: pl.BlockSpec((a.shape[0], nb, a.shape[2]), lambda i: (0, i, 0))
    return pl.pallas_call(
        functools.partial(_ffn_kernel, nb=nb, nt=nt, carried=False, final_norm=g_final is not None),
        grid=(n_seq // nb,),
        in_specs=[seq3(h)] + w_specs + [seq3(st)],
        out_specs=(seq3(h), seq3(st)),
        out_shape=(jax.ShapeDtypeStruct(h.shape, _F32), jax.ShapeDtypeStruct(st.shape, _F32)),
        scratch_shapes=[pltpu.VMEM((nt * nb, D_FF), _BF16)],
        compiler_params=_params(("arbitrary",)),
        name="ffn_sample",
    )(h, *w_arrs, st)


def _mem_kv(mem, g_mem, w_k, w_v):
    n_seq, n_mem = mem.shape[0], mem.shape[1]
    per_layer = lambda a: pl.BlockSpec((None,) + a.shape[1:], lambda l: (l,) + (0,) * (a.ndim - 1))
    out_shapes = ((DEPTH, n_seq, n_mem * N_HEADS, HEAD_DIM),) * 2 + (
        (DEPTH, n_seq, N_HEADS, HEAD_DIM, n_mem), (DEPTH, n_seq, n_mem, D_Q))
    dtypes = (_F32, _F32, _BF16, _BF16)
    return pl.pallas_call(
        _mem_kv_kernel,
        grid=(DEPTH,),
        in_specs=[pl.BlockSpec(mem.shape, lambda l: (0, 0, 0)), per_layer(g_mem), per_layer(w_k), per_layer(w_v)],
        out_specs=tuple(pl.BlockSpec((None,) + s[1:], lambda l, nd=len(s): (l,) + (0,) * (nd - 1))
                        for s in out_shapes),
        out_shape=tuple(jax.ShapeDtypeStruct(s, d) for s, d in zip(out_shapes, dtypes)),
        compiler_params=_params(("arbitrary",)),
        name="mem_kv",
    )(mem, g_mem, w_k, w_v)


def kernel(x_prompt, x_sample, mem_prompt, cache_mem_k, cache_mem_v, state_pool, state_rnn_conv, state_rnn_h, state_ffn_conv, g_mix, w_in, w_gate, b_gate, pool_w, pool_scale, rnn_conv_w, rnn_conv_b, rnn_wa, rnn_ba, rnn_wx, rnn_bx, rnn_lambda, g_mem, w_k, w_v, w_br_pool, w_br_rnn, w_br_attn, w_out, g_ffn, w_up, ffn_conv_w, ffn_conv_b, w_down, g_final):
    bp = x_prompt.shape[0]
    bs, ts, _ = x_sample.shape
    n_mem = mem_prompt.shape[1]
    bf = lambda a: a.astype(_BF16)
    pw = {
        'mixer_vecs': _pack_rows([b_gate, g_mix, pool_scale, rnn_conv_b, rnn_ba, rnn_bx, rnn_lambda, rnn_conv_w],
                                 _MIXER_VEC_ROWS, N_BRANCH * D_MODEL),
        'pool_w': bf(pool_w), 'rnn_wax': bf(jnp.concatenate([rnn_wa, rnn_wx], axis=-1)),
        'w_br': bf(jnp.concatenate([w_br_pool, w_br_rnn, w_br_attn], axis=1)), 'w_out': bf(w_out),
        'ffn_vecs': _pack_rows([ffn_conv_w, ffn_conv_b, g_ffn], _FFN_VEC_ROWS, D_FF),
        'w_in': bf(w_in[:1]), 'w_gate': bf(w_gate[:1]), 'w_up': bf(w_up[:1]), 'w_down': bf(w_down[:1]),
    }
    g_fin = g_final[None, :]

    p_mk, p_mv, kb, vb = _mem_kv(mem_prompt, g_mem[:, None, :], bf(w_k), bf(w_v))
    cache_k = cache_mem_k.reshape(DEPTH, bs, n_mem * N_HEADS, HEAD_DIM)
    cache_v = cache_mem_v.reshape(DEPTH, bs, n_mem * N_HEADS, HEAD_DIM)

    time_major = lambda a: jnp.swapaxes(a, 0, 1)
    hp = x_prompt
    hs = time_major(x_sample)
    p_pool, p_rconv, p_rh, p_fconv = [], [], [], []
    s_pool, s_rconv, s_rh, s_fconv = [], [], [], []
    for l in range(DEPTH):
        last = l == DEPTH - 1
        hp, pool_st, conv_st, h_st, *next_ffn = _mixer_prompt(
            hp, pw, kb, vb, l, bp, round_next=() if last else (w_up, w_down))
        hp, ffn_st, *next_mixer = _ffn_prompt(
            hp, pw, l, bp, g_fin if last else None, round_next=() if last else (w_in, w_gate))
        p_pool.append(time_major(pool_st.reshape(POOL_BUF + 1, bp, D_POOL)[1:]))
        p_rconv.append(time_major(conv_st.reshape(RNN_CONV - 1, bp, D_RNN)))
        p_rh.append(h_st)
        p_fconv.append(time_major(ffn_st.reshape(FFN_CONV - 1, bp, D_FF)))

        hs, pool_o, conv_o, h_o = _mixer_sample(
            hs, pw, l, cache_k, cache_v, time_major(state_pool[l]), time_major(state_rnn_conv[l]),
            state_rnn_h[l])
        hs, ffn_o = _ffn_sample(hs, pw, l, time_major(state_ffn_conv[l]), g_fin if last else None)
        s_pool.append(time_major(pool_o))
        s_rconv.append(time_major(conv_o))
        s_rh.append(h_o)
        s_fconv.append(time_major(ffn_o))
        if not last:
            pw = dict(pw, w_up=next_ffn[0], w_down=next_ffn[1], w_in=next_mixer[0], w_gate=next_mixer[1])

    head = (N_HEADS, HEAD_DIM)
    return (hp, time_major(hs),
            jnp.stack(p_pool), jnp.stack(p_rconv), jnp.stack(p_rh), jnp.stack(p_fconv),
            p_mk.reshape(DEPTH, bp, n_mem, *head), p_mv.reshape(DEPTH, bp, n_mem, *head),
            jnp.stack(s_pool), jnp.stack(s_rconv), jnp.stack(s_rh), jnp.stack(s_fconv))
```
